```python
import math
import jax, jax.numpy as jnp
from jax import lax
import numpy as np

D_MODEL = 1024
BATCH = 16
SEQ = 256
DEPTH = 4
DEC_BATCH = 8
DEC_SEQ = 2048
PAST_LEN = 512

GRID_W = 64
EPS = 1e-6
SSD_D_INNER = D_MODEL
SSD_HEAD_DIM = 64
SSD_HEADS = SSD_D_INNER // SSD_HEAD_DIM
SSD_GROUPS = 2
SSD_D_STATE = 64
SSD_CONV_W = 5
SSD_CHUNK = 128
SSD_XBC = SSD_D_INNER + 2 * SSD_GROUPS * SSD_D_STATE
CONV_CH = D_MODEL // 2
CONV_W = 31
DA_HEADS = 4
DA_HEAD_DIM = 64
DA_V_DIM = 2 * DA_HEAD_DIM
DA_QK = DA_HEADS * 2 * DA_HEAD_DIM
DA_V = DA_HEADS * DA_V_DIM
Q_BLOCK = 128
ROPE_BASE = 10000.0
MOE_GROUPS = 4
MOE_EXPERTS_PER_GROUP = 8
MOE_N_EXPERTS = MOE_GROUPS * MOE_EXPERTS_PER_GROUP
MOE_TOP_K = 2
MOE_HIDDEN = 512
MOE_BLOCK = 128
N_BRANCHES = 3
IN_SIZES = (SSD_D_INNER, SSD_XBC, 2 * SSD_HEADS, 2 * CONV_CH, DA_QK, DA_QK, DA_V, N_BRANCHES * D_MODEL)
N_IN = sum(IN_SIZES)

kernel_name = 'hybrid_ssd_conv_diffattn_hmoe_dit_step'


def rmsnorm(x, g):
    xf = x.astype(jnp.float32)
    xf = xf * lax.rsqrt(jnp.mean(xf * xf, axis=-1, keepdims=True) + EPS)
    return xf.astype(x.dtype) * g


def layernorm(x, g, b):
    xf = x.astype(jnp.float32)
    mu = jnp.mean(xf, axis=-1, keepdims=True)
    var = jnp.mean(jnp.square(xf - mu), axis=-1, keepdims=True)
    return ((xf - mu) * lax.rsqrt(var + EPS)).astype(x.dtype) * g + b


def dwconv_centred(x, w, b):
    ch = x.shape[-1]
    pad = w.shape[0] // 2
    y = lax.conv_general_dilated(x, w[:, None, :].astype(x.dtype), (1,), [(pad, pad)],
                                 dimension_numbers=('NWC', 'WIO', 'NWC'), feature_group_count=ch)
    return y + b


def axial_rope_tables(n_tok, dtype):
    rows = n_tok // GRID_W
    row = jnp.repeat(jnp.arange(rows), GRID_W).astype(jnp.float32)
    col = jnp.tile(jnp.arange(GRID_W), rows).astype(jnp.float32)
    axis_dim = DA_HEAD_DIM // 2
    inv_freq = 1.0 / (ROPE_BASE ** (jnp.arange(0, axis_dim, 2, dtype=jnp.float32) / axis_dim))
    shape = (1, n_tok, 1, 1, axis_dim // 2)
    def cs(a):
        return jnp.cos(a).reshape(shape).astype(dtype), jnp.sin(a).reshape(shape).astype(dtype)
    return cs(row[:, None] * inv_freq), cs(col[:, None] * inv_freq)


def _rotate(x, cos, sin):
    x1, x2 = jnp.split(x, 2, axis=-1)
    return jnp.concatenate([x1 * cos - x2 * sin, x2 * cos + x1 * sin], axis=-1)


def apply_axial_rope(x, tables):
    (cr, sr), (cc, sc) = tables
    half = DA_HEAD_DIM // 2
    return jnp.concatenate([_rotate(x[..., :half], cr, sr), _rotate(x[..., half:], cc, sc)], axis=-1)


def ssd_chunked_scan(x, dt, a_neg, bm, cm, h0):
    b, L, h, p = x.shape
    g, n = bm.shape[2], bm.shape[3]
    q = SSD_CHUNK
    nc = L // q
    dtype = x.dtype
    b_h = jnp.repeat(bm, h // g, axis=2).reshape(b, nc, q, h, n)
    c_h = jnp.repeat(cm, h // g, axis=2).reshape(b, nc, q, h, n)
    xdt = (x * dt[..., None]).reshape(b, nc, q, h, p)
    la = jnp.cumsum((dt.astype(jnp.float32) * a_neg.astype(jnp.float32)).reshape(b, nc, q, h), axis=2)
    causal = jnp.tril(jnp.ones((q, q), dtype=bool))[None, None, :, :, None]
    seg = la[:, :, :, None, :] - la[:, :, None, :, :]
    decay_in = jnp.exp(jnp.where(causal, seg, -jnp.inf)).astype(dtype)
    scores = jnp.einsum('bcihn,bcjhn->bcijh', c_h, b_h) * decay_in
    y_diag = jnp.einsum('bcijh,bcjhp->bcihp', scores, xdt)
    decay_end = jnp.exp(la[:, :, -1:, :] - la).astype(dtype)
    chunk_states = jnp.einsum('bcjhn,bcjh,bcjhp->bchpn', b_h, decay_end, xdt)
    chunk_decay = jnp.exp(la[:, :, -1, :]).astype(dtype)
    def step(carry, inp):
        st, dec = inp
        return (carry * dec[:, :, None, None] + st).astype(carry.dtype), carry
    h_final, h_in = lax.scan(step, h0.astype(dtype),
                             (jnp.moveaxis(chunk_states, 1, 0), jnp.moveaxis(chunk_decay, 1, 0)))
    h_in = jnp.moveaxis(h_in, 0, 1)
    y_off = jnp.einsum('bcihn,bchpn->bcihp', c_h, h_in) * jnp.exp(la).astype(dtype)[..., None]
    return (y_diag + y_off).reshape(b, L, h, p), h_final


def ssd_branch(z, xbc, dt_raw, lp, h0):
    b, L, _ = z.shape
    xbc = jax.nn.silu(dwconv_centred(xbc, lp['ssd_conv_w'], lp['ssd_conv_b']))
    gn = SSD_GROUPS * SSD_D_STATE
    xs = xbc[..., :SSD_D_INNER].reshape(b, L, SSD_HEADS, SSD_HEAD_DIM)
    bm = xbc[..., SSD_D_INNER:SSD_D_INNER + gn].reshape(b, L, SSD_GROUPS, SSD_D_STATE)
    cm = xbc[..., SSD_D_INNER + gn:].reshape(b, L, SSD_GROUPS, SSD_D_STATE)
    dt = jax.nn.softplus(dt_raw.reshape(b, L, 2, SSD_HEADS) + lp['ssd_dt_bias'])
    a_neg = -jnp.exp(lp['ssd_a_log'])
    y_f, h_f = ssd_chunked_scan(xs, dt[:, :, 0], a_neg[0], bm, cm, h0[:, 0])
    rev = lambda t: jnp.flip(t, axis=1)
    y_b, h_b = ssd_chunked_scan(rev(xs), rev(dt[:, :, 1]), a_neg[1], rev(bm), rev(cm), h0[:, 1])
    y = y_f + rev(y_b) + xs * lp['ssd_d'][:, None]
    y = rmsnorm(y.reshape(b, L, SSD_D_INNER) * jax.nn.silu(z), lp['ssd_norm_g'])
    return y @ lp['w_br_ssd'], jnp.stack([h_f, h_b], axis=1)


def conv_branch(glu, lp):
    u = glu[..., :CONV_CH] * jax.nn.sigmoid(glu[..., CONV_CH:])
    u = dwconv_centred(u, lp['cv_dw_w'], lp['cv_dw_b'])
    u = jax.nn.silu(layernorm(u, lp['cv_ln_g'], lp['cv_ln_b']))
    return u @ lp['w_br_conv']


def blocked_diff_attention(q, k, v, lam):
    b, lq, nh, _, dh = q.shape
    nb = lq // Q_BLOCK
    scale = dh ** -0.5
    q_blocks = jnp.moveaxis(q.reshape(b, nb, Q_BLOCK, nh, 2, dh), 1, 0)
    def one_block(qb):
        s = jnp.einsum('bqhcd,bkhcd->cbhqk', qb, k).astype(jnp.float32) * scale
        p = jax.nn.softmax(s, axis=-1)
        a = (p[0] - lam * p[1]).astype(v.dtype)
        return jnp.einsum('bhqk,bkhe->bqhe', a, v)
    out = lax.map(one_block, q_blocks)
    return jnp.moveaxis(out, 0, 1).reshape(b, lq, nh, v.shape[-1])


def diff_attn_branch(q, k, v, lp, lam, lam_init, ctx_kv):
    b, L, _ = q.shape
    q = q.reshape(b, L, DA_HEADS, 2, DA_HEAD_DIM)
    k = k.reshape(b, L, DA_HEADS, 2, DA_HEAD_DIM)
    v = v.reshape(b, L, DA_HEADS, DA_V_DIM)
    if ctx_kv is None:
        kk, vv = k, v
        new_kv = (k.reshape(b, L, DA_HEADS, 2 * DA_HEAD_DIM), v)
    else:
        tables = axial_rope_tables(L, q.dtype)
        q = apply_axial_rope(q, tables)
        ck, cv = ctx_kv
        ck = ck.reshape(b, ck.shape[1], DA_HEADS, 2, DA_HEAD_DIM).astype(q.dtype)
        kk = jnp.concatenate([ck, apply_axial_rope(k, tables)], axis=1)
        vv = jnp.concatenate([cv.astype(v.dtype), v], axis=1)
        new_kv = None
    o = blocked_diff_attention(q, kk, vv, lam)
    o = rmsnorm(o, lp['da_subln_g']) * (1.0 - lam_init)
    return o.reshape(b, L, DA_V) @ lp['w_br_attn'], new_kv


def token_mixer(h, lp, lam, lam_init, ctx):
    b = h.shape[0]
    proj = h @ lp['w_in'] + lp['b_in']
    splits = [int(s) for s in np.cumsum(IN_SIZES)[:-1]]
    z, xbc, dt_raw, glu, q, k, v, gates = jnp.split(proj, splits, axis=-1)
    if ctx is None:
        h0 = jnp.zeros((b, 2, SSD_HEADS, SSD_HEAD_DIM, SSD_D_STATE), h.dtype)
        ctx_kv = None
    else:
        ctx_k, ctx_v, h0 = ctx
        ctx_kv = (ctx_k, ctx_v)
    br_a, ssd_state = ssd_branch(z, xbc, dt_raw, lp, h0)
    br_b = conv_branch(glu, lp)
    br_c, new_kv = diff_attn_branch(q, k, v, lp, lam, lam_init, ctx_kv)
    g_a, g_b, g_c = jnp.split(jax.nn.sigmoid(gates), N_BRANCHES, axis=-1)
    out = (g_a * br_a + g_b * br_b + g_c * br_c) @ lp['w_out']
    if ctx is None:
        return out, (new_kv[0], new_kv[1], ssd_state)
    return out, None


def moe_dispatch(xf, expert_idx, weights, w_gate, w_up, w_down):
    t, d = xf.shape
    k = expert_idx.shape[1]
    e = w_gate.shape[0]
    a = t * k
    flat_e = expert_idx.reshape(-1).astype(jnp.int32)
    order = jnp.argsort(flat_e)
    sorted_e = flat_e[order]
    tok = (order // k).astype(jnp.int32)
    counts = jnp.bincount(flat_e, length=e)
    starts = jnp.cumsum(counts) - counts
    pcounts = (counts + MOE_BLOCK - 1) // MOE_BLOCK * MOE_BLOCK
    pends = jnp.cumsum(pcounts)
    pstarts = pends - pcounts
    dest = pstarts[sorted_e] + jnp.arange(a, dtype=jnp.int32) - starts[sorted_e]
    n_blocks = -(-a // MOE_BLOCK) + e
    slot_tok = jnp.full((n_blocks * MOE_BLOCK,), t, dtype=jnp.int32).at[dest].set(tok)
    xpad = jnp.concatenate([xf, jnp.zeros((1, d), xf.dtype)], axis=0)
    xs = xpad[slot_tok].reshape(n_blocks, MOE_BLOCK, d)
    block_e = jnp.minimum(jnp.searchsorted(pends, jnp.arange(n_blocks) * MOE_BLOCK, side='right'), e - 1)
    def expert_block(args):
        xb, ei = args
        hid = jax.nn.silu(xb @ w_gate[ei]) * (xb @ w_up[ei])
        return hid @ w_down[ei]
    ys = lax.map(expert_block, (xs, block_e)).reshape(n_blocks * MOE_BLOCK, d)
    y_assign = ys[dest] * weights.reshape(-1)[order][:, None]
    return jax.ops.segment_sum(y_assign, tok, num_segments=t)


def hier_moe(h, lp):
    b, L, d = h.shape
    xf = h.reshape(-1, d)
    t = xf.shape[0]
    rows = jnp.arange(t)
    g_logits = (xf @ lp['moe_w_group'] + lp['moe_b_group']).astype(jnp.float32)
    g_prob = jax.nn.softmax(g_logits, axis=-1)
    g_sel = jnp.argmax(g_logits, axis=-1)
    p_g = g_prob[rows, g_sel]
    e_logits = (xf @ lp['moe_w_expert'] + lp['moe_b_expert']).astype(jnp.float32)
    e_in = e_logits.reshape(t, MOE_GROUPS, MOE_EXPERTS_PER_GROUP)[rows, g_sel]
    top_v, top_i = lax.top_k(e_in, MOE_TOP_K)
    weights = (p_g[:, None] * jax.nn.softmax(top_v, axis=-1)).astype(xf.dtype)
    expert_idx = g_sel[:, None].astype(jnp.int32) * MOE_EXPERTS_PER_GROUP + top_i
    y = moe_dispatch(xf, expert_idx, weights, lp['moe_w_gate'], lp['moe_w_up'], lp['moe_w_down'])
    return y.reshape(b, L, d)


def trunk_layer(x, mod, lp, lam, lam_init, ctx):
    shift1, scale1, gate1, shift2, scale2, gate2 = jnp.split(mod, 6, axis=-1)
    h = rmsnorm(x, lp['norm1_g']) * (1.0 + scale1) + shift1
    mixed, new_ctx = token_mixer(h, lp, lam, lam_init, ctx)
    x = x + gate1 * mixed
    h = rmsnorm(x, lp['norm2_g']) * (1.0 + scale2) + shift2
    x = x + gate2 * hier_moe(h, lp)
    return x, new_ctx


def setup_inputs(seed: int = 0) -> dict:
    key = jax.random.key(seed)
    ks = iter(jax.random.split(key, 48))
    def nrm(shape, scale):
        return scale * jax.random.normal(next(ks), shape, jnp.float32)
    L, D = DEPTH, D_MODEL
    E, F = MOE_N_EXPERTS, MOE_HIDDEN
    dt0 = jnp.exp(jax.random.uniform(next(ks), (L, 2, SSD_HEADS), jnp.float32, math.log(1e-3), math.log(1e-1)))
    return {
        'x_prompt': nrm((BATCH, SEQ, D), 1.0),
        'x_sample': nrm((DEC_BATCH, DEC_SEQ, D), 1.0),
        'cache_k': nrm((DEC_BATCH, L, PAST_LEN, DA_HEADS, 2 * DA_HEAD_DIM), 1.0),
        'cache_v': nrm((DEC_BATCH, L, PAST_LEN, DA_HEADS, DA_V_DIM), 1.0),
        'state_ssd': nrm((DEC_BATCH, L, 2, SSD_HEADS, SSD_HEAD_DIM, SSD_D_STATE), 0.3),
        'c': nrm((DEC_BATCH, D), 1.0),
        'c_ctx': nrm((D,), 1.0),
        'w_ada': nrm((L, D, 6 * D), 0.5 * D ** -0.5),
        'b_ada': nrm((L, 6 * D), 0.02),
        'norm1_g': 1.0 + nrm((L, D), 0.05),
        'norm2_g': 1.0 + nrm((L, D), 0.05),
        'w_in': nrm((L, D, N_IN), D ** -0.5),
        'b_in': nrm((L, N_IN), 0.02),
        'ssd_conv_w': nrm((L, SSD_CONV_W, SSD_XBC), SSD_CONV_W ** -0.5),
        'ssd_conv_b': nrm((L, SSD_XBC), 0.02),
        'ssd_dt_bias': dt0 + jnp.log(-jnp.expm1(-dt0)),
        'ssd_a_log': jnp.log(jax.random.uniform(next(ks), (L, 2, SSD_HEADS), jnp.float32, 1.0, 16.0)),
        'ssd_d': 1.0 + nrm((L, SSD_HEADS), 0.1),
        'ssd_norm_g': 1.0 + nrm((L, SSD_D_INNER), 0.05),
        'w_br_ssd': nrm((L, SSD_D_INNER, D), SSD_D_INNER ** -0.5),
        'cv_dw_w': nrm((L, CONV_W, CONV_CH), CONV_W ** -0.5),
        'cv_dw_b': nrm((L, CONV_CH), 0.02),
        'cv_ln_g': 1.0 + nrm((L, CONV_CH), 0.05),
        'cv_ln_b': nrm((L, CONV_CH), 0.02),
        'w_br_conv': nrm((L, CONV_CH, D), CONV_CH ** -0.5),
        'da_lambda': nrm((L, 4, DA_HEAD_DIM), 0.1),
        'da_subln_g': 1.0 + nrm((L, DA_V_DIM), 0.05),
        'w_br_attn': nrm((L, DA_V, D), DA_V ** -0.5),
        'w_out': nrm((L, D, D), D ** -0.5),
        'moe_w_group': nrm((L, D, MOE_GROUPS), D ** -0.5),
        'moe_b_group': nrm((L, MOE_GROUPS), 0.01),
        'moe_w_expert': nrm((L, D, E), D ** -0.5),
        'moe_b_expert': nrm((L, E), 0.01),
        'moe_w_gate': nrm((L, E, D, F), D ** -0.5),
        'moe_w_up': nrm((L, E, D, F), D ** -0.5),
        'moe_w_down': nrm((L, E, F, D), F ** -0.5),
        'final_g': 1.0 + nrm((D,), 0.05),
    }


def reference(x_prompt, x_sample, cache_k, cache_v, state_ssd, c, c_ctx, w_ada, b_ada, norm1_g, norm2_g,
              w_in, b_in, ssd_conv_w, ssd_conv_b, ssd_dt_bias, ssd_a_log, ssd_d, ssd_norm_g, w_br_ssd,
              cv_dw_w, cv_dw_b, cv_ln_g, cv_ln_b, w_br_conv, da_lambda, da_subln_g, w_br_attn, w_out,
              moe_w_group, moe_b_group, moe_w_expert, moe_b_expert, moe_w_gate, moe_w_up, moe_w_down,
              final_g):
    xp, xs = x_prompt, x_sample
    ks_new, vs_new, ss_new = [], [], []
    for l in range(DEPTH):
        lp = dict(norm1_g=norm1_g[l], norm2_g=norm2_g[l], w_in=w_in[l], b_in=b_in[l],
                  ssd_conv_w=ssd_conv_w[l], ssd_conv_b=ssd_conv_b[l], ssd_dt_bias=ssd_dt_bias[l],
                  ssd_a_log=ssd_a_log[l], ssd_d=ssd_d[l], ssd_norm_g=ssd_norm_g[l], w_br_ssd=w_br_ssd[l],
                  cv_dw_w=cv_dw_w[l], cv_dw_b=cv_dw_b[l], cv_ln_g=cv_ln_g[l], cv_ln_b=cv_ln_b[l],
                  w_br_conv=w_br_conv[l], da_subln_g=da_subln_g[l], w_br_attn=w_br_attn[l], w_out=w_out[l],
                  moe_w_group=moe_w_group[l], moe_b_group=moe_b_group[l], moe_w_expert=moe_w_expert[l],
                  moe_b_expert=moe_b_expert[l], moe_w_gate=moe_w_gate[l], moe_w_up=moe_w_up[l],
                  moe_w_down=moe_w_down[l])
        lam_init = 0.8 - 0.6 * math.exp(-0.3 * l)
        lq1, lk1, lq2, lk2 = da_lambda[l].astype(jnp.float32)
        lam = jnp.exp(jnp.sum(lq1 * lk1)) - jnp.exp(jnp.sum(lq2 * lk2)) + lam_init
        mod_ctx = (jax.nn.silu(c_ctx) @ w_ada[l] + b_ada[l])[None, None, :]
        mod_lat = (jax.nn.silu(c) @ w_ada[l] + b_ada[l])[:, None, :]
        xp, (k_new, v_new, s_new) = trunk_layer(xp, mod_ctx, lp, lam, lam_init, None)
        ks_new.append(k_new)
        vs_new.append(v_new)
        ss_new.append(s_new)
        xs, _ = trunk_layer(xs, mod_lat, lp, lam, lam_init, (cache_k[:, l], cache_v[:, l], state_ssd[:, l]))
    y_prompt = rmsnorm(xp, final_g)
    y_sample = rmsnorm(xs, final_g)
    new_cache_k = jnp.stack(ks_new, axis=1)
    new_cache_v = jnp.stack(vs_new, axis=1)
    new_state_ssd = jnp.stack(ss_new, axis=1)
    return (y_prompt, y_sample, new_cache_k, new_cache_v, new_state_ssd)
```

```python
import functools
import math

import jax
import jax.numpy as jnp
import numpy as np
from jax import lax
from jax.experimental import pallas as pl
from jax.experimental.pallas import tpu as pltpu

D = 1024
DEPTH = 4
N_PROMPT_SEQ, PROMPT_LEN = 16, 256
N_LATENT_SEQ, LATENT_LEN = 8, 2048
PAST_LEN = 512
T_P = N_PROMPT_SEQ * PROMPT_LEN
T_L = N_LATENT_SEQ * LATENT_LEN
T = T_P + T_L
GRID_W = 64
EPS = 1e-6
SSD_HEADS, SSD_HEAD_DIM, SSD_STATE, SSD_GROUPS = 16, 64, 64, 2
SSD_CONV_W = 5
SSD_CHUNK = 128
SSD_XBC = D + 2 * SSD_GROUPS * SSD_STATE
CONV_CH, CONV_W = 512, 31
DA_HEADS, DA_HEAD_DIM, DA_V_DIM = 4, 64, 128
DA_QK = DA_HEADS * 2 * DA_HEAD_DIM
DA_V = DA_HEADS * DA_V_DIM
ROPE_BASE = 10000.0
MOE_GROUPS, MOE_EPG, MOE_E, MOE_HIDDEN = 4, 8, 32, 512
ROUTER_LANE0 = MOE_GROUPS

LANES = 128
SUBLANES = 8
TM = 256
N_TILES = T // TM
N_PROMPT_TILES = T_P // TM
TILES_PER_LATENT_SEQ = LATENT_LEN // TM
MOE_BLOCK = 256
N_SLOT_BLOCKS = (2 * T) // MOE_BLOCK + MOE_E
N_SLOTS = N_SLOT_BLOCKS * MOE_BLOCK
VMEM_LIMIT = 56 * 1024 * 1024

F32 = jnp.float32
BF16 = jnp.bfloat16
HI = lax.Precision.HIGHEST
NEG = -1e30


def _cparams(sem, vmem=VMEM_LIMIT):
    return pltpu.CompilerParams(dimension_semantics=sem, vmem_limit_bytes=vmem)


def _mod_row(i):
    return jnp.where(i < N_PROMPT_TILES, N_LATENT_SEQ, (i - N_PROMPT_TILES) // TILES_PER_LATENT_SEQ)


def _silu(x):
    return x * (1.0 / (1.0 + jnp.exp(-x)))


def _sigmoid(x):
    return 1.0 / (1.0 + jnp.exp(-x))


def _softplus(x):
    return jnp.maximum(x, 0.0) + jnp.log(1.0 + jnp.exp(-jnp.abs(x)))


def _dot(a, b, **kw):
    return jnp.dot(a, b, preferred_element_type=F32, **kw)


def _dot_nt(a, b):
    return lax.dot_general(a, b, (((1,), (1,)), ((), ())), preferred_element_type=F32)


def _ada_kernel(c_ref, w_ref, b_ref, o_ref):
    cs = _silu(c_ref[...])
    o_ref[...] = _dot(cs, w_ref[...], precision=HI) + b_ref[...]


def _ada_call(cvec, w_ada, b_ada):
    nj = 6
    return pl.pallas_call(
        _ada_kernel,
        out_shape=jax.ShapeDtypeStruct((DEPTH, 16, 6 * D), F32),
        grid=(DEPTH, nj),
        in_specs=[
            pl.BlockSpec((16, D), lambda l, j: (0, 0)),
            pl.BlockSpec((None, D, D), lambda l, j: (l, 0, j)),
            pl.BlockSpec((None, 1, D), lambda l, j: (l, 0, j)),
        ],
        out_specs=pl.BlockSpec((None, 16, D), lambda l, j: (l, 0, j)),
        compiler_params=_cparams(("arbitrary", "arbitrary")),
        name="ada_mod",
    )(cvec, w_ada, b_ada.reshape(DEPTH, 1, 6 * D))


_C_Z = (0, D)
_C_XBC = (_C_Z[1], _C_Z[1] + SSD_XBC)
_C_DT = (_C_XBC[1], _C_XBC[1] + LANES)
_C_GLU = (_C_DT[1], _C_DT[1] + 2 * CONV_CH)
_C_Q = (_C_GLU[1], _C_GLU[1] + DA_QK)
_C_K = (_C_Q[1], _C_Q[1] + DA_QK)
_C_V = (_C_K[1], _C_K[1] + DA_V)
N_PROJ = _C_V[1]


def _swap16(x):
    cols = []
    for c in range(x.shape[1] // LANES):
        xc = x[:, c * LANES:(c + 1) * LANES]
        lane = lax.broadcasted_iota(jnp.int32, xc.shape, 1)
        from_right = pltpu.roll(xc, LANES - 16, axis=1)
        from_left = pltpu.roll(xc, 16, axis=1)
        cols.append(jnp.where((lane >> 4) % 2 == 0, from_right, from_left))
    return jnp.concatenate(cols, axis=1)


def _inproj_kernel(l_ref, x_ref, mod_ref, g_ref, w_ref, b_ref, cos_ref, sin_ref,
                   h_ref, z_ref, xbc_ref, dt_ref, u_ref, q_ref, k_ref, v_ref):
    x = x_ref[...]
    xn = x * lax.rsqrt(jnp.mean(x * x, axis=-1, keepdims=True) + EPS)
    h = xn * g_ref[...] * (1.0 + mod_ref[1:2, :]) + mod_ref[0:1, :]
    hb = h.astype(BF16)
    h_ref[...] = hb

    def proj(c):
        return _dot(hb, w_ref[:, c[0]:c[1]]) + b_ref[:, c[0]:c[1]]

    z_ref[...] = proj(_C_Z).astype(BF16)
    xbc_ref[...] = proj(_C_XBC)
    dt_ref[...] = proj(_C_DT)
    glu = proj(_C_GLU)
    u_ref[...] = glu[:, :CONV_CH] * _sigmoid(glu[:, CONV_CH:])
    cos = cos_ref[...]
    sin = sin_ref[...]
    q = proj(_C_Q)
    q = q * cos + _swap16(q) * sin
    q_ref[...] = (q * (DA_HEAD_DIM ** -0.5)).astype(BF16)
    k = proj(_C_K)
    k_ref[...] = k * cos + _swap16(k) * sin
    v_ref[...] = proj(_C_V)


def _inproj_call(l_arr, x, mod6, norm1_g, w_proj, b_proj, cos_tab, sin_tab):
    tok = lambda n: pl.BlockSpec((TM, n), lambda i, l: (i, 0))
    rope_blk = lambda i, l: (jnp.where(i < N_PROMPT_TILES, TILES_PER_LATENT_SEQ,
                                       (i - N_PROMPT_TILES) % TILES_PER_LATENT_SEQ), 0)
    grid_spec = pltpu.PrefetchScalarGridSpec(
        num_scalar_prefetch=1,
        grid=(N_TILES,),
        in_specs=[
            tok(D),
            pl.BlockSpec((None, None, 6, D), lambda i, l: (l[0], _mod_row(i), 0, 0)),
            pl.BlockSpec((None, 1, D), lambda i, l: (l[0], 0, 0)),
            pl.BlockSpec((None, D, N_PROJ), lambda i, l: (l[0], 0, 0)),
            pl.BlockSpec((None, 1, N_PROJ), lambda i, l: (l[0], 0, 0)),
            pl.BlockSpec((TM, DA_QK), rope_blk),
            pl.BlockSpec((TM, DA_QK), rope_blk),
        ],
        out_specs=[tok(D), tok(D), tok(SSD_XBC), tok(LANES), tok(CONV_CH), tok(DA_QK), tok(DA_QK), tok(DA_V)],
    )
    sds = lambda n, dt: jax.ShapeDtypeStruct((T, n), dt)
    return pl.pallas_call(
        _inproj_kernel,
        grid_spec=grid_spec,
        out_shape=[sds(D, BF16), sds(D, BF16), sds(SSD_XBC, F32), sds(LANES, F32), sds(CONV_CH, F32),
                   sds(DA_QK, BF16), sds(DA_QK, F32), sds(DA_V, F32)],
        compiler_params=_cparams(("arbitrary",)),
        name="inproj",
    )(l_arr, x, mod6, norm1_g, w_proj, b_proj, cos_tab, sin_tab)


N_CHUNKS = T // SSD_CHUNK
N_PROMPT_CHUNKS = T_P // SSD_CHUNK
CH_PER_PROMPT = PROMPT_LEN // SSD_CHUNK
CH_PER_LATENT = LATENT_LEN // SSD_CHUNK
HALO = SUBLANES
STATE_DUMP = N_PROMPT_SEQ


def _ssd_tables():
    cidx = np.zeros((2, N_CHUNKS), np.int32)
    flags = np.zeros((2, N_CHUNKS), np.int32)
    h0 = np.zeros((2, N_CHUNKS), np.int32)
    so = np.zeros((2, N_CHUNKS), np.int32)
    for d in range(2):
        for j in range(N_CHUNKS):
            c = j if d == 0 else N_CHUNKS - 1 - j
            if c < N_PROMPT_CHUNKS:
                seq, pos, n = c // CH_PER_PROMPT, c % CH_PER_PROMPT, CH_PER_PROMPT
                h0i, soi = N_LATENT_SEQ, seq
            else:
                cc = c - N_PROMPT_CHUNKS
                seq, pos, n = cc // CH_PER_LATENT, cc % CH_PER_LATENT, CH_PER_LATENT
                h0i, soi = seq, STATE_DUMP
            first = pos == 0 if d == 0 else pos == n - 1
            cidx[d, j] = c
            flags[d, j] = int(first) | (int(pos > 0) << 1) | (int(pos < n - 1) << 2)
            h0[d, j] = h0i
            so[d, j] = soi
    return [jnp.asarray(a.reshape(-1)) for a in (cidx, flags, h0, so)]


def _ssd_kernel(l_ref, cidx_ref, flags_ref, h0i_ref, soi_ref,
                xc_ref, xp_ref, xn_ref, dt_ref, cw_ref, cb_ref, dtb_ref, alog_ref, dskip_ref,
                e_ref, e128_ref, h0_ref,
                y_ref, so_ref,
                state, xpad, lat_s):
    d = pl.program_id(0)
    j = pl.program_id(1)
    flags = flags_ref[d * N_CHUNKS + j]
    q = SSD_CHUNK

    @pl.when((flags & 1) == 1)
    def _():
        state[...] = h0_ref[...]

    xpad[0:HALO, :] = jnp.where(((flags >> 1) & 1) == 1, xp_ref[...], 0.0)
    xpad[HALO:HALO + q, :] = xc_ref[...]
    xpad[HALO + q:HALO + q + HALO, :] = jnp.where(((flags >> 2) & 1) == 1, xn_ref[...], 0.0)
    acc = jnp.zeros((q, SSD_XBC), F32) + cb_ref[...]
    pad = SSD_CONV_W // 2
    for k in range(SSD_CONV_W):
        acc = acc + cw_ref[k:k + 1, :] * xpad[HALO - pad + k:HALO - pad + k + q, :]
    xc = _silu(acc)
    xs = xc[:, :D]
    bm = xc[:, D:D + LANES]
    cm = xc[:, D + LANES:D + 2 * LANES]

    dt_all = _softplus(dt_ref[...] + dtb_ref[...])
    da_all = dt_all * (-jnp.exp(alog_ref[...]))
    is_fwd = d == 0
    dt = jnp.where(is_fwd, dt_all, pltpu.roll(dt_all, LANES - SSD_HEADS, axis=1))
    da = jnp.where(is_fwd, da_all, pltpu.roll(da_all, LANES - SSD_HEADS, axis=1))
    row = lax.broadcasted_iota(jnp.int32, (q, q), 0)
    col = lax.broadcasted_iota(jnp.int32, (q, q), 1)
    tri = (row - col) * (1 - 2 * d) >= 0
    la = _dot(tri.astype(F32), da, precision=HI)
    lat_s[...] = la.T
    e = e_ref[...]
    la_exp = _dot(la, e, precision=HI)
    dt_exp = _dot(dt, e, precision=HI)
    la_col = _dot(la, e128_ref[...], precision=HI)
    la_end = jnp.where(is_fwd, la_exp[q - 1:q, :], la_exp[0:1, :])
    decay_end = jnp.exp(la_end - la_exp)
    chunk_decay = jnp.exp(la_end)
    decay_in = jnp.exp(la_exp)
    xdt = xs * dt_exp
    xdt_b = xdt.astype(BF16)
    xdtw_b = (xdt * decay_end).astype(BF16)
    bmt = bm.T
    skip = jnp.where(is_fwd, dskip_ref[...], 0.0)

    half = LANES // 2
    lane = lax.broadcasted_iota(jnp.int32, (q, LANES), 1)
    hpg = SSD_HEADS // SSD_GROUPS
    for g in range(SSD_GROUPS):
        c_g = cm[:, g * half:(g + 1) * half].astype(BF16)
        b_g = bm[:, g * half:(g + 1) * half].astype(BF16)
        bt_g = bmt[g * half:(g + 1) * half, :].astype(BF16)
        cb = _dot_nt(c_g, b_g)
        for pp in range(hpg // 2):
            h_a = g * hpg + 2 * pp
            sl = slice(h_a * SSD_HEAD_DIM, (h_a + 2) * SSD_HEAD_DIM)
            xdt_p = xdt_b[:, sl]
            yd = []
            for h in (h_a, h_a + 1):
                seg = la_col[:, h * LANES:(h + 1) * LANES] - lat_s[h:h + 1, :]
                s_h = (cb * jnp.exp(jnp.where(tri, seg, NEG))).astype(BF16)
                yd.append(_dot(s_h, xdt_p))
            y_diag = jnp.where(lane < half, yd[0], yd[1])
            st_in = state[:, sl]
            y_off = _dot(c_g, st_in.astype(BF16)) * decay_in[:, sl]
            state[:, sl] = st_in * chunk_decay[:, sl] + _dot(bt_g, xdtw_b[:, sl])
            y = y_diag + y_off + xs[:, sl] * skip[:, sl]
            y_ref[:, sl] = y.astype(BF16)
    so_ref[...] = state[...]


def _ssd_call(l_arr, tabs, xbc, dt, conv_w, conv_b, dt_bias, a_log, dskip, e_mat, e128_mat, h0t):
    nb8 = T // HALO
    per = SSD_CHUNK // HALO
    cur = lambda d, j, l, ci, fl, h0, so: (ci[d * N_CHUNKS + j], 0)
    prev = lambda d, j, l, ci, fl, h0, so: (jnp.maximum(ci[d * N_CHUNKS + j] * per - 1, 0), 0)
    nxt = lambda d, j, l, ci, fl, h0, so: (jnp.minimum((ci[d * N_CHUNKS + j] + 1) * per, nb8 - 1), 0)
    lay = lambda *shape: pl.BlockSpec((None,) + shape, lambda d, j, l, *_: (l[0],) + (0,) * len(shape))
    const = lambda *shape: pl.BlockSpec(shape, lambda d, j, *_: (0,) * len(shape))
    grid_spec = pltpu.PrefetchScalarGridSpec(
        num_scalar_prefetch=5,
        grid=(2, N_CHUNKS),
        in_specs=[
            pl.BlockSpec((SSD_CHUNK, SSD_XBC), cur),
            pl.BlockSpec((HALO, SSD_XBC), prev),
            pl.BlockSpec((HALO, SSD_XBC), nxt),
            pl.BlockSpec((SSD_CHUNK, LANES), cur),
            lay(SSD_CONV_W, SSD_XBC),
            lay(1, SSD_XBC),
            lay(1, LANES),
            lay(1, LANES),
            lay(1, D),
            const(LANES, D),
            const(LANES, SSD_HEADS * LANES),
            pl.BlockSpec((None, None, None, SSD_STATE, D),
                         lambda d, j, l, ci, fl, h0, so: (l[0], h0[d * N_CHUNKS + j], d, 0, 0)),
        ],
        out_specs=[
            pl.BlockSpec((None, SSD_CHUNK, D), lambda d, j, l, ci, fl, h0, so: (d, ci[d * N_CHUNKS + j], 0)),
            pl.BlockSpec((None, None, SSD_STATE, D),
                         lambda d, j, l, ci, fl, h0, so: (so[d * N_CHUNKS + j], d, 0, 0)),
        ],
        scratch_shapes=[
            pltpu.VMEM((SSD_STATE, D), F32),
            pltpu.VMEM((SSD_CHUNK + 2 * HALO, SSD_XBC), F32),
            pltpu.VMEM((LANES, SSD_CHUNK), F32),
        ],
    )
    return pl.pallas_call(
        _ssd_kernel,
        grid_spec=grid_spec,
        out_shape=[jax.ShapeDtypeStruct((2, T, D), BF16),
                   jax.ShapeDtypeStruct((N_PROMPT_SEQ + 1, 2, SSD_STATE, D), F32)],
        compiler_params=_cparams(("arbitrary", "arbitrary")),
        name="ssd_scan",
    )(l_arr, *tabs, xbc, xbc, xbc, dt, conv_w, conv_b, dt_bias, a_log, dskip, e_mat, e128_mat, h0t)


CV_HALO = 16


def _cv_kernel(l_ref, uc_ref, up_ref, un_ref, w_ref, b_ref, g_ref, beta_ref, o_ref, upad):
    i = pl.program_id(0)
    pos = (i - N_PROMPT_TILES) % TILES_PER_LATENT_SEQ
    is_prompt = i < N_PROMPT_TILES
    no_l = jnp.logical_or(is_prompt, pos == 0)
    no_r = jnp.logical_or(is_prompt, pos == TILES_PER_LATENT_SEQ - 1)
    upad[0:CV_HALO, :] = jnp.where(no_l, 0.0, up_ref[...])
    upad[CV_HALO:CV_HALO + TM, :] = uc_ref[...]
    upad[CV_HALO + TM:CV_HALO + TM + CV_HALO, :] = jnp.where(no_r, 0.0, un_ref[...])
    acc = jnp.zeros((TM, CONV_CH), F32) + b_ref[...]
    pad = CONV_W // 2
    for k in range(CONV_W):
        acc = acc + w_ref[k:k + 1, :] * upad[CV_HALO - pad + k:CV_HALO - pad + k + TM, :]
    mu = jnp.mean(acc, axis=-1, keepdims=True)
    xc = acc - mu
    var = jnp.mean(xc * xc, axis=-1, keepdims=True)
    y = xc * lax.rsqrt(var + EPS) * g_ref[...] + beta_ref[...]
    o_ref[...] = _silu(y).astype(BF16)


def _cv_call(l_arr, u, w, b, g, beta):
    per = TM // CV_HALO
    nb = T // CV_HALO
    lay = lambda *shape: pl.BlockSpec((None,) + shape, lambda i, l: (l[0],) + (0,) * len(shape))
    grid_spec = pltpu.PrefetchScalarGridSpec(
        num_scalar_prefetch=1,
        grid=(N_TILES,),
        in_specs=[
            pl.BlockSpec((TM, CONV_CH), lambda i, l: (i, 0)),
            pl.BlockSpec((CV_HALO, CONV_CH), lambda i, l: (jnp.maximum(i * per - 1, 0), 0)),
            pl.BlockSpec((CV_HALO, CONV_CH), lambda i, l: (jnp.minimum((i + 1) * per, nb - 1), 0)),
            lay(CONV_W, CONV_CH), lay(1, CONV_CH), lay(1, CONV_CH), lay(1, CONV_CH),
        ],
        out_specs=pl.BlockSpec((TM, CONV_CH), lambda i, l: (i, 0)),
        scratch_shapes=[pltpu.VMEM((TM + 2 * CV_HALO, CONV_CH), F32)],
    )
    return pl.pallas_call(
        _cv_kernel,
        grid_spec=grid_spec,
        out_shape=jax.ShapeDtypeStruct((T, CONV_CH), BF16),
        compiler_params=_cparams(("arbitrary",)),
        name="conformer_conv",
    )(l_arr, u, u, u, w, b, g, beta)


def _lambda_terms(l_ref, lam_ref):
    lf = jnp.full((1, 1), l_ref[0], jnp.int32).astype(F32)
    lam_init = 0.8 - 0.6 * jnp.exp(-0.3 * lf)
    p = lam_ref[...]
    s1 = jnp.sum(p[0:1, :] * p[1:2, :], axis=-1, keepdims=True)
    s2 = jnp.sum(p[2:3, :] * p[3:4, :], axis=-1, keepdims=True)
    lam = jnp.exp(s1) - jnp.exp(s2) + lam_init
    return lam, 1.0 - lam_init


def _attn_body(l_ref, q_ref, k_ref, v_ref, ck_ref, cv_ref, lam_ref, g_ref, o_ref, k_s, v_s, n_ctx):
    @pl.when(pl.program_id(1) == 0)
    def _():
        if n_ctx:
            k_s[0:n_ctx, :] = ck_ref[...].astype(BF16)
            v_s[0:n_ctx, :] = cv_ref[...].astype(BF16)
        k_s[n_ctx:, :] = k_ref[...].astype(BF16)
        v_s[n_ctx:, :] = v_ref[...].astype(BF16)

    lam, out_scale = _lambda_terms(l_ref, lam_ref)
    tq = q_ref.shape[0]
    lane = lax.broadcasted_iota(jnp.int32, (tq, LANES), 1)
    zero = jnp.zeros((tq, LANES), BF16)
    for h in range(DA_HEADS):
        sl = slice(h * LANES, (h + 1) * LANES)
        qh = q_ref[:, sl]
        kh = k_s[:, sl]
        vh = v_s[:, sl]
        outs = []
        for c in range(2):
            in_c = (lane < DA_HEAD_DIM) if c == 0 else (lane >= DA_HEAD_DIM)
            s = _dot_nt(jnp.where(in_c, qh, zero), kh)
            m = jnp.max(s, axis=-1, keepdims=True)
            e = jnp.exp(s - m)
            den = jnp.sum(e, axis=-1, keepdims=True)
            outs.append(_dot(e.astype(BF16), vh) / den)
        o = outs[0] - lam * outs[1]
        o = o * lax.rsqrt(jnp.mean(o * o, axis=-1, keepdims=True) + EPS)
        o_ref[:, sl] = (o * g_ref[...] * out_scale).astype(BF16)


def _attn_prompt_kernel(l_ref, q_ref, k_ref, v_ref, lam_ref, g_ref, o_ref, k_s, v_s):
    _attn_body(l_ref, q_ref, k_ref, v_ref, None, None, lam_ref, g_ref, o_ref, k_s, v_s, 0)


def _attn_latent_kernel(l_ref, q_ref, k_ref, v_ref, ck_ref, cv_ref, lam_ref, g_ref, o_ref, k_s, v_s):
    _attn_body(l_ref, q_ref, k_ref, v_ref, ck_ref, cv_ref, lam_ref, g_ref, o_ref, k_s, v_s, PAST_LEN)


def _attn_call(l_arr, q, k, v, cache_k, cache_v, da_lambda, subln_g, latent):
    if latent:
        nseq, seqlen, n_ctx = N_LATENT_SEQ, LATENT_LEN, PAST_LEN
        tile0, seq0 = N_PROMPT_TILES, T_P // LATENT_LEN
    else:
        nseq, seqlen, n_ctx = N_PROMPT_SEQ, PROMPT_LEN, 0
        tile0, seq0 = 0, 0
    nq = seqlen // TM
    lay = lambda *shape: pl.BlockSpec((None,) + shape, lambda b, i, l: (l[0],) + (0,) * len(shape))
    in_specs = [
        pl.BlockSpec((TM, DA_QK), lambda b, i, l: (tile0 + b * nq + i, 0)),
        pl.BlockSpec((seqlen, DA_QK), lambda b, i, l: (seq0 + b, 0)),
        pl.BlockSpec((seqlen, DA_V), lambda b, i, l: (seq0 + b, 0)),
    ]
    args = [q, k, v]
    if latent:
        ctx = pl.BlockSpec((None, None, PAST_LEN, DA_QK), lambda b, i, l: (b, l[0], 0, 0))
        in_specs += [ctx, ctx]
        args += [cache_k, cache_v]
    in_specs += [lay(4, DA_HEAD_DIM), lay(1, DA_V_DIM)]
    args += [da_lambda, subln_g]
    grid_spec = pltpu.PrefetchScalarGridSpec(
        num_scalar_prefetch=1,
        grid=(nseq, nq),
        in_specs=in_specs,
        out_specs=pl.BlockSpec((TM, DA_V), lambda b, i, l: (b * nq + i, 0)),
        scratch_shapes=[pltpu.VMEM((n_ctx + seqlen, DA_QK), BF16), pltpu.VMEM((n_ctx + seqlen, DA_V), BF16)],
    )
    return pl.pallas_call(
        _attn_latent_kernel if latent else _attn_prompt_kernel,
        grid_spec=grid_spec,
        out_shape=jax.ShapeDtypeStruct((nseq * seqlen, DA_V), BF16),
        compiler_params=_cparams(("arbitrary", "arbitrary")),
        name="diff_attn_latent" if latent else "diff_attn_prompt",
    )(l_arr, *args)


def _combine_kernel(l_ref, x_ref, h_ref, y_ref, z_ref, uc_ref, oa_ref, mod_ref,
                    wg_ref, bg_ref, sg_ref, wa_ref, wb_ref, wc_ref, wo_ref, g2_ref, wr_ref, br_ref,
                    x1_ref, h2_ref, route_ref, cnt_ref, carry):
    i = pl.program_id(0)

    @pl.when(i == 0)
    def _():
        carry[...] = jnp.zeros_like(carry)

    y = (y_ref[0].astype(F32) + y_ref[1].astype(F32)) * _silu(z_ref[...].astype(F32))
    y = y * lax.rsqrt(jnp.mean(y * y, axis=-1, keepdims=True) + EPS) * sg_ref[...]
    br_a = _dot(y.astype(BF16), wa_ref[...])
    br_b = _dot(uc_ref[...], wb_ref[...])
    br_c = _dot(oa_ref[...], wc_ref[...])
    hb = h_ref[...]

    def gate(n):
        return _sigmoid(_dot(hb, wg_ref[:, n * D:(n + 1) * D]) + bg_ref[:, n * D:(n + 1) * D])

    mix = gate(0) * br_a + gate(1) * br_b + gate(2) * br_c
    mixed = _dot(mix.astype(BF16), wo_ref[...])
    x1 = x_ref[...] + mod_ref[2:3, :] * mixed
    x1_ref[...] = x1
    xn = x1 * lax.rsqrt(jnp.mean(x1 * x1, axis=-1, keepdims=True) + EPS)
    h2 = xn * g2_ref[...] * (1.0 + mod_ref[4:5, :]) + mod_ref[3:4, :]
    h2_ref[...] = h2

    logits = _dot(h2, wr_ref[...], precision=HI) + br_ref[...]
    lane = lax.broadcasted_iota(jnp.int32, (TM, LANES), 1).astype(F32)

    def first_argmax(vals, vmax):
        return jnp.min(jnp.where(vals == vmax, lane, float(LANES)), axis=-1, keepdims=True)

    glog = jnp.where(lane < MOE_GROUPS, logits, NEG)
    gmax = jnp.max(glog, axis=-1, keepdims=True)
    gsel = first_argmax(glog, gmax)
    p_g = 1.0 / jnp.sum(jnp.exp(glog - gmax), axis=-1, keepdims=True)
    lo = ROUTER_LANE0 + MOE_EPG * gsel
    elog = jnp.where(jnp.logical_and(lane >= lo, lane < lo + MOE_EPG), logits, NEG)
    v1 = jnp.max(elog, axis=-1, keepdims=True)
    i1 = first_argmax(elog, v1)
    elog2 = jnp.where(lane == i1, NEG, elog)
    v2 = jnp.max(elog2, axis=-1, keepdims=True)
    i2 = first_argmax(elog2, v2)
    t2 = jnp.exp(v2 - v1)
    w1 = p_g / (1.0 + t2)
    w2 = p_g * t2 / (1.0 + t2)

    oh1 = lane == i1
    oh2 = lane == i2
    both = jnp.where(jnp.logical_or(oh1, oh2), 1.0, 0.0)
    r = lax.broadcasted_iota(jnp.int32, (TM, TM), 0)
    c = lax.broadcasted_iota(jnp.int32, (TM, TM), 1)
    strict_lower = jnp.where(c < r, 1.0, 0.0).astype(BF16)
    before = _dot(strict_lower, both.astype(BF16)) + carry[0:1, :]
    r1 = jnp.sum(jnp.where(oh1, before, 0.0), axis=-1, keepdims=True)
    r2 = jnp.sum(jnp.where(oh2, before, 0.0), axis=-1, keepdims=True)
    carry[...] = carry[...] + jnp.sum(both, axis=0, keepdims=True)
    cnt_ref[...] = carry[...]

    e1 = i1 - ROUTER_LANE0
    e2 = i2 - ROUTER_LANE0
    route = jnp.zeros((TM, LANES), F32)
    for n, val in enumerate((e1, e2, w1, w2, r1, r2)):
        route = jnp.where(lane == float(n), val, route)
    route_ref[...] = route


def _combine_call(l_arr, x, h, y2, z, uc, oa, mod6, w_gates, b_gates, ssd_norm_g, w_br_ssd, w_br_conv,
                  w_br_attn, w_out, norm2_g, w_router, b_router):
    tok = lambda n: pl.BlockSpec((TM, n), lambda i, l: (i, 0))
    lay = lambda *shape: pl.BlockSpec((None,) + shape, lambda i, l: (l[0],) + (0,) * len(shape))
    grid_spec = pltpu.PrefetchScalarGridSpec(
        num_scalar_prefetch=1,
        grid=(N_TILES,),
        in_specs=[
            tok(D), tok(D),
            pl.BlockSpec((2, TM, D), lambda i, l: (0, i, 0)),
            tok(D), tok(CONV_CH), tok(DA_V),
            pl.BlockSpec((None, None, 6, D), lambda i, l: (l[0], _mod_row(i), 0, 0)),
            lay(D, 3 * D), lay(1, 3 * D), lay(1, D), lay(D, D), lay(CONV_CH, D), lay(DA_V, D), lay(D, D),
            lay(1, D), lay(D, LANES), lay(1, LANES),
        ],
        out_specs=[tok(D), tok(D), tok(LANES), pl.BlockSpec((SUBLANES, LANES), lambda i, l: (0, 0))],
        scratch_shapes=[pltpu.VMEM((SUBLANES, LANES), F32)],
    )
    return pl.pallas_call(
        _combine_kernel,
        grid_spec=grid_spec,
        out_shape=[jax.ShapeDtypeStruct((T, D), F32), jax.ShapeDtypeStruct((T, D), F32),
                   jax.ShapeDtypeStruct((T, LANES), F32), jax.ShapeDtypeStruct((SUBLANES, LANES), F32)],
        compiler_params=_cparams(("arbitrary",)),
        name="branch_combine_router",
    )(l_arr, x, h, y2, z, uc, oa, mod6, w_gates, b_gates, ssd_norm_g, w_br_ssd, w_br_conv, w_br_attn,
      w_out, norm2_g, w_router, b_router)


def _dispatch_kernel(d1_ref, d2_ref, h2_ref, xs_in_ref, xs_ref, sem):
    del xs_in_ref
    base = pl.program_id(0) * TM

    def row_copy(t, dref):
        return pltpu.make_async_copy(h2_ref.at[pl.ds(t, 1), :], xs_ref.at[pl.ds(dref[base + t], 1), :], sem)

    def issue(t, carry):
        row_copy(t, d1_ref).start()
        row_copy(t, d2_ref).start()
        return carry

    lax.fori_loop(0, TM, issue, 0)

    def drain(t, carry):
        row_copy(t, d1_ref).wait()
        row_copy(t, d2_ref).wait()
        return carry

    lax.fori_loop(0, TM, drain, 0)


def _dispatch_call(dest1, dest2, h2, xs_init):
    grid_spec = pltpu.PrefetchScalarGridSpec(
        num_scalar_prefetch=2,
        grid=(N_TILES,),
        in_specs=[pl.BlockSpec((TM, D), lambda i, d1, d2: (i, 0)), pl.BlockSpec(memory_space=pl.ANY)],
        out_specs=pl.BlockSpec(memory_space=pl.ANY),
        scratch_shapes=[pltpu.SemaphoreType.DMA(())],
    )
    return pl.pallas_call(
        _dispatch_kernel,
        grid_spec=grid_spec,
        out_shape=jax.ShapeDtypeStruct((N_SLOTS, D), F32),
        input_output_aliases={3: 0},
        compiler_params=_cparams(("arbitrary",)),
        name="moe_dispatch",
    )(dest1, dest2, h2, xs_init)


def _moe_kernel(l_ref, be_ref, nb_ref, xs_ref, wg_ref, wu_ref, wd_ref, ys_ref, wg_s, wu_s, wd_s):
    i = pl.program_id(0)
    prev = be_ref[jnp.maximum(i - 1, 0)]

    @pl.when(jnp.logical_or(i == 0, be_ref[i] != prev))
    def _():
        wg_s[...] = wg_ref[...].astype(BF16)
        wu_s[...] = wu_ref[...].astype(BF16)
        wd_s[...] = wd_ref[...].astype(BF16)

    @pl.when(i < nb_ref[0])
    def _():
        xb = xs_ref[...].astype(BF16)
        hid = _silu(_dot(xb, wg_s[...])) * _dot(xb, wu_s[...])
        ys_ref[...] = _dot(hid.astype(BF16), wd_s[...])

    @pl.when(i >= nb_ref[0])
    def _():
        ys_ref[...] = jnp.zeros_like(ys_ref)


def _moe_call(l_arr, block_expert, n_used, xs, w_gate, w_up, w_down):
    wspec = lambda a, b: pl.BlockSpec((None, None, a, b), lambda i, l, be, nb: (l[0], be[i], 0, 0))
    grid_spec = pltpu.PrefetchScalarGridSpec(
        num_scalar_prefetch=3,
        grid=(N_SLOT_BLOCKS,),
        in_specs=[pl.BlockSpec((MOE_BLOCK, D), lambda i, l, be, nb: (i, 0)),
                  wspec(D, MOE_HIDDEN), wspec(D, MOE_HIDDEN), wspec(MOE_HIDDEN, D)],
        out_specs=pl.BlockSpec((MOE_BLOCK, D), lambda i, l, be, nb: (i, 0)),
        scratch_shapes=[pltpu.VMEM((D, MOE_HIDDEN), BF16), pltpu.VMEM((D, MOE_HIDDEN), BF16),
                        pltpu.VMEM((MOE_HIDDEN, D), BF16)],
    )
    return pl.pallas_call(
        _moe_kernel,
        grid_spec=grid_spec,
        out_shape=jax.ShapeDtypeStruct((N_SLOTS, D), F32),
        compiler_params=_cparams(("arbitrary",)),
        name="moe_experts",
    )(l_arr, block_expert, n_used, xs, w_gate, w_up, w_down)


def _moe_combine_kernel(d1_ref, d2_ref, x1_ref, route_ref, mod_ref, fg_ref, ys_ref, o_ref, buf, sem, *, final):
    base = pl.program_id(0) * TM

    def row_copy(t, which, dref):
        return pltpu.make_async_copy(ys_ref.at[pl.ds(dref[base + t], 1), :], buf.at[which, pl.ds(t, 1), :], sem)

    def issue(t, carry):
        row_copy(t, 0, d1_ref).start()
        row_copy(t, 1, d2_ref).start()
        return carry

    lax.fori_loop(0, TM, issue, 0)

    def drain(t, carry):
        row_copy(t, 0, d1_ref).wait()
        row_copy(t, 1, d2_ref).wait()
        return carry

    lax.fori_loop(0, TM, drain, 0)
    w1 = route_ref[:, 2:3]
    w2 = route_ref[:, 3:4]
    y = buf[0] * w1 + buf[1] * w2
    x2 = x1_ref[...] + mod_ref[5:6, :] * y
    if final:
        x2 = x2 * lax.rsqrt(jnp.mean(x2 * x2, axis=-1, keepdims=True) + EPS) * fg_ref[...]
    o_ref[...] = x2


def _moe_combine_call(l_arr, dest1, dest2, x1, route, mod6, final_g, ys, final):
    grid_spec = pltpu.PrefetchScalarGridSpec(
        num_scalar_prefetch=3,
        grid=(N_TILES,),
        in_specs=[
            pl.BlockSpec((TM, D), lambda i, l, d1, d2: (i, 0)),
            pl.BlockSpec((TM, LANES), lambda i, l, d1, d2: (i, 0)),
            pl.BlockSpec((None, None, 6, D), lambda i, l, d1, d2: (l[0], _mod_row(i), 0, 0)),
            pl.BlockSpec((1, D), lambda i, l, d1, d2: (0, 0)),
            pl.BlockSpec(memory_space=pl.ANY),
        ],
        out_specs=pl.BlockSpec((TM, D), lambda i, l, d1, d2: (i, 0)),
        scratch_shapes=[pltpu.VMEM((2, TM, D), F32), pltpu.SemaphoreType.DMA(())],
    )

    def body(l_ref, d1_ref, d2_ref, *rest):
        del l_ref
        _moe_combine_kernel(d1_ref, d2_ref, *rest, final=final)

    return pl.pallas_call(
        body,
        grid_spec=grid_spec,
        out_shape=jax.ShapeDtypeStruct((T, D), F32),
        compiler_params=_cparams(("arbitrary",)),
        name="moe_combine_final" if final else "moe_combine",
    )(l_arr, dest1, dest2, x1, route, mod6, final_g, ys)


def _rope_tables():
    n = LATENT_LEN
    rows = n // GRID_W
    row = jnp.repeat(jnp.arange(rows), GRID_W).astype(F32)
    col = jnp.tile(jnp.arange(GRID_W), rows).astype(F32)
    axis_dim = DA_HEAD_DIM // 2
    inv_freq = 1.0 / (ROPE_BASE ** (jnp.arange(0, axis_dim, 2, dtype=F32) / axis_dim))
    ar, ac = row[:, None] * inv_freq, col[:, None] * inv_freq
    cos64 = jnp.concatenate([jnp.cos(ar), jnp.cos(ar), jnp.cos(ac), jnp.cos(ac)], axis=1)
    sin64 = jnp.concatenate([-jnp.sin(ar), jnp.sin(ar), -jnp.sin(ac), jnp.sin(ac)], axis=1)
    reps = DA_QK // DA_HEAD_DIM
    cos = jnp.concatenate([jnp.tile(cos64, (1, reps)), jnp.ones((TM, DA_QK), F32)], axis=0)
    sin = jnp.concatenate([jnp.tile(sin64, (1, reps)), jnp.zeros((TM, DA_QK), F32)], axis=0)
    return cos, sin


def _pad_lanes(a, n=LANES):
    return jnp.pad(a, [(0, 0)] * (a.ndim - 1) + [(0, n - a.shape[-1])])


def kernel(x_prompt, x_sample, cache_k, cache_v, state_ssd, c, c_ctx, w_ada, b_ada, norm1_g, norm2_g, w_in, b_in,
           ssd_conv_w, ssd_conv_b, ssd_dt_bias, ssd_a_log, ssd_d, ssd_norm_g, w_br_ssd, cv_dw_w, cv_dw_b, cv_ln_g,
           cv_ln_b, w_br_conv, da_lambda, da_subln_g, w_br_attn, w_out, moe_w_group, moe_b_group, moe_w_expert,
           moe_b_expert, moe_w_gate, moe_w_up, moe_w_down, final_g):
    L = DEPTH
    x = jnp.concatenate([x_prompt.reshape(T_P, D), x_sample.reshape(T_L, D)], axis=0)

    cvec = jnp.concatenate([c, c_ctx[None, :], jnp.zeros((16 - N_LATENT_SEQ - 1, D), F32)], axis=0)
    mod6 = _ada_call(cvec, w_ada, b_ada).reshape(L, 16, 6, D)

    o_z, o_xbc, o_dt = 0, D, D + SSD_XBC
    o_glu = o_dt + 2 * SSD_HEADS
    o_q = o_glu + 2 * CONV_CH
    o_k, o_v, o_g = o_q + DA_QK, o_q + 2 * DA_QK, o_q + 2 * DA_QK + DA_V

    def regroup(w):
        return jnp.concatenate([w[..., o_z:o_dt], _pad_lanes(w[..., o_dt:o_glu]), w[..., o_glu:o_g]], axis=-1)

    w_proj = regroup(w_in).astype(BF16)
    b_proj = regroup(b_in).reshape(L, 1, N_PROJ)
    w_gates = w_in[..., o_g:].astype(BF16)
    b_gates = b_in[..., o_g:].reshape(L, 1, 3 * D)
    cos_tab, sin_tab = _rope_tables()

    ssd_tabs = _ssd_tables()
    dt_bias = _pad_lanes(ssd_dt_bias.reshape(L, 1, 2 * SSD_HEADS))
    a_log = _pad_lanes(ssd_a_log.reshape(L, 1, 2 * SSD_HEADS))
    dskip = jnp.repeat(ssd_d, SSD_HEAD_DIM, axis=-1).reshape(L, 1, D)
    hp = np.arange(D) // SSD_HEAD_DIM
    e_mat = jnp.asarray((np.arange(LANES)[:, None] == hp[None, :]).astype(np.float32))
    h128 = np.arange(SSD_HEADS * LANES) // LANES
    e128_mat = jnp.asarray((np.arange(LANES)[:, None] == h128[None, :]).astype(np.float32))
    h0t = jnp.transpose(state_ssd, (1, 0, 2, 5, 3, 4)).reshape(L, N_LATENT_SEQ, 2, SSD_STATE, D)
    h0t = jnp.concatenate([h0t, jnp.zeros((L, 1, 2, SSD_STATE, D), F32)], axis=1)

    ck = cache_k.reshape(N_LATENT_SEQ, L, PAST_LEN, DA_QK)
    cv = cache_v.reshape(N_LATENT_SEQ, L, PAST_LEN, DA_V)
    w_router = _pad_lanes(jnp.concatenate([moe_w_group, moe_w_expert], axis=-1))
    b_router = _pad_lanes(jnp.concatenate([moe_b_group, moe_b_expert], axis=-1)).reshape(L, 1, LANES)
    w_br_ssd_b, w_br_conv_b = w_br_ssd.astype(BF16), w_br_conv.astype(BF16)
    w_br_attn_b, w_out_b = w_br_attn.astype(BF16), w_out.astype(BF16)
    r3 = lambda a: a.reshape(L, 1, a.shape[-1])
    xs_init = jnp.zeros((N_SLOTS, D), F32)

    ks_new, vs_new, ss_new = [], [], []
    for layer in range(L):
        l_arr = jnp.full((1,), layer, jnp.int32)
        h, z, xbc, dt, u, q, k, v = _inproj_call(l_arr, x, mod6, r3(norm1_g), w_proj, b_proj, cos_tab, sin_tab)
        y2, st = _ssd_call(l_arr, ssd_tabs, xbc, dt, ssd_conv_w, r3(ssd_conv_b), dt_bias, a_log, dskip,
                           e_mat, e128_mat, h0t)
        uc = _cv_call(l_arr, u, cv_dw_w, r3(cv_dw_b), r3(cv_ln_g), r3(cv_ln_b))
        oa_p = _attn_call(l_arr, q, k, v, None, None, da_lambda, r3(da_subln_g), latent=False)
        oa_l = _attn_call(l_arr, q, k, v, ck, cv, da_lambda, r3(da_subln_g), latent=True)
        oa = jnp.concatenate([oa_p, oa_l], axis=0)
        x1, h2, route, counts = _combine_call(l_arr, x, h, y2, z, uc, oa, mod6, w_gates, b_gates, r3(ssd_norm_g),
                                              w_br_ssd_b, w_br_conv_b, w_br_attn_b, w_out_b, r3(norm2_g),
                                              w_router, b_router)
        cnt = counts[0, ROUTER_LANE0:ROUTER_LANE0 + MOE_E].astype(jnp.int32)
        pcnt = (cnt + MOE_BLOCK - 1) // MOE_BLOCK * MOE_BLOCK
        pend = jnp.cumsum(pcnt)
        pstart = pend - pcnt
        e1 = route[:, 0].astype(jnp.int32)
        e2 = route[:, 1].astype(jnp.int32)
        dest1 = pstart[e1] + route[:, 4].astype(jnp.int32)
        dest2 = pstart[e2] + route[:, 5].astype(jnp.int32)
        blk0 = jnp.arange(N_SLOT_BLOCKS, dtype=jnp.int32) * MOE_BLOCK
        block_expert = jnp.minimum(jnp.searchsorted(pend, blk0, side='right'), MOE_E - 1).astype(jnp.int32)
        n_used = (pend[-1:] // MOE_BLOCK).astype(jnp.int32)

        xs = _dispatch_call(dest1, dest2, h2, xs_init)
        ys = _moe_call(l_arr, block_expert, n_used, xs, moe_w_gate, moe_w_up, moe_w_down)
        x = _moe_combine_call(l_arr, dest1, dest2, x1, route, mod6, final_g.reshape(1, D), ys, layer == L - 1)

        ks_new.append(k[:T_P].reshape(N_PROMPT_SEQ, PROMPT_LEN, DA_HEADS, 2 * DA_HEAD_DIM))
        vs_new.append(v[:T_P].reshape(N_PROMPT_SEQ, PROMPT_LEN, DA_HEADS, DA_V_DIM))
        s = st[:N_PROMPT_SEQ].reshape(N_PROMPT_SEQ, 2, SSD_STATE, SSD_HEADS, SSD_HEAD_DIM)
        ss_new.append(jnp.transpose(s, (0, 1, 3, 4, 2)))

    y_prompt = x[:T_P].reshape(N_PROMPT_SEQ, PROMPT_LEN, D)
    y_sample = x[T_P:].reshape(N_LATENT_SEQ, LATENT_LEN, D)
    return (y_prompt, y_sample, jnp.stack(ks_new, axis=1), jnp.stack(vs_new, axis=1), jnp.stack(ss_new, axis=1))
```

```python
import functools
import math

import jax
import jax.numpy as jnp
import numpy as np
from jax import lax
from jax.experimental import pallas as pl
from jax.experimental.pallas import tpu as pltpu

D = 1024
DEPTH = 4
N_PROMPT_SEQ, PROMPT_LEN = 16, 256
N_LATENT_SEQ, LATENT_LEN = 8, 2048
PAST_LEN = 512
T_P = N_PROMPT_SEQ * PROMPT_LEN
T_L = N_LATENT_SEQ * LATENT_LEN
T = T_P + T_L
GRID_W = 64
EPS = 1e-6
SSD_HEADS, SSD_HEAD_DIM, SSD_STATE, SSD_GROUPS = 16, 64, 64, 2
SSD_CONV_W = 5
SSD_CHUNK = 128
SSD_XBC = D + 2 * SSD_GROUPS * SSD_STATE
CONV_CH, CONV_W = 512, 31
DA_HEADS, DA_HEAD_DIM, DA_V_DIM = 4, 64, 128
DA_QK = DA_HEADS * 2 * DA_HEAD_DIM
DA_V = DA_HEADS * DA_V_DIM
ROPE_BASE = 10000.0
MOE_GROUPS, MOE_EPG, MOE_E, MOE_HIDDEN = 4, 8, 32, 512
ROUTER_LANE0 = MOE_GROUPS

LANES = 128
SUBLANES = 8
TM = 256
N_TILES = T // TM
N_PROMPT_TILES = T_P // TM
TILES_PER_LATENT_SEQ = LATENT_LEN // TM
MOE_BLOCK = 256
N_SLOT_BLOCKS = (2 * T) // MOE_BLOCK + MOE_E
N_SLOTS = N_SLOT_BLOCKS * MOE_BLOCK
VMEM_LIMIT = 56 * 1024 * 1024

F32 = jnp.float32
BF16 = jnp.bfloat16
HI = lax.Precision.HIGHEST
NEG = -1e30


def _cparams(sem, vmem=VMEM_LIMIT):
    return pltpu.CompilerParams(dimension_semantics=sem, vmem_limit_bytes=vmem)


def _mod_row(i):
    return jnp.where(i < N_PROMPT_TILES, N_LATENT_SEQ, (i - N_PROMPT_TILES) // TILES_PER_LATENT_SEQ)


def _silu(x):
    return x * (1.0 / (1.0 + jnp.exp(-x)))


def _sigmoid(x):
    return 1.0 / (1.0 + jnp.exp(-x))


def _softplus(x):
    return jnp.maximum(x, 0.0) + jnp.log(1.0 + jnp.exp(-jnp.abs(x)))


def _dot(a, b, **kw):
    return jnp.dot(a, b, preferred_element_type=F32, **kw)


def _split3(x):
    hi = x.astype(BF16)
    r1 = x - hi.astype(F32)
    mid = r1.astype(BF16)
    lo = (r1 - mid.astype(F32)).astype(BF16)
    return jnp.concatenate([hi, mid, lo], axis=1)


def _dot_nt(a, b):
    return lax.dot_general(a, b, (((1,), (1,)), ((), ())), preferred_element_type=F32)


def _ada_kernel(c_ref, w_ref, b_ref, o_ref):
    cs = _silu(c_ref[...])
    o_ref[...] = _dot(cs, w_ref[...], precision=HI) + b_ref[...]


def _ada_call(cvec, w_ada, b_ada):
    nj = 6
    return pl.pallas_call(
        _ada_kernel,
        out_shape=jax.ShapeDtypeStruct((DEPTH, 16, 6 * D), F32),
        grid=(DEPTH, nj),
        in_specs=[
            pl.BlockSpec((16, D), lambda l, j: (0, 0)),
            pl.BlockSpec((None, D, D), lambda l, j: (l, 0, j)),
            pl.BlockSpec((None, 1, D), lambda l, j: (l, 0, j)),
        ],
        out_specs=pl.BlockSpec((None, 16, D), lambda l, j: (l, 0, j)),
        compiler_params=_cparams(("arbitrary", "arbitrary")),
        name="ada_mod",
    )(cvec, w_ada, b_ada.reshape(DEPTH, 1, 6 * D))


_C_Z = (0, D)
_C_XBC = (_C_Z[1], _C_Z[1] + SSD_XBC)
_C_DT = (_C_XBC[1], _C_XBC[1] + LANES)
_C_GLU = (_C_DT[1], _C_DT[1] + 2 * CONV_CH)
_C_Q = (_C_GLU[1], _C_GLU[1] + DA_QK)
_C_K = (_C_Q[1], _C_Q[1] + DA_QK)
_C_V = (_C_K[1], _C_K[1] + DA_V)
N_PROJ = _C_V[1]


def _swap16(x):
    cols = []
    for c in range(x.shape[1] // LANES):
        xc = x[:, c * LANES:(c + 1) * LANES]
        lane = lax.broadcasted_iota(jnp.int32, xc.shape, 1)
        from_right = pltpu.roll(xc, LANES - 16, axis=1)
        from_left = pltpu.roll(xc, 16, axis=1)
        cols.append(jnp.where((lane >> 4) % 2 == 0, from_right, from_left))
    return jnp.concatenate(cols, axis=1)


def _inproj_kernel(l_ref, x_ref, mod_ref, g_ref, w_ref, b_ref, cos_ref, sin_ref,
                   h_ref, z_ref, xbc_ref, dt_ref, u_ref, q_ref, k_ref, v_ref):
    x = x_ref[...]
    xn = x * lax.rsqrt(jnp.mean(x * x, axis=-1, keepdims=True) + EPS)
    h = xn * g_ref[...] * (1.0 + mod_ref[1:2, :]) + mod_ref[0:1, :]
    hb = h.astype(BF16)
    h_ref[...] = hb

    def proj(c):
        return _dot(hb, w_ref[:, c[0]:c[1]]) + b_ref[:, c[0]:c[1]]

    z_ref[...] = proj(_C_Z).astype(BF16)
    xbc_ref[...] = proj(_C_XBC)
    dt_ref[...] = proj(_C_DT)
    glu = proj(_C_GLU)
    u_ref[...] = glu[:, :CONV_CH] * _sigmoid(glu[:, CONV_CH:])
    cos = cos_ref[...]
    sin = sin_ref[...]
    q = proj(_C_Q)
    q = q * cos + _swap16(q) * sin
    q_ref[...] = (q * (DA_HEAD_DIM ** -0.5 * math.log2(math.e))).astype(BF16)
    k = proj(_C_K)
    k_ref[...] = k * cos + _swap16(k) * sin
    v_ref[...] = proj(_C_V)


def _inproj_call(l_arr, x, mod6, norm1_g, w_proj, b_proj, cos_tab, sin_tab):
    tok = lambda n: pl.BlockSpec((TM, n), lambda i, l: (i, 0))
    rope_blk = lambda i, l: (jnp.where(i < N_PROMPT_TILES, TILES_PER_LATENT_SEQ,
                                       (i - N_PROMPT_TILES) % TILES_PER_LATENT_SEQ), 0)
    grid_spec = pltpu.PrefetchScalarGridSpec(
        num_scalar_prefetch=1,
        grid=(N_TILES,),
        in_specs=[
            tok(D),
            pl.BlockSpec((None, None, 6, D), lambda i, l: (l[0], _mod_row(i), 0, 0)),
            pl.BlockSpec((None, 1, D), lambda i, l: (l[0], 0, 0)),
            pl.BlockSpec((None, D, N_PROJ), lambda i, l: (l[0], 0, 0)),
            pl.BlockSpec((None, 1, N_PROJ), lambda i, l: (l[0], 0, 0)),
            pl.BlockSpec((TM, DA_QK), rope_blk),
            pl.BlockSpec((TM, DA_QK), rope_blk),
        ],
        out_specs=[tok(D), tok(D), tok(SSD_XBC), tok(LANES), tok(CONV_CH), tok(DA_QK), tok(DA_QK), tok(DA_V)],
    )
    sds = lambda n, dt: jax.ShapeDtypeStruct((T, n), dt)
    return pl.pallas_call(
        _inproj_kernel,
        grid_spec=grid_spec,
        out_shape=[sds(D, BF16), sds(D, BF16), sds(SSD_XBC, F32), sds(LANES, F32), sds(CONV_CH, F32),
                   sds(DA_QK, BF16), sds(DA_QK, F32), sds(DA_V, F32)],
        compiler_params=_cparams(("arbitrary",)),
        name="inproj",
    )(l_arr, x, mod6, norm1_g, w_proj, b_proj, cos_tab, sin_tab)


N_CHUNKS = T // SSD_CHUNK
N_PROMPT_CHUNKS = T_P // SSD_CHUNK
CH_PER_PROMPT = PROMPT_LEN // SSD_CHUNK
CH_PER_LATENT = LATENT_LEN // SSD_CHUNK
HALO = SUBLANES
STATE_DUMP = N_PROMPT_SEQ


def _ssd_tables():
    cidx = np.zeros((2, N_CHUNKS), np.int32)
    flags = np.zeros((2, N_CHUNKS), np.int32)
    h0 = np.zeros((2, N_CHUNKS), np.int32)
    so = np.zeros((2, N_CHUNKS), np.int32)
    for d in range(2):
        for j in range(N_CHUNKS):
            c = j if d == 0 else N_CHUNKS - 1 - j
            if c < N_PROMPT_CHUNKS:
                seq, pos, n = c // CH_PER_PROMPT, c % CH_PER_PROMPT, CH_PER_PROMPT
                h0i, soi = N_LATENT_SEQ, seq
            else:
                cc = c - N_PROMPT_CHUNKS
                seq, pos, n = cc // CH_PER_LATENT, cc % CH_PER_LATENT, CH_PER_LATENT
                h0i, soi = seq, STATE_DUMP
            first = pos == 0 if d == 0 else pos == n - 1
            cidx[d, j] = c
            flags[d, j] = int(first) | (int(pos > 0) << 1) | (int(pos < n - 1) << 2)
            h0[d, j] = h0i
            so[d, j] = soi
    return [jnp.asarray(a.reshape(-1)) for a in (cidx, flags, h0, so)]


def _ssd_kernel(l_ref, cidx_ref, flags_ref, h0i_ref, soi_ref,
                xc_ref, xp_ref, xn_ref, dt_ref, cw_ref, cb_ref, dtb_ref, alog_ref, dskip_ref,
                e_ref, h0_ref,
                y_ref, so_ref,
                state, xpad, lat_s):
    d = pl.program_id(0)
    j = pl.program_id(1)
    flags = flags_ref[d * N_CHUNKS + j]
    q = SSD_CHUNK

    @pl.when((flags & 1) == 1)
    def _():
        state[...] = h0_ref[...]

    xpad[0:HALO, :] = jnp.where(((flags >> 1) & 1) == 1, xp_ref[...], 0.0)
    xpad[HALO:HALO + q, :] = xc_ref[...]
    xpad[HALO + q:HALO + q + HALO, :] = jnp.where(((flags >> 2) & 1) == 1, xn_ref[...], 0.0)
    acc = jnp.zeros((q, SSD_XBC), F32) + cb_ref[...]
    pad = SSD_CONV_W // 2
    xp = xpad[...]
    rows = q + 2 * HALO
    for k in range(SSD_CONV_W):
        shifted = xp if k == pad else pltpu.roll(xp, (pad - k) % rows, axis=0)
        acc = acc + cw_ref[k:k + 1, :] * shifted[HALO:HALO + q, :]
    xc = _silu(acc)
    xs = xc[:, :D]
    bm = xc[:, D:D + LANES]
    cm = xc[:, D + LANES:D + 2 * LANES]

    dt_all = _softplus(dt_ref[...] + dtb_ref[...])
    da_all = dt_all * (-jnp.exp(alog_ref[...]))
    is_fwd = d == 0
    dt = jnp.where(is_fwd, dt_all, pltpu.roll(dt_all, LANES - SSD_HEADS, axis=1))
    da = jnp.where(is_fwd, da_all, pltpu.roll(da_all, LANES - SSD_HEADS, axis=1))
    row = lax.broadcasted_iota(jnp.int32, (q, q), 0)
    col = lax.broadcasted_iota(jnp.int32, (q, q), 1)
    tri = (row - col) * (1 - 2 * d) >= 0
    p = _dot(jnp.where(tri, 1.0, 0.0).astype(BF16), _split3(da))
    la = (p[:, :LANES] + p[:, LANES:2 * LANES]) + p[:, 2 * LANES:]
    lat_s[...] = la.T
    e3 = e_ref[...]
    la_exp = _dot(_split3(la), e3)
    dt_exp = _dot(_split3(dt), e3)
    la_end = jnp.where(is_fwd, la_exp[q - 1:q, :], la_exp[0:1, :])
    decay_end = jnp.exp(la_end - la_exp)
    chunk_decay = jnp.exp(la_end)
    decay_in = jnp.exp(la_exp)
    xdt = xs * dt_exp
    xdt_b = xdt.astype(BF16)
    xdtw_b = (xdt * decay_end).astype(BF16)
    bmt = bm.T
    skip = jnp.where(is_fwd, dskip_ref[...], 0.0)

    half = LANES // 2
    lane = lax.broadcasted_iota(jnp.int32, (q, LANES), 1)
    hpg = SSD_HEADS // SSD_GROUPS
    for g in range(SSD_GROUPS):
        c_g = cm[:, g * half:(g + 1) * half].astype(BF16)
        b_g = bm[:, g * half:(g + 1) * half].astype(BF16)
        bt_g = bmt[g * half:(g + 1) * half, :].astype(BF16)
        cb = _dot_nt(c_g, b_g)
        for pp in range(hpg // 2):
            h_a = g * hpg + 2 * pp
            sl = slice(h_a * SSD_HEAD_DIM, (h_a + 2) * SSD_HEAD_DIM)
            xdt_p = xdt_b[:, sl]
            yd = []
            for h in (h_a, h_a + 1):
                seg = la[:, h:h + 1] - lat_s[h:h + 1, :]
                s_h = (cb * jnp.exp(jnp.where(tri, seg, NEG))).astype(BF16)
                yd.append(_dot(s_h, xdt_p))
            y_diag = jnp.where(lane < half, yd[0], yd[1])
            st_in = state[:, sl]
            y_off = _dot(c_g, st_in.astype(BF16)) * decay_in[:, sl]
            state[:, sl] = st_in * chunk_decay[:, sl] + _dot(bt_g, xdtw_b[:, sl])
            y = y_diag + y_off + xs[:, sl] * skip[:, sl]
            y_ref[:, sl] = y.astype(BF16)
    so_ref[...] = state[...]


def _ssd_call(l_arr, tabs, xbc, dt, conv_w, conv_b, dt_bias, a_log, dskip, e_mat, h0t):
    nb8 = T // HALO
    per = SSD_CHUNK // HALO
    cur = lambda d, j, l, ci, fl, h0, so: (ci[d * N_CHUNKS + j], 0)
    prev = lambda d, j, l, ci, fl, h0, so: (jnp.maximum(ci[d * N_CHUNKS + j] * per - 1, 0), 0)
    nxt = lambda d, j, l, ci, fl, h0, so: (jnp.minimum((ci[d * N_CHUNKS + j] + 1) * per, nb8 - 1), 0)
    lay = lambda *shape: pl.BlockSpec((None,) + shape, lambda d, j, l, *_: (l[0],) + (0,) * len(shape))
    const = lambda *shape: pl.BlockSpec(shape, lambda d, j, *_: (0,) * len(shape))
    grid_spec = pltpu.PrefetchScalarGridSpec(
        num_scalar_prefetch=5,
        grid=(2, N_CHUNKS),
        in_specs=[
            pl.BlockSpec((SSD_CHUNK, SSD_XBC), cur),
            pl.BlockSpec((HALO, SSD_XBC), prev),
            pl.BlockSpec((HALO, SSD_XBC), nxt),
            pl.BlockSpec((SSD_CHUNK, LANES), cur),
            lay(SSD_CONV_W, SSD_XBC),
            lay(1, SSD_XBC),
            lay(1, LANES),
            lay(1, LANES),
            lay(1, D),
            const(3 * LANES, D),
            pl.BlockSpec((None, None, None, SSD_STATE, D),
                         lambda d, j, l, ci, fl, h0, so: (l[0], h0[d * N_CHUNKS + j], d, 0, 0)),
        ],
        out_specs=[
            pl.BlockSpec((None, SSD_CHUNK, D), lambda d, j, l, ci, fl, h0, so: (d, ci[d * N_CHUNKS + j], 0)),
            pl.BlockSpec((None, None, SSD_STATE, D),
                         lambda d, j, l, ci, fl, h0, so: (so[d * N_CHUNKS + j], d, 0, 0)),
        ],
        scratch_shapes=[
            pltpu.VMEM((SSD_STATE, D), F32),
            pltpu.VMEM((SSD_CHUNK + 2 * HALO, SSD_XBC), F32),
            pltpu.VMEM((LANES, SSD_CHUNK), F32),
        ],
    )
    return pl.pallas_call(
        _ssd_kernel,
        grid_spec=grid_spec,
        out_shape=[jax.ShapeDtypeStruct((2, T, D), BF16),
                   jax.ShapeDtypeStruct((N_PROMPT_SEQ + 1, 2, SSD_STATE, D), F32)],
        compiler_params=_cparams(("arbitrary", "arbitrary")),
        name="ssd_scan",
    )(l_arr, *tabs, xbc, xbc, xbc, dt, conv_w, conv_b, dt_bias, a_log, dskip, e_mat, h0t)


CV_HALO = 16


def _cv_kernel(l_ref, uc_ref, up_ref, un_ref, w_ref, b_ref, g_ref, beta_ref, o_ref, upad):
    i = pl.program_id(0)
    pos = (i - N_PROMPT_TILES) % TILES_PER_LATENT_SEQ
    is_prompt = i < N_PROMPT_TILES
    no_l = jnp.logical_or(is_prompt, pos == 0)
    no_r = jnp.logical_or(is_prompt, pos == TILES_PER_LATENT_SEQ - 1)
    upad[0:CV_HALO, :] = jnp.where(no_l, 0.0, up_ref[...])
    upad[CV_HALO:CV_HALO + TM, :] = uc_ref[...]
    upad[CV_HALO + TM:CV_HALO + TM + CV_HALO, :] = jnp.where(no_r, 0.0, un_ref[...])
    acc = jnp.zeros((TM, CONV_CH), F32) + b_ref[...]
    pad = CONV_W // 2
    up = upad[...]
    rows = TM + 2 * CV_HALO
    for rot in range(SUBLANES):
        taps = [k for k in range(CONV_W) if (CV_HALO - pad + k) % SUBLANES == rot]
        rolled = up if rot == 0 else pltpu.roll(up, rows - rot, axis=0)
        for k in taps:
            base = CV_HALO - pad + k - rot
            acc = acc + w_ref[k:k + 1, :] * rolled[base:base + TM, :]
    mu = jnp.mean(acc, axis=-1, keepdims=True)
    xc = acc - mu
    var = jnp.mean(xc * xc, axis=-1, keepdims=True)
    y = xc * lax.rsqrt(var + EPS) * g_ref[...] + beta_ref[...]
    o_ref[...] = _silu(y).astype(BF16)


def _cv_call(l_arr, u, w, b, g, beta):
    per = TM // CV_HALO
    nb = T // CV_HALO
    lay = lambda *shape: pl.BlockSpec((None,) + shape, lambda i, l: (l[0],) + (0,) * len(shape))
    grid_spec = pltpu.PrefetchScalarGridSpec(
        num_scalar_prefetch=1,
        grid=(N_TILES,),
        in_specs=[
            pl.BlockSpec((TM, CONV_CH), lambda i, l: (i, 0)),
            pl.BlockSpec((CV_HALO, CONV_CH), lambda i, l: (jnp.maximum(i * per - 1, 0), 0)),
            pl.BlockSpec((CV_HALO, CONV_CH), lambda i, l: (jnp.minimum((i + 1) * per, nb - 1), 0)),
            lay(CONV_W, CONV_CH), lay(1, CONV_CH), lay(1, CONV_CH), lay(1, CONV_CH),
        ],
        out_specs=pl.BlockSpec((TM, CONV_CH), lambda i, l: (i, 0)),
        scratch_shapes=[pltpu.VMEM((TM + 2 * CV_HALO, CONV_CH), F32)],
    )
    return pl.pallas_call(
        _cv_kernel,
        grid_spec=grid_spec,
        out_shape=jax.ShapeDtypeStruct((T, CONV_CH), BF16),
        compiler_params=_cparams(("arbitrary",)),
        name="conformer_conv",
    )(l_arr, u, u, u, w, b, g, beta)


def _lambda_terms(l_ref, lam_ref):
    lf = jnp.full((1, 1), l_ref[0], jnp.int32).astype(F32)
    lam_init = 0.8 - 0.6 * jnp.exp(-0.3 * lf)
    p = lam_ref[...]
    s1 = jnp.sum(p[0:1, :] * p[1:2, :], axis=-1, keepdims=True)
    s2 = jnp.sum(p[2:3, :] * p[3:4, :], axis=-1, keepdims=True)
    lam = jnp.exp(s1) - jnp.exp(s2) + lam_init
    return lam, 1.0 - lam_init


def _attn_body(l_ref, q_ref, k_ref, v_ref, ck_ref, cv_ref, lam_ref, g_ref, o_ref, k_s, v_s, n_ctx):
    @pl.when(pl.program_id(1) == 0)
    def _():
        if n_ctx:
            k_s[0:n_ctx, :] = ck_ref[...].astype(BF16)
        k_s[n_ctx:, :] = k_ref[...].astype(BF16)
        for h in range(DA_HEADS):
            sl = slice(h * LANES, (h + 1) * LANES)
            if n_ctx:
                v_s[0:n_ctx, 2 * h * LANES:(2 * h + 1) * LANES] = cv_ref[:, sl].astype(BF16)
            v_s[n_ctx:, 2 * h * LANES:(2 * h + 1) * LANES] = v_ref[:, sl].astype(BF16)
            v_s[:, (2 * h + 1) * LANES:(2 * h + 2) * LANES] = jnp.ones((v_s.shape[0], LANES), BF16)

    lam, out_scale = _lambda_terms(l_ref, lam_ref)
    tq = q_ref.shape[0]
    lane = lax.broadcasted_iota(jnp.int32, (tq, LANES), 1)
    zero = jnp.zeros((tq, LANES), BF16)
    for h in range(DA_HEADS):
        sl = slice(h * LANES, (h + 1) * LANES)
        qh = q_ref[:, sl]
        kh = k_s[:, sl]
        vh = v_s[:, 2 * h * LANES:(2 * h + 2) * LANES]
        outs = []
        for c in range(2):
            in_c = (lane < DA_HEAD_DIM) if c == 0 else (lane >= DA_HEAD_DIM)
            s = _dot_nt(jnp.where(in_c, qh, zero), kh)
            m = jnp.max(s, axis=-1, keepdims=True)
            pv = _dot(jnp.exp2(s - m).astype(BF16), vh)
            outs.append(pv[:, :LANES] / pv[:, LANES:])
        o = outs[0] - lam * outs[1]
        o = o * lax.rsqrt(jnp.mean(o * o, axis=-1, keepdims=True) + EPS)
        o_ref[:, sl] = (o * g_ref[...] * out_scale).astype(BF16)


def _attn_prompt_kernel(l_ref, q_ref, k_ref, v_ref, lam_ref, g_ref, o_ref, k_s, v_s):
    _attn_body(l_ref, q_ref, k_ref, v_ref, None, None, lam_ref, g_ref, o_ref, k_s, v_s, 0)


def _attn_latent_kernel(l_ref, q_ref, k_ref, v_ref, ck_ref, cv_ref, lam_ref, g_ref, o_ref, k_s, v_s):
    _attn_body(l_ref, q_ref, k_ref, v_ref, ck_ref, cv_ref, lam_ref, g_ref, o_ref, k_s, v_s, PAST_LEN)


def _attn_call(l_arr, q, k, v, cache_k, cache_v, da_lambda, subln_g, latent):
    if latent:
        nseq, seqlen, n_ctx = N_LATENT_SEQ, LATENT_LEN, PAST_LEN
        tile0, seq0 = N_PROMPT_TILES, T_P // LATENT_LEN
    else:
        nseq, seqlen, n_ctx = N_PROMPT_SEQ, PROMPT_LEN, 0
        tile0, seq0 = 0, 0
    nq = seqlen // TM
    lay = lambda *shape: pl.BlockSpec((None,) + shape, lambda b, i, l: (l[0],) + (0,) * len(shape))
    in_specs = [
        pl.BlockSpec((TM, DA_QK), lambda b, i, l: (tile0 + b * nq + i, 0)),
        pl.BlockSpec((seqlen, DA_QK), lambda b, i, l: (seq0 + b, 0)),
        pl.BlockSpec((seqlen, DA_V), lambda b, i, l: (seq0 + b, 0)),
    ]
    args = [q, k, v]
    if latent:
        ctx = pl.BlockSpec((None, None, PAST_LEN, DA_QK), lambda b, i, l: (b, l[0], 0, 0))
        in_specs += [ctx, ctx]
        args += [cache_k, cache_v]
    in_specs += [lay(4, DA_HEAD_DIM), lay(1, DA_V_DIM)]
    args += [da_lambda, subln_g]
    grid_spec = pltpu.PrefetchScalarGridSpec(
        num_scalar_prefetch=1,
        grid=(nseq, nq),
        in_specs=in_specs,
        out_specs=pl.BlockSpec((TM, DA_V), lambda b, i, l: (b * nq + i, 0)),
        scratch_shapes=[pltpu.VMEM((n_ctx + seqlen, DA_QK), BF16), pltpu.VMEM((n_ctx + seqlen, 2 * DA_V), BF16)],
    )
    return pl.pallas_call(
        _attn_latent_kernel if latent else _attn_prompt_kernel,
        grid_spec=grid_spec,
        out_shape=jax.ShapeDtypeStruct((nseq * seqlen, DA_V), BF16),
        compiler_params=_cparams(("arbitrary", "arbitrary")),
        name="diff_attn_latent" if latent else "diff_attn_prompt",
    )(l_arr, *args)


def _combine_kernel(l_ref, x_ref, h_ref, y_ref, z_ref, uc_ref, oa_ref, mod_ref,
                    wg_ref, bg_ref, sg_ref, wa_ref, wb_ref, wc_ref, wo_ref, g2_ref, wr_ref, br_ref,
                    x1_ref, h2_ref, route_ref, cnt_ref, e1_ref, e2_ref, r1_ref, r2_ref, carry):
    i = pl.program_id(0)

    @pl.when(i == 0)
    def _():
        carry[...] = jnp.zeros_like(carry)

    y = (y_ref[0].astype(F32) + y_ref[1].astype(F32)) * _silu(z_ref[...].astype(F32))
    y = y * lax.rsqrt(jnp.mean(y * y, axis=-1, keepdims=True) + EPS) * sg_ref[...]
    br_a = _dot(y.astype(BF16), wa_ref[...])
    br_b = _dot(uc_ref[...], wb_ref[...])
    br_c = _dot(oa_ref[...], wc_ref[...])
    hb = h_ref[...]

    def gate(n):
        return _sigmoid(_dot(hb, wg_ref[:, n * D:(n + 1) * D]) + bg_ref[:, n * D:(n + 1) * D])

    mix = gate(0) * br_a + gate(1) * br_b + gate(2) * br_c
    mixed = _dot(mix.astype(BF16), wo_ref[...])
    x1 = x_ref[...] + mod_ref[2:3, :] * mixed
    x1_ref[...] = x1
    xn = x1 * lax.rsqrt(jnp.mean(x1 * x1, axis=-1, keepdims=True) + EPS)
    h2 = xn * g2_ref[...] * (1.0 + mod_ref[4:5, :]) + mod_ref[3:4, :]
    h2_ref[...] = h2

    h2_hi = h2.astype(BF16)
    h2_mid = (h2 - h2_hi.astype(F32)).astype(BF16)
    logits = _dot(jnp.concatenate([h2_hi, h2_mid, h2_hi], axis=1), wr_ref[...]) + br_ref[...]
    lane = lax.broadcasted_iota(jnp.int32, (TM, LANES), 1).astype(F32)

    def first_argmax(vals, vmax):
        return jnp.min(jnp.where(vals == vmax, lane, float(LANES)), axis=-1, keepdims=True)

    glog = jnp.where(lane < MOE_GROUPS, logits, NEG)
    gmax = jnp.max(glog, axis=-1, keepdims=True)
    gsel = first_argmax(glog, gmax)
    p_g = 1.0 / jnp.sum(jnp.exp(glog - gmax), axis=-1, keepdims=True)
    lo = ROUTER_LANE0 + MOE_EPG * gsel
    elog = jnp.where(jnp.logical_and(lane >= lo, lane < lo + MOE_EPG), logits, NEG)
    v1 = jnp.max(elog, axis=-1, keepdims=True)
    i1 = first_argmax(elog, v1)
    elog2 = jnp.where(lane == i1, NEG, elog)
    v2 = jnp.max(elog2, axis=-1, keepdims=True)
    i2 = first_argmax(elog2, v2)
    t2 = jnp.exp(v2 - v1)
    w1 = p_g / (1.0 + t2)
    w2 = p_g * t2 / (1.0 + t2)

    oh1 = lane == i1
    oh2 = lane == i2
    both = jnp.where(jnp.logical_or(oh1, oh2), 1.0, 0.0)
    r = lax.broadcasted_iota(jnp.int32, (TM, TM), 0)
    c = lax.broadcasted_iota(jnp.int32, (TM, TM), 1)
    strict_lower = jnp.where(c < r, 1.0, 0.0).astype(BF16)
    before = _dot(strict_lower, both.astype(BF16)) + carry[0:1, :]
    r1 = jnp.sum(jnp.where(oh1, before, 0.0), axis=-1, keepdims=True)
    r2 = jnp.sum(jnp.where(oh2, before, 0.0), axis=-1, keepdims=True)
    carry[...] = carry[...] + jnp.sum(both, axis=0, keepdims=True)
    cnt_ref[...] = carry[...]

    e1 = i1 - ROUTER_LANE0
    e2 = i2 - ROUTER_LANE0
    route = jnp.zeros((TM, LANES), F32)
    for n, val in enumerate((e1, e2, w1, w2, r1, r2)):
        route = jnp.where(lane == float(n), val, route)
    route_ref[...] = route
    route_t = route.T
    for n, ref in ((0, e1_ref), (1, e2_ref), (4, r1_ref), (5, r2_ref)):
        ref[...] = route_t[n:n + 1, :].astype(jnp.int32)


def _combine_call(l_arr, x, h, y2, z, uc, oa, mod6, w_gates, b_gates, ssd_norm_g, w_br_ssd, w_br_conv,
                  w_br_attn, w_out, norm2_g, w_router, b_router):
    tok = lambda n: pl.BlockSpec((TM, n), lambda i, l: (i, 0))
    lay = lambda *shape: pl.BlockSpec((None,) + shape, lambda i, l: (l[0],) + (0,) * len(shape))
    grid_spec = pltpu.PrefetchScalarGridSpec(
        num_scalar_prefetch=1,
        grid=(N_TILES,),
        in_specs=[
            tok(D), tok(D),
            pl.BlockSpec((2, TM, D), lambda i, l: (0, i, 0)),
            tok(D), tok(CONV_CH), tok(DA_V),
            pl.BlockSpec((None, None, 6, D), lambda i, l: (l[0], _mod_row(i), 0, 0)),
            lay(D, 3 * D), lay(1, 3 * D), lay(1, D), lay(D, D), lay(CONV_CH, D), lay(DA_V, D), lay(D, D),
            lay(1, D), lay(3 * D, LANES), lay(1, LANES),
        ],
        out_specs=[tok(D), tok(D), tok(LANES), pl.BlockSpec((SUBLANES, LANES), lambda i, l: (0, 0))]
        + [pl.BlockSpec((None, 1, TM), lambda i, l: (i, 0, 0))] * 4,
        scratch_shapes=[pltpu.VMEM((SUBLANES, LANES), F32)],
    )
    return pl.pallas_call(
        _combine_kernel,
        grid_spec=grid_spec,
        out_shape=[jax.ShapeDtypeStruct((T, D), F32), jax.ShapeDtypeStruct((T, D), F32),
                   jax.ShapeDtypeStruct((T, LANES), F32), jax.ShapeDtypeStruct((SUBLANES, LANES), F32)]
        + [jax.ShapeDtypeStruct((N_TILES, 1, TM), jnp.int32)] * 4,
        compiler_params=_cparams(("arbitrary",)),
        name="branch_combine_router",
    )(l_arr, x, h, y2, z, uc, oa, mod6, w_gates, b_gates, ssd_norm_g, w_br_ssd, w_br_conv, w_br_attn,
      w_out, norm2_g, w_router, b_router)


ROW_DMA_UNROLL = 8


def _dispatch_kernel(e1_ref, e2_ref, r1_ref, r2_ref, ps_ref, pe_ref, h2_ref, xs_ref, zbuf, sem, zsem):
    i = pl.program_id(0)
    base = i * TM

    @pl.when(i == 0)
    def _():
        zbuf[...] = jnp.zeros_like(zbuf)

        def zero_fill(e):
            off = pl.multiple_of(pe_ref[e] - MOE_BLOCK, MOE_BLOCK)
            return pltpu.make_async_copy(zbuf, xs_ref.at[pl.ds(off, MOE_BLOCK), :], zsem)

        def unused_fill(b):
            off = pl.multiple_of(b * MOE_BLOCK, MOE_BLOCK)
            return pltpu.make_async_copy(zbuf, xs_ref.at[pl.ds(off, MOE_BLOCK), :], zsem)

        n_used = pe_ref[MOE_E - 1] // MOE_BLOCK
        for e in range(MOE_E):
            @pl.when(pe_ref[e] > ps_ref[e])
            def _():
                zero_fill(e).start()
        lax.fori_loop(n_used, N_SLOT_BLOCKS, lambda b, c: (unused_fill(b).start(), c)[1], 0)
        for e in range(MOE_E):
            @pl.when(pe_ref[e] > ps_ref[e])
            def _():
                zero_fill(e).wait()
        lax.fori_loop(n_used, N_SLOT_BLOCKS, lambda b, c: (unused_fill(b).wait(), c)[1], 0)

    def issue(t, carry):
        for e_ref, r_ref in ((e1_ref, r1_ref), (e2_ref, r2_ref)):
            dest = ps_ref[e_ref[base + t]] + r_ref[base + t]
            pltpu.make_async_copy(h2_ref.at[pl.ds(t, 1), :], xs_ref.at[pl.ds(dest, 1), :], sem).start()
        return carry

    lax.fori_loop(0, TM, issue, 0, unroll=ROW_DMA_UNROLL)
    for _ in range(2):
        pltpu.make_async_copy(h2_ref, xs_ref.at[pl.ds(0, TM), :], sem).wait()


def _dispatch_call(e1, e2, r1, r2, pstart, pend, h2):
    grid_spec = pltpu.PrefetchScalarGridSpec(
        num_scalar_prefetch=6,
        grid=(N_TILES,),
        in_specs=[pl.BlockSpec((TM, D), lambda i, *_: (i, 0))],
        out_specs=pl.BlockSpec(memory_space=pl.ANY),
        scratch_shapes=[pltpu.VMEM((MOE_BLOCK, D), F32), pltpu.SemaphoreType.DMA(()), pltpu.SemaphoreType.DMA(())],
    )
    return pl.pallas_call(
        _dispatch_kernel,
        grid_spec=grid_spec,
        out_shape=jax.ShapeDtypeStruct((N_SLOTS, D), F32),
        compiler_params=_cparams(("arbitrary",)),
        name="moe_dispatch",
    )(e1, e2, r1, r2, pstart, pend, h2)


def _moe_kernel(l_ref, be_ref, nb_ref, xs_ref, wg_ref, wu_ref, wd_ref, ys_ref, wg_s, wu_s, wd_s):
    i = pl.program_id(0)
    prev = be_ref[jnp.maximum(i - 1, 0)]

    @pl.when(jnp.logical_or(i == 0, be_ref[i] != prev))
    def _():
        wg_s[...] = wg_ref[...].astype(BF16)
        wu_s[...] = wu_ref[...].astype(BF16)
        wd_s[...] = wd_ref[...].astype(BF16)

    @pl.when(i < nb_ref[0])
    def _():
        xb = xs_ref[...].astype(BF16)
        hid = _silu(_dot(xb, wg_s[...])) * _dot(xb, wu_s[...])
        ys_ref[...] = _dot(hid.astype(BF16), wd_s[...])

    @pl.when(i >= nb_ref[0])
    def _():
        ys_ref[...] = jnp.zeros_like(ys_ref)


def _moe_call(l_arr, block_expert, n_used, xs, w_gate, w_up, w_down):
    wspec = lambda a, b: pl.BlockSpec((None, None, a, b), lambda i, l, be, nb: (l[0], be[i], 0, 0))
    grid_spec = pltpu.PrefetchScalarGridSpec(
        num_scalar_prefetch=3,
        grid=(N_SLOT_BLOCKS,),
        in_specs=[pl.BlockSpec((MOE_BLOCK, D), lambda i, l, be, nb: (jnp.minimum(i, nb[0] - 1), 0)),
                  wspec(D, MOE_HIDDEN), wspec(D, MOE_HIDDEN), wspec(MOE_HIDDEN, D)],
        out_specs=pl.BlockSpec((MOE_BLOCK, D), lambda i, l, be, nb: (i, 0)),
        scratch_shapes=[pltpu.VMEM((D, MOE_HIDDEN), BF16), pltpu.VMEM((D, MOE_HIDDEN), BF16),
                        pltpu.VMEM((MOE_HIDDEN, D), BF16)],
    )
    return pl.pallas_call(
        _moe_kernel,
        grid_spec=grid_spec,
        out_shape=jax.ShapeDtypeStruct((N_SLOTS, D), F32),
        compiler_params=_cparams(("arbitrary",)),
        name="moe_experts",
    )(l_arr, block_expert, n_used, xs, w_gate, w_up, w_down)


def _moe_combine_kernel(e1_ref, e2_ref, r1_ref, r2_ref, ps_ref, x1_ref, route_ref, mod_ref, fg_ref, ys_ref,
                        o_ref, buf, sem, *, final):
    base = pl.program_id(0) * TM

    def issue(t, carry):
        for which, (e_ref, r_ref) in enumerate(((e1_ref, r1_ref), (e2_ref, r2_ref))):
            src = ps_ref[e_ref[base + t]] + r_ref[base + t]
            pltpu.make_async_copy(ys_ref.at[pl.ds(src, 1), :], buf.at[which, pl.ds(t, 1), :], sem).start()
        return carry

    lax.fori_loop(0, TM, issue, 0, unroll=ROW_DMA_UNROLL)
    for which in range(2):
        pltpu.make_async_copy(ys_ref.at[pl.ds(0, TM), :], buf.at[which], sem).wait()
    w1 = route_ref[:, 2:3]
    w2 = route_ref[:, 3:4]
    y = buf[0] * w1 + buf[1] * w2
    x2 = x1_ref[...] + mod_ref[5:6, :] * y
    if final:
        x2 = x2 * lax.rsqrt(jnp.mean(x2 * x2, axis=-1, keepdims=True) + EPS) * fg_ref[...]
    o_ref[...] = x2


def _moe_combine_call(l_arr, e1, e2, r1, r2, pstart, x1, route, mod6, final_g, ys, final):
    grid_spec = pltpu.PrefetchScalarGridSpec(
        num_scalar_prefetch=6,
        grid=(N_TILES,),
        in_specs=[
            pl.BlockSpec((TM, D), lambda i, *_: (i, 0)),
            pl.BlockSpec((TM, LANES), lambda i, *_: (i, 0)),
            pl.BlockSpec((None, None, 6, D), lambda i, l, *_: (l[0], _mod_row(i), 0, 0)),
            pl.BlockSpec((1, D), lambda i, *_: (0, 0)),
            pl.BlockSpec(memory_space=pl.ANY),
        ],
        out_specs=pl.BlockSpec((TM, D), lambda i, *_: (i, 0)),
        scratch_shapes=[pltpu.VMEM((2, TM, D), F32), pltpu.SemaphoreType.DMA(())],
    )

    def body(l_ref, *rest):
        del l_ref
        _moe_combine_kernel(*rest, final=final)

    return pl.pallas_call(
        body,
        grid_spec=grid_spec,
        out_shape=jax.ShapeDtypeStruct((T, D), F32),
        compiler_params=_cparams(("arbitrary",)),
        name="moe_combine_final" if final else "moe_combine",
    )(l_arr, e1, e2, r1, r2, pstart, x1, route, mod6, final_g, ys)


def _rope_tables():
    n = LATENT_LEN
    rows = n // GRID_W
    row = jnp.repeat(jnp.arange(rows), GRID_W).astype(F32)
    col = jnp.tile(jnp.arange(GRID_W), rows).astype(F32)
    axis_dim = DA_HEAD_DIM // 2
    inv_freq = 1.0 / (ROPE_BASE ** (jnp.arange(0, axis_dim, 2, dtype=F32) / axis_dim))
    ar, ac = row[:, None] * inv_freq, col[:, None] * inv_freq
    cos64 = jnp.concatenate([jnp.cos(ar), jnp.cos(ar), jnp.cos(ac), jnp.cos(ac)], axis=1)
    sin64 = jnp.concatenate([-jnp.sin(ar), jnp.sin(ar), -jnp.sin(ac), jnp.sin(ac)], axis=1)
    reps = DA_QK // DA_HEAD_DIM
    cos = jnp.concatenate([jnp.tile(cos64, (1, reps)), jnp.ones((TM, DA_QK), F32)], axis=0)
    sin = jnp.concatenate([jnp.tile(sin64, (1, reps)), jnp.zeros((TM, DA_QK), F32)], axis=0)
    return cos, sin


def _pad_lanes(a, n=LANES):
    return jnp.pad(a, [(0, 0)] * (a.ndim - 1) + [(0, n - a.shape[-1])])


def kernel(x_prompt, x_sample, cache_k, cache_v, state_ssd, c, c_ctx, w_ada, b_ada, norm1_g, norm2_g, w_in, b_in,
           ssd_conv_w, ssd_conv_b, ssd_dt_bias, ssd_a_log, ssd_d, ssd_norm_g, w_br_ssd, cv_dw_w, cv_dw_b, cv_ln_g,
           cv_ln_b, w_br_conv, da_lambda, da_subln_g, w_br_attn, w_out, moe_w_group, moe_b_group, moe_w_expert,
           moe_b_expert, moe_w_gate, moe_w_up, moe_w_down, final_g):
    L = DEPTH
    x = jnp.concatenate([x_prompt.reshape(T_P, D), x_sample.reshape(T_L, D)], axis=0)

    cvec = jnp.concatenate([c, c_ctx[None, :], jnp.zeros((16 - N_LATENT_SEQ - 1, D), F32)], axis=0)
    mod6 = _ada_call(cvec, w_ada, b_ada).reshape(L, 16, 6, D)

    o_z, o_xbc, o_dt = 0, D, D + SSD_XBC
    o_glu = o_dt + 2 * SSD_HEADS
    o_q = o_glu + 2 * CONV_CH
    o_k, o_v, o_g = o_q + DA_QK, o_q + 2 * DA_QK, o_q + 2 * DA_QK + DA_V

    def regroup(w):
        return jnp.concatenate([w[..., o_z:o_dt], _pad_lanes(w[..., o_dt:o_glu]), w[..., o_glu:o_g]], axis=-1)

    w_proj = regroup(w_in).astype(BF16)
    b_proj = regroup(b_in).reshape(L, 1, N_PROJ)
    w_gates = w_in[..., o_g:].astype(BF16)
    b_gates = b_in[..., o_g:].reshape(L, 1, 3 * D)
    cos_tab, sin_tab = _rope_tables()

    ssd_tabs = _ssd_tables()
    dt_bias = _pad_lanes(ssd_dt_bias.reshape(L, 1, 2 * SSD_HEADS))
    a_log = _pad_lanes(ssd_a_log.reshape(L, 1, 2 * SSD_HEADS))
    dskip = jnp.repeat(ssd_d, SSD_HEAD_DIM, axis=-1).reshape(L, 1, D)
    hp = np.arange(D) // SSD_HEAD_DIM
    e_mat = jnp.asarray(np.tile(np.arange(LANES)[:, None] == hp[None, :], (3, 1)).astype(np.float32)).astype(BF16)
    h0t = jnp.transpose(state_ssd, (1, 0, 2, 5, 3, 4)).reshape(L, N_LATENT_SEQ, 2, SSD_STATE, D)
    h0t = jnp.concatenate([h0t, jnp.zeros((L, 1, 2, SSD_STATE, D), F32)], axis=1)

    ck = cache_k.reshape(N_LATENT_SEQ, L, PAST_LEN, DA_QK)
    cv = cache_v.reshape(N_LATENT_SEQ, L, PAST_LEN, DA_V)
    w_router = _pad_lanes(jnp.concatenate([moe_w_group, moe_w_expert], axis=-1))
    wr_hi = w_router.astype(BF16)
    wr_mid = (w_router - wr_hi.astype(F32)).astype(BF16)
    w_router = jnp.concatenate([wr_hi, wr_hi, wr_mid], axis=1)
    b_router = _pad_lanes(jnp.concatenate([moe_b_group, moe_b_expert], axis=-1)).reshape(L, 1, LANES)
    w_br_ssd_b, w_br_conv_b = w_br_ssd.astype(BF16), w_br_conv.astype(BF16)
    w_br_attn_b, w_out_b = w_br_attn.astype(BF16), w_out.astype(BF16)
    r3 = lambda a: a.reshape(L, 1, a.shape[-1])

    ks_new, vs_new, ss_new = [], [], []
    for layer in range(L):
        l_arr = jnp.full((1,), layer, jnp.int32)
        h, z, xbc, dt, u, q, k, v = _inproj_call(l_arr, x, mod6, r3(norm1_g), w_proj, b_proj, cos_tab, sin_tab)
        y2, st = _ssd_call(l_arr, ssd_tabs, xbc, dt, ssd_conv_w, r3(ssd_conv_b), dt_bias, a_log, dskip,
                           e_mat, h0t)
        uc = _cv_call(l_arr, u, cv_dw_w, r3(cv_dw_b), r3(cv_ln_g), r3(cv_ln_b))
        oa_p = _attn_call(l_arr, q, k, v, None, None, da_lambda, r3(da_subln_g), latent=False)
        oa_l = _attn_call(l_arr, q, k, v, ck, cv, da_lambda, r3(da_subln_g), latent=True)
        oa = jnp.concatenate([oa_p, oa_l], axis=0)
        x1, h2, route, counts, e1, e2, r1, r2 = _combine_call(
            l_arr, x, h, y2, z, uc, oa, mod6, w_gates, b_gates, r3(ssd_norm_g), w_br_ssd_b, w_br_conv_b,
            w_br_attn_b, w_out_b, r3(norm2_g), w_router, b_router)
        e1, e2, r1, r2 = (a.reshape(T) for a in (e1, e2, r1, r2))
        cnt = counts[0, ROUTER_LANE0:ROUTER_LANE0 + MOE_E].astype(jnp.int32)
        pcnt = (cnt + MOE_BLOCK - 1) // MOE_BLOCK * MOE_BLOCK
        pend = jnp.cumsum(pcnt)
        pstart = pend - pcnt
        blk0 = jnp.arange(N_SLOT_BLOCKS, dtype=jnp.int32) * MOE_BLOCK
        block_expert = jnp.minimum(jnp.sum((pend[None, :] <= blk0[:, None]).astype(jnp.int32), axis=1), MOE_E - 1)
        n_used = pend[-1:] // MOE_BLOCK

        xs = _dispatch_call(e1, e2, r1, r2, pstart, pend, h2)
        ys = _moe_call(l_arr, block_expert, n_used, xs, moe_w_gate, moe_w_up, moe_w_down)
        x = _moe_combine_call(l_arr, e1, e2, r1, r2, pstart, x1, route, mod6, final_g.reshape(1, D), ys,
                              layer == L - 1)

        ks_new.append(k[:T_P].reshape(N_PROMPT_SEQ, PROMPT_LEN, DA_HEADS, 2 * DA_HEAD_DIM))
        vs_new.append(v[:T_P].reshape(N_PROMPT_SEQ, PROMPT_LEN, DA_HEADS, DA_V_DIM))
        s = st[:N_PROMPT_SEQ].reshape(N_PROMPT_SEQ, 2, SSD_STATE, SSD_HEADS, SSD_HEAD_DIM)
        ss_new.append(jnp.transpose(s, (0, 1, 3, 4, 2)))

    y_prompt = x[:T_P].reshape(N_PROMPT_SEQ, PROMPT_LEN, D)
    y_sample = x[T_P:].reshape(N_LATENT_SEQ, LATENT_LEN, D)
    return (y_prompt, y_sample, jnp.stack(ks_new, axis=1), jnp.stack(vs_new, axis=1), jnp.stack(ss_new, axis=1))
```

```python
import functools
import math

import jax
import jax.numpy as jnp
import numpy as np
from jax import lax
from jax.experimental import pallas as pl
from jax.experimental.pallas import tpu as pltpu

D = 1024
DEPTH = 4
N_PROMPT_SEQ, PROMPT_LEN = 16, 256
N_LATENT_SEQ, LATENT_LEN = 8, 2048
PAST_LEN = 512
T_P = N_PROMPT_SEQ * PROMPT_LEN
T_L = N_LATENT_SEQ * LATENT_LEN
T = T_P + T_L
GRID_W = 64
EPS = 1e-6
SSD_HEADS, SSD_HEAD_DIM, SSD_STATE, SSD_GROUPS = 16, 64, 64, 2
SSD_CONV_W = 5
SSD_CHUNK = 128
SSD_XBC = D + 2 * SSD_GROUPS * SSD_STATE
CONV_CH, CONV_W = 512, 31
DA_HEADS, DA_HEAD_DIM, DA_V_DIM = 4, 64, 128
DA_QK = DA_HEADS * 2 * DA_HEAD_DIM
DA_V = DA_HEADS * DA_V_DIM
ROPE_BASE = 10000.0
MOE_GROUPS, MOE_EPG, MOE_E, MOE_HIDDEN = 4, 8, 32, 512
ROUTER_LANE0 = MOE_GROUPS

LANES = 128
SUBLANES = 8
TM = 256
N_TILES = T // TM
N_PROMPT_TILES = T_P // TM
TILES_PER_LATENT_SEQ = LATENT_LEN // TM
MOE_BLOCK = 256
N_SLOT_BLOCKS = (2 * T) // MOE_BLOCK + MOE_E
N_SLOTS = N_SLOT_BLOCKS * MOE_BLOCK
VMEM_LIMIT = 56 * 1024 * 1024

F32 = jnp.float32
BF16 = jnp.bfloat16
HI = lax.Precision.HIGHEST
NEG = -1e30


def _cparams(sem, vmem=VMEM_LIMIT):
    return pltpu.CompilerParams(dimension_semantics=sem, vmem_limit_bytes=vmem)


def _mod_row(i):
    return jnp.where(i < N_PROMPT_TILES, N_LATENT_SEQ, (i - N_PROMPT_TILES) // TILES_PER_LATENT_SEQ)


def _pair_specs(n, tile_arg=0):
    def prompt(*a):
        return (jnp.minimum(a[tile_arg], N_PROMPT_TILES - 1), 0)

    def latent(*a):
        return (jnp.maximum(a[tile_arg] - N_PROMPT_TILES, 0), 0)

    return [pl.BlockSpec((TM, n), prompt), pl.BlockSpec((TM, n), latent)]


def _pair_read(i, p_ref, l_ref):
    return jnp.where(i < N_PROMPT_TILES, p_ref[...], l_ref[...])


def _silu(x):
    return x * (1.0 / (1.0 + jnp.exp(-x)))


def _sigmoid(x):
    return 1.0 / (1.0 + jnp.exp(-x))


def _softplus(x):
    return jnp.maximum(x, 0.0) + jnp.log(1.0 + jnp.exp(-jnp.abs(x)))


def _dot(a, b, **kw):
    return jnp.dot(a, b, preferred_element_type=F32, **kw)


def _split3(x):
    hi = x.astype(BF16)
    r1 = x - hi.astype(F32)
    mid = r1.astype(BF16)
    lo = (r1 - mid.astype(F32)).astype(BF16)
    return jnp.concatenate([hi, mid, lo], axis=1)


def _dot_nt(a, b):
    return lax.dot_general(a, b, (((1,), (1,)), ((), ())), preferred_element_type=F32)


def _ada_kernel(c_ref, w_ref, b_ref, o_ref):
    cs = _silu(c_ref[...])
    o_ref[...] = _dot(cs, w_ref[...], precision=HI) + b_ref[...]


def _ada_call(cvec, w_ada, b_ada):
    nj = 6
    return pl.pallas_call(
        _ada_kernel,
        out_shape=jax.ShapeDtypeStruct((DEPTH, 16, 6 * D), F32),
        grid=(DEPTH, nj),
        in_specs=[
            pl.BlockSpec((16, D), lambda l, j: (0, 0)),
            pl.BlockSpec((None, D, D), lambda l, j: (l, 0, j)),
            pl.BlockSpec((None, 1, D), lambda l, j: (l, 0, j)),
        ],
        out_specs=pl.BlockSpec((None, 16, D), lambda l, j: (l, 0, j)),
        compiler_params=_cparams(("arbitrary", "arbitrary")),
        name="ada_mod",
    )(cvec, w_ada, b_ada.reshape(DEPTH, 1, 6 * D))


_C_Z = (0, D)
_C_XBC = (_C_Z[1], _C_Z[1] + SSD_XBC)
_C_DT = (_C_XBC[1], _C_XBC[1] + LANES)
_C_GLU = (_C_DT[1], _C_DT[1] + 2 * CONV_CH)
_C_Q = (_C_GLU[1], _C_GLU[1] + DA_QK)
_C_K = (_C_Q[1], _C_Q[1] + DA_QK)
_C_V = (_C_K[1], _C_K[1] + DA_V)
N_PROJ = _C_V[1]


def _swap16(x):
    cols = []
    for c in range(x.shape[1] // LANES):
        xc = x[:, c * LANES:(c + 1) * LANES]
        lane = lax.broadcasted_iota(jnp.int32, xc.shape, 1)
        from_right = pltpu.roll(xc, LANES - 16, axis=1)
        from_left = pltpu.roll(xc, 16, axis=1)
        cols.append(jnp.where((lane >> 4) % 2 == 0, from_right, from_left))
    return jnp.concatenate(cols, axis=1)


def _inproj_kernel(l_ref, xp_ref, xl_ref, mod_ref, g_ref, w_ref, b_ref, cos_ref, sin_ref,
                   h_ref, z_ref, xbc_ref, dt_ref, u_ref, q_ref, k_ref, v_ref):
    x = _pair_read(pl.program_id(0), xp_ref, xl_ref)
    xn = x * lax.rsqrt(jnp.mean(x * x, axis=-1, keepdims=True) + EPS)
    h = xn * g_ref[...] * (1.0 + mod_ref[1:2, :]) + mod_ref[0:1, :]
    hb = h.astype(BF16)
    h_ref[...] = hb

    def proj(c):
        return _dot(hb, w_ref[:, c[0]:c[1]]) + b_ref[:, c[0]:c[1]]

    z_ref[...] = proj(_C_Z).astype(BF16)
    xbc_ref[...] = proj(_C_XBC)
    dt_ref[...] = proj(_C_DT)
    glu = proj(_C_GLU)
    u_ref[...] = glu[:, :CONV_CH] * _sigmoid(glu[:, CONV_CH:])
    cos = cos_ref[...]
    sin = sin_ref[...]
    q = proj(_C_Q)
    q = q * cos + _swap16(q) * sin
    q_ref[...] = (q * (DA_HEAD_DIM ** -0.5 * math.log2(math.e))).astype(BF16)
    k = proj(_C_K)
    k_ref[...] = k * cos + _swap16(k) * sin
    v_ref[...] = proj(_C_V)


def _inproj_call(l_arr, x_p, x_l, mod6, norm1_g, w_proj, b_proj, cos_tab, sin_tab):
    tok = lambda n: pl.BlockSpec((TM, n), lambda i, l: (i, 0))
    rope_blk = lambda i, l: (jnp.where(i < N_PROMPT_TILES, TILES_PER_LATENT_SEQ,
                                       (i - N_PROMPT_TILES) % TILES_PER_LATENT_SEQ), 0)
    grid_spec = pltpu.PrefetchScalarGridSpec(
        num_scalar_prefetch=1,
        grid=(N_TILES,),
        in_specs=_pair_specs(D) + [
            pl.BlockSpec((None, None, 6, D), lambda i, l: (l[0], _mod_row(i), 0, 0)),
            pl.BlockSpec((None, 1, D), lambda i, l: (l[0], 0, 0)),
            pl.BlockSpec((None, D, N_PROJ), lambda i, l: (l[0], 0, 0)),
            pl.BlockSpec((None, 1, N_PROJ), lambda i, l: (l[0], 0, 0)),
            pl.BlockSpec((TM, DA_QK), rope_blk),
            pl.BlockSpec((TM, DA_QK), rope_blk),
        ],
        out_specs=[tok(D), tok(D), tok(SSD_XBC), tok(LANES), tok(CONV_CH), tok(DA_QK), tok(DA_QK), tok(DA_V)],
    )
    sds = lambda n, dt: jax.ShapeDtypeStruct((T, n), dt)
    return pl.pallas_call(
        _inproj_kernel,
        grid_spec=grid_spec,
        out_shape=[sds(D, BF16), sds(D, BF16), sds(SSD_XBC, F32), sds(LANES, F32), sds(CONV_CH, F32),
                   sds(DA_QK, BF16), sds(DA_QK, F32), sds(DA_V, F32)],
        compiler_params=_cparams(("arbitrary",)),
        name="inproj",
    )(l_arr, x_p, x_l, mod6, norm1_g, w_proj, b_proj, cos_tab, sin_tab)


N_CHUNKS = T // SSD_CHUNK
N_PROMPT_CHUNKS = T_P // SSD_CHUNK
CH_PER_PROMPT = PROMPT_LEN // SSD_CHUNK
CH_PER_LATENT = LATENT_LEN // SSD_CHUNK
HALO = SUBLANES
STATE_DUMP = N_PROMPT_SEQ


def _ssd_tables(fwd):
    cidx, flags, h0, so = (np.zeros((N_CHUNKS,), np.int32) for _ in range(4))
    for j in range(N_CHUNKS):
        c = j if fwd else N_CHUNKS - 1 - j
        if c < N_PROMPT_CHUNKS:
            seq, pos, n = c // CH_PER_PROMPT, c % CH_PER_PROMPT, CH_PER_PROMPT
            h0i, soi = N_LATENT_SEQ, seq
        else:
            cc = c - N_PROMPT_CHUNKS
            seq, pos, n = cc // CH_PER_LATENT, cc % CH_PER_LATENT, CH_PER_LATENT
            h0i, soi = seq, STATE_DUMP
        first = pos == 0 if fwd else pos == n - 1
        cidx[j] = c
        flags[j] = int(first) | (int(pos > 0) << 1) | (int(pos < n - 1) << 2)
        h0[j] = h0i
        so[j] = soi
    return [jnp.asarray(a) for a in (cidx, flags, h0, so)]


def _ssd_conv(flags, xc_ref, xp_ref, xn_ref, cw_ref, cb_ref, xpad):
    q = SSD_CHUNK
    xpad[0:HALO, :] = jnp.where(((flags >> 1) & 1) == 1, xp_ref[...], 0.0)
    xpad[HALO:HALO + q, :] = xc_ref[...]
    xpad[HALO + q:HALO + q + HALO, :] = jnp.where(((flags >> 2) & 1) == 1, xn_ref[...], 0.0)
    acc = jnp.zeros((q, SSD_XBC), F32) + cb_ref[...]
    pad = SSD_CONV_W // 2
    xp = xpad[...]
    rows = q + 2 * HALO
    for k in range(SSD_CONV_W):
        shifted = xp if k == pad else pltpu.roll(xp, (pad - k) % rows, axis=0)
        acc = acc + cw_ref[k:k + 1, :] * shifted[HALO:HALO + q, :]
    return _silu(acc)


def _ssd_scan_chunk(fwd, flags, xc, dt_ref, dtb_ref, alog_ref, dskip_ref, e_ref, h0_ref, y_ref, so_ref, state, lat_s):
    q = SSD_CHUNK
    ho = 0 if fwd else SSD_HEADS

    @pl.when((flags & 1) == 1)
    def _():
        state[...] = h0_ref[...]

    xs = xc[:, :D]
    bm = xc[:, D:D + LANES]
    cm = xc[:, D + LANES:D + 2 * LANES]
    dt = _softplus(dt_ref[...] + dtb_ref[...])
    da = dt * (-jnp.exp(alog_ref[...]))
    row = lax.broadcasted_iota(jnp.int32, (q, q), 0)
    col = lax.broadcasted_iota(jnp.int32, (q, q), 1)
    tri = (col <= row) if fwd else (col >= row)
    p = _dot(jnp.where(tri, 1.0, 0.0).astype(BF16), _split3(da))
    la = (p[:, :LANES] + p[:, LANES:2 * LANES]) + p[:, 2 * LANES:]
    lat_s[...] = la.T
    e3 = e_ref[...]
    la_exp = _dot(_split3(la), e3)
    dt_exp = _dot(_split3(dt), e3)
    la_end = la_exp[q - 1:q, :] if fwd else la_exp[0:1, :]
    decay_end = jnp.exp(la_end - la_exp)
    chunk_decay = jnp.exp(la_end)
    decay_in = jnp.exp(la_exp)
    xdt = xs * dt_exp
    xdt_b = xdt.astype(BF16)
    xdtw_b = (xdt * decay_end).astype(BF16)
    bmt = bm.T

    half = LANES // 2
    lane = lax.broadcasted_iota(jnp.int32, (q, LANES), 1)
    hpg = SSD_HEADS // SSD_GROUPS
    for g in range(SSD_GROUPS):
        c_g = cm[:, g * half:(g + 1) * half].astype(BF16)
        b_g = bm[:, g * half:(g + 1) * half].astype(BF16)
        bt_g = bmt[g * half:(g + 1) * half, :].astype(BF16)
        cb = _dot_nt(c_g, b_g)
        for pp in range(hpg // 2):
            h_a = g * hpg + 2 * pp
            sl = slice(h_a * SSD_HEAD_DIM, (h_a + 2) * SSD_HEAD_DIM)
            xdt_p = xdt_b[:, sl]
            yd = []
            for h in (ho + h_a, ho + h_a + 1):
                seg = la[:, h:h + 1] - lat_s[h:h + 1, :]
                s_h = (cb * jnp.exp(jnp.where(tri, seg, NEG))).astype(BF16)
                yd.append(_dot(s_h, xdt_p))
            y_diag = jnp.where(lane < half, yd[0], yd[1])
            st_in = state[:, sl]
            y = y_diag + _dot(c_g, st_in.astype(BF16)) * decay_in[:, sl]
            state[:, sl] = st_in * chunk_decay[:, sl] + _dot(bt_g, xdtw_b[:, sl])
            if fwd:
                y = y + xs[:, sl] * dskip_ref[:, sl]
            y_ref[:, sl] = y.astype(BF16)
    so_ref[...] = state[...]


def _ssd_fwd_kernel(l_ref, cidx_ref, flags_ref, h0i_ref, soi_ref,
                    xc_ref, xp_ref, xn_ref, cw_ref, cb_ref, dt_ref, dtb_ref, alog_ref, dskip_ref, e_ref, h0_ref,
                    y_ref, so_ref, xco_ref, state, lat_s, xpad):
    flags = flags_ref[pl.program_id(0)]
    xc = _ssd_conv(flags, xc_ref, xp_ref, xn_ref, cw_ref, cb_ref, xpad)
    xco_ref[...] = xc
    _ssd_scan_chunk(True, flags, xc, dt_ref, dtb_ref, alog_ref, dskip_ref, e_ref, h0_ref, y_ref, so_ref, state, lat_s)


def _ssd_bwd_kernel(l_ref, cidx_ref, flags_ref, h0i_ref, soi_ref,
                    xc_ref, dt_ref, dtb_ref, alog_ref, e_ref, h0_ref, y_ref, so_ref, state, lat_s):
    flags = flags_ref[pl.program_id(0)]
    _ssd_scan_chunk(False, flags, xc_ref[...], dt_ref, dtb_ref, alog_ref, None, e_ref, h0_ref, y_ref, so_ref,
                    state, lat_s)


def _ssd_call(l_arr, tabs, xbc, dt, conv_w, conv_b, dt_bias, a_log, dskip, e_mat, h0t, xc_in, fwd):
    nb8 = T // HALO
    per = SSD_CHUNK // HALO
    cur = lambda j, l, ci, fl, h0, so: (ci[j], 0)
    prev = lambda j, l, ci, fl, h0, so: (jnp.maximum(ci[j] * per - 1, 0), 0)
    nxt = lambda j, l, ci, fl, h0, so: (jnp.minimum((ci[j] + 1) * per, nb8 - 1), 0)
    lay = lambda *shape: pl.BlockSpec((None,) + shape, lambda j, l, *_: (l[0],) + (0,) * len(shape))
    chunk = lambda n: pl.BlockSpec((SSD_CHUNK, n), cur)
    common_in = [chunk(LANES), lay(1, LANES), lay(1, LANES)]
    tail_in = [pl.BlockSpec((3 * LANES, D), lambda j, *_: (0, 0)),
               pl.BlockSpec((None, None, None, SSD_STATE, D),
                            lambda j, l, ci, fl, h0, so: (l[0], h0[j], 0 if fwd else 1, 0, 0))]
    out_specs = [chunk(D), pl.BlockSpec((None, SSD_STATE, D), lambda j, l, ci, fl, h0, so: (so[j], 0, 0))]
    out_shape = [jax.ShapeDtypeStruct((T, D), BF16), jax.ShapeDtypeStruct((N_PROMPT_SEQ + 1, SSD_STATE, D), F32)]
    scratch = [pltpu.VMEM((SSD_STATE, D), F32), pltpu.VMEM((LANES, SSD_CHUNK), F32)]
    if fwd:
        in_specs = ([chunk(SSD_XBC), pl.BlockSpec((HALO, SSD_XBC), prev), pl.BlockSpec((HALO, SSD_XBC), nxt),
                     lay(SSD_CONV_W, SSD_XBC), lay(1, SSD_XBC)] + common_in + [lay(1, D)] + tail_in)
        args = (xbc, xbc, xbc, conv_w, conv_b, dt, dt_bias, a_log, dskip, e_mat, h0t)
        out_specs.append(chunk(SSD_XBC))
        out_shape.append(jax.ShapeDtypeStruct((T, SSD_XBC), F32))
        scratch.append(pltpu.VMEM((SSD_CHUNK + 2 * HALO, SSD_XBC), F32))
    else:
        in_specs = [chunk(SSD_XBC)] + common_in + tail_in
        args = (xc_in, dt, dt_bias, a_log, e_mat, h0t)
    grid_spec = pltpu.PrefetchScalarGridSpec(num_scalar_prefetch=5, grid=(N_CHUNKS,), in_specs=in_specs,
                                             out_specs=out_specs, scratch_shapes=scratch)
    return pl.pallas_call(
        _ssd_fwd_kernel if fwd else _ssd_bwd_kernel,
        grid_spec=grid_spec,
        out_shape=out_shape,
        compiler_params=_cparams(("arbitrary",)),
        name="ssd_scan_fwd" if fwd else "ssd_scan_bwd",
    )(l_arr, *tabs, *args)


CV_HALO = 16


def _cv_kernel(l_ref, uc_ref, up_ref, un_ref, w_ref, b_ref, g_ref, beta_ref, o_ref, upad):
    i = pl.program_id(0)
    pos = (i - N_PROMPT_TILES) % TILES_PER_LATENT_SEQ
    is_prompt = i < N_PROMPT_TILES
    no_l = jnp.logical_or(is_prompt, pos == 0)
    no_r = jnp.logical_or(is_prompt, pos == TILES_PER_LATENT_SEQ - 1)
    upad[0:CV_HALO, :] = jnp.where(no_l, 0.0, up_ref[...])
    upad[CV_HALO:CV_HALO + TM, :] = uc_ref[...]
    upad[CV_HALO + TM:CV_HALO + TM + CV_HALO, :] = jnp.where(no_r, 0.0, un_ref[...])
    acc = jnp.zeros((TM, CONV_CH), F32) + b_ref[...]
    pad = CONV_W // 2
    up = upad[...]
    rows = TM + 2 * CV_HALO
    for rot in range(SUBLANES):
        taps = [k for k in range(CONV_W) if (CV_HALO - pad + k) % SUBLANES == rot]
        rolled = up if rot == 0 else pltpu.roll(up, rows - rot, axis=0)
        for k in taps:
            base = CV_HALO - pad + k - rot
            acc = acc + w_ref[k:k + 1, :] * rolled[base:base + TM, :]
    mu = jnp.mean(acc, axis=-1, keepdims=True)
    xc = acc - mu
    var = jnp.mean(xc * xc, axis=-1, keepdims=True)
    y = xc * lax.rsqrt(var + EPS) * g_ref[...] + beta_ref[...]
    o_ref[...] = _silu(y).astype(BF16)


def _cv_call(l_arr, u, w, b, g, beta):
    per = TM // CV_HALO
    nb = T // CV_HALO
    lay = lambda *shape: pl.BlockSpec((None,) + shape, lambda i, l: (l[0],) + (0,) * len(shape))
    grid_spec = pltpu.PrefetchScalarGridSpec(
        num_scalar_prefetch=1,
        grid=(N_TILES,),
        in_specs=[
            pl.BlockSpec((TM, CONV_CH), lambda i, l: (i, 0)),
            pl.BlockSpec((CV_HALO, CONV_CH), lambda i, l: (jnp.maximum(i * per - 1, 0), 0)),
            pl.BlockSpec((CV_HALO, CONV_CH), lambda i, l: (jnp.minimum((i + 1) * per, nb - 1), 0)),
            lay(CONV_W, CONV_CH), lay(1, CONV_CH), lay(1, CONV_CH), lay(1, CONV_CH),
        ],
        out_specs=pl.BlockSpec((TM, CONV_CH), lambda i, l: (i, 0)),
        scratch_shapes=[pltpu.VMEM((TM + 2 * CV_HALO, CONV_CH), F32)],
    )
    return pl.pallas_call(
        _cv_kernel,
        grid_spec=grid_spec,
        out_shape=jax.ShapeDtypeStruct((T, CONV_CH), BF16),
        compiler_params=_cparams(("arbitrary",)),
        name="conformer_conv",
    )(l_arr, u, u, u, w, b, g, beta)


def _lambda_terms(l_ref, lam_ref):
    lf = jnp.full((1, 1), l_ref[0], jnp.int32).astype(F32)
    lam_init = 0.8 - 0.6 * jnp.exp(-0.3 * lf)
    p = lam_ref[...]
    s1 = jnp.sum(p[0:1, :] * p[1:2, :], axis=-1, keepdims=True)
    s2 = jnp.sum(p[2:3, :] * p[3:4, :], axis=-1, keepdims=True)
    lam = jnp.exp(s1) - jnp.exp(s2) + lam_init
    return lam, 1.0 - lam_init


def _attn_body(l_ref, q_ref, k_ref, v_ref, ck_ref, cv_ref, lam_ref, g_ref, o_ref, k_s, v_s, n_ctx):
    @pl.when(pl.program_id(1) == 0)
    def _():
        k_s[n_ctx:, :] = k_ref[...].astype(BF16)
        for h in range(DA_HEADS):
            sl = slice(h * LANES, (h + 1) * LANES)
            if n_ctx:
                k_s[0:n_ctx, sl] = ck_ref[:, h, :].astype(BF16)
                v_s[0:n_ctx, 2 * h * LANES:(2 * h + 1) * LANES] = cv_ref[:, h, :].astype(BF16)
            v_s[n_ctx:, 2 * h * LANES:(2 * h + 1) * LANES] = v_ref[:, sl].astype(BF16)
            v_s[:, (2 * h + 1) * LANES:(2 * h + 2) * LANES] = jnp.ones((v_s.shape[0], LANES), BF16)

    lam, out_scale = _lambda_terms(l_ref, lam_ref)
    tq = q_ref.shape[0]
    lane = lax.broadcasted_iota(jnp.int32, (tq, LANES), 1)
    zero = jnp.zeros((tq, LANES), BF16)
    for h in range(DA_HEADS):
        sl = slice(h * LANES, (h + 1) * LANES)
        qh = q_ref[:, sl]
        kh = k_s[:, sl]
        vh = v_s[:, 2 * h * LANES:(2 * h + 2) * LANES]
        outs = []
        for c in range(2):
            in_c = (lane < DA_HEAD_DIM) if c == 0 else (lane >= DA_HEAD_DIM)
            s = _dot_nt(jnp.where(in_c, qh, zero), kh)
            m = jnp.max(s, axis=-1, keepdims=True)
            pv = _dot(jnp.exp2(s - m).astype(BF16), vh)
            outs.append(pv[:, :LANES] / pv[:, LANES:])
        o = outs[0] - lam * outs[1]
        o = o * lax.rsqrt(jnp.mean(o * o, axis=-1, keepdims=True) + EPS)
        o_ref[:, sl] = (o * g_ref[...] * out_scale).astype(BF16)


def _attn_prompt_kernel(l_ref, q_ref, k_ref, v_ref, lam_ref, g_ref, o_ref, k_s, v_s):
    _attn_body(l_ref, q_ref, k_ref, v_ref, None, None, lam_ref, g_ref, o_ref, k_s, v_s, 0)


def _attn_latent_kernel(l_ref, q_ref, k_ref, v_ref, ck_ref, cv_ref, lam_ref, g_ref, o_ref, k_s, v_s):
    _attn_body(l_ref, q_ref, k_ref, v_ref, ck_ref, cv_ref, lam_ref, g_ref, o_ref, k_s, v_s, PAST_LEN)


def _attn_call(l_arr, q, k, v, cache_k, cache_v, da_lambda, subln_g, latent):
    if latent:
        nseq, seqlen, n_ctx = N_LATENT_SEQ, LATENT_LEN, PAST_LEN
        tile0, seq0 = N_PROMPT_TILES, T_P // LATENT_LEN
    else:
        nseq, seqlen, n_ctx = N_PROMPT_SEQ, PROMPT_LEN, 0
        tile0, seq0 = 0, 0
    nq = seqlen // TM
    lay = lambda *shape: pl.BlockSpec((None,) + shape, lambda b, i, l: (l[0],) + (0,) * len(shape))
    in_specs = [
        pl.BlockSpec((TM, DA_QK), lambda b, i, l: (tile0 + b * nq + i, 0)),
        pl.BlockSpec((seqlen, DA_QK), lambda b, i, l: (seq0 + b, 0)),
        pl.BlockSpec((seqlen, DA_V), lambda b, i, l: (seq0 + b, 0)),
    ]
    args = [q, k, v]
    if latent:
        ctx = pl.BlockSpec((None, None, PAST_LEN, DA_HEADS, LANES), lambda b, i, l: (b, l[0], 0, 0, 0))
        in_specs += [ctx, ctx]
        args += [cache_k, cache_v]
    in_specs += [lay(4, DA_HEAD_DIM), lay(1, DA_V_DIM)]
    args += [da_lambda, subln_g]
    grid_spec = pltpu.PrefetchScalarGridSpec(
        num_scalar_prefetch=1,
        grid=(nseq, nq),
        in_specs=in_specs,
        out_specs=pl.BlockSpec((TM, DA_V), lambda b, i, l: (b * nq + i, 0)),
        scratch_shapes=[pltpu.VMEM((n_ctx + seqlen, DA_QK), BF16), pltpu.VMEM((n_ctx + seqlen, 2 * DA_V), BF16)],
    )
    return pl.pallas_call(
        _attn_latent_kernel if latent else _attn_prompt_kernel,
        grid_spec=grid_spec,
        out_shape=jax.ShapeDtypeStruct((nseq * seqlen, DA_V), BF16),
        compiler_params=_cparams(("arbitrary", "arbitrary")),
        name="diff_attn_latent" if latent else "diff_attn_prompt",
    )(l_arr, *args)


def _combine_kernel(l_ref, xp_ref, xl_ref, h_ref, yf_ref, yb_ref, z_ref, uc_ref, oap_ref, oal_ref, mod_ref,
                    wg_ref, bg_ref, sg_ref, wa_ref, wb_ref, wc_ref, wo_ref, g2_ref, wr_ref, br_ref,
                    x1_ref, h2_ref, route_ref, cnt_ref, e1_ref, e2_ref, r1_ref, r2_ref, carry):
    i = pl.program_id(0)

    @pl.when(i == 0)
    def _():
        carry[...] = jnp.zeros_like(carry)

    y = (yf_ref[...].astype(F32) + yb_ref[...].astype(F32)) * _silu(z_ref[...].astype(F32))
    y = y * lax.rsqrt(jnp.mean(y * y, axis=-1, keepdims=True) + EPS) * sg_ref[...]
    br_a = _dot(y.astype(BF16), wa_ref[...])
    br_b = _dot(uc_ref[...], wb_ref[...])
    br_c = _dot(_pair_read(i, oap_ref, oal_ref), wc_ref[...])
    hb = h_ref[...]

    def gate(n):
        return _sigmoid(_dot(hb, wg_ref[:, n * D:(n + 1) * D]) + bg_ref[:, n * D:(n + 1) * D])

    mix = gate(0) * br_a + gate(1) * br_b + gate(2) * br_c
    mixed = _dot(mix.astype(BF16), wo_ref[...])
    x1 = _pair_read(i, xp_ref, xl_ref) + mod_ref[2:3, :] * mixed
    x1_ref[...] = x1
    xn = x1 * lax.rsqrt(jnp.mean(x1 * x1, axis=-1, keepdims=True) + EPS)
    h2 = xn * g2_ref[...] * (1.0 + mod_ref[4:5, :]) + mod_ref[3:4, :]
    h2_ref[...] = h2

    h2_hi = h2.astype(BF16)
    h2_mid = (h2 - h2_hi.astype(F32)).astype(BF16)
    logits = _dot(jnp.concatenate([h2_hi, h2_mid, h2_hi], axis=1), wr_ref[...]) + br_ref[...]
    lane = lax.broadcasted_iota(jnp.int32, (TM, LANES), 1).astype(F32)

    def first_argmax(vals, vmax):
        return jnp.min(jnp.where(vals == vmax, lane, float(LANES)), axis=-1, keepdims=True)

    glog = jnp.where(lane < MOE_GROUPS, logits, NEG)
    gmax = jnp.max(glog, axis=-1, keepdims=True)
    gsel = first_argmax(glog, gmax)
    p_g = 1.0 / jnp.sum(jnp.exp(glog - gmax), axis=-1, keepdims=True)
    lo = ROUTER_LANE0 + MOE_EPG * gsel
    elog = jnp.where(jnp.logical_and(lane >= lo, lane < lo + MOE_EPG), logits, NEG)
    v1 = jnp.max(elog, axis=-1, keepdims=True)
    i1 = first_argmax(elog, v1)
    elog2 = jnp.where(lane == i1, NEG, elog)
    v2 = jnp.max(elog2, axis=-1, keepdims=True)
    i2 = first_argmax(elog2, v2)
    t2 = jnp.exp(v2 - v1)
    w1 = p_g / (1.0 + t2)
    w2 = p_g * t2 / (1.0 + t2)

    oh1 = lane == i1
    oh2 = lane == i2
    both = jnp.where(jnp.logical_or(oh1, oh2), 1.0, 0.0)
    r = lax.broadcasted_iota(jnp.int32, (TM, TM), 0)
    c = lax.broadcasted_iota(jnp.int32, (TM, TM), 1)
    strict_lower = jnp.where(c < r, 1.0, 0.0).astype(BF16)
    before = _dot(strict_lower, both.astype(BF16)) + carry[0:1, :]
    r1 = jnp.sum(jnp.where(oh1, before, 0.0), axis=-1, keepdims=True)
    r2 = jnp.sum(jnp.where(oh2, before, 0.0), axis=-1, keepdims=True)
    carry[...] = carry[...] + jnp.sum(both, axis=0, keepdims=True)
    cnt_ref[...] = carry[...]

    e1 = i1 - ROUTER_LANE0
    e2 = i2 - ROUTER_LANE0
    route = jnp.zeros((TM, LANES), F32)
    for n, val in enumerate((e1, e2, w1, w2, r1, r2)):
        route = jnp.where(lane == float(n), val, route)
    route_ref[...] = route
    route_t = route.T
    for n, ref in ((0, e1_ref), (1, e2_ref), (4, r1_ref), (5, r2_ref)):
        ref[...] = route_t[n:n + 1, :].astype(jnp.int32)


def _combine_call(l_arr, x_p, x_l, h, y_f, y_b, z, uc, oa_p, oa_l, mod6, w_gates, b_gates, ssd_norm_g, w_br_ssd,
                  w_br_conv, w_br_attn, w_out, norm2_g, w_router, b_router):
    tok = lambda n: pl.BlockSpec((TM, n), lambda i, l: (i, 0))
    lay = lambda *shape: pl.BlockSpec((None,) + shape, lambda i, l: (l[0],) + (0,) * len(shape))
    grid_spec = pltpu.PrefetchScalarGridSpec(
        num_scalar_prefetch=1,
        grid=(N_TILES,),
        in_specs=_pair_specs(D) + [tok(D), tok(D), tok(D), tok(D), tok(CONV_CH)] + _pair_specs(DA_V) + [
            pl.BlockSpec((None, None, 6, D), lambda i, l: (l[0], _mod_row(i), 0, 0)),
            lay(D, 3 * D), lay(1, 3 * D), lay(1, D), lay(D, D), lay(CONV_CH, D), lay(DA_V, D), lay(D, D),
            lay(1, D), lay(3 * D, LANES), lay(1, LANES),
        ],
        out_specs=[tok(D), tok(D), tok(LANES), pl.BlockSpec((SUBLANES, LANES), lambda i, l: (0, 0))]
        + [pl.BlockSpec((None, 1, TM), lambda i, l: (i, 0, 0))] * 4,
        scratch_shapes=[pltpu.VMEM((SUBLANES, LANES), F32)],
    )
    return pl.pallas_call(
        _combine_kernel,
        grid_spec=grid_spec,
        out_shape=[jax.ShapeDtypeStruct((T, D), F32), jax.ShapeDtypeStruct((T, D), F32),
                   jax.ShapeDtypeStruct((T, LANES), F32), jax.ShapeDtypeStruct((SUBLANES, LANES), F32)]
        + [jax.ShapeDtypeStruct((N_TILES, 1, TM), jnp.int32)] * 4,
        compiler_params=_cparams(("arbitrary",)),
        name="branch_combine_router",
    )(l_arr, x_p, x_l, h, y_f, y_b, z, uc, oa_p, oa_l, mod6, w_gates, b_gates, ssd_norm_g, w_br_ssd, w_br_conv,
      w_br_attn, w_out, norm2_g, w_router, b_router)


ROW_DMA_UNROLL = 8


def _dispatch_kernel(e1_ref, e2_ref, r1_ref, r2_ref, ps_ref, pe_ref, h2_ref, xs_ref, zbuf, sem, zsem):
    i = pl.program_id(0)
    base = i * TM

    @pl.when(i == 0)
    def _():
        zbuf[...] = jnp.zeros_like(zbuf)

        def zero_fill(e):
            off = pl.multiple_of(pe_ref[e] - MOE_BLOCK, MOE_BLOCK)
            return pltpu.make_async_copy(zbuf, xs_ref.at[pl.ds(off, MOE_BLOCK), :], zsem)

        def unused_fill(b):
            off = pl.multiple_of(b * MOE_BLOCK, MOE_BLOCK)
            return pltpu.make_async_copy(zbuf, xs_ref.at[pl.ds(off, MOE_BLOCK), :], zsem)

        n_used = pe_ref[MOE_E - 1] // MOE_BLOCK
        for e in range(MOE_E):
            @pl.when(pe_ref[e] > ps_ref[e])
            def _():
                zero_fill(e).start()
        lax.fori_loop(n_used, N_SLOT_BLOCKS, lambda b, c: (unused_fill(b).start(), c)[1], 0)
        for e in range(MOE_E):
            @pl.when(pe_ref[e] > ps_ref[e])
            def _():
                zero_fill(e).wait()
        lax.fori_loop(n_used, N_SLOT_BLOCKS, lambda b, c: (unused_fill(b).wait(), c)[1], 0)

    def issue(t, carry):
        for e_ref, r_ref in ((e1_ref, r1_ref), (e2_ref, r2_ref)):
            dest = ps_ref[e_ref[base + t]] + r_ref[base + t]
            pltpu.make_async_copy(h2_ref.at[pl.ds(t, 1), :], xs_ref.at[pl.ds(dest, 1), :], sem).start()
        return carry

    lax.fori_loop(0, TM, issue, 0, unroll=ROW_DMA_UNROLL)
    for _ in range(2):
        pltpu.make_async_copy(h2_ref, xs_ref.at[pl.ds(0, TM), :], sem).wait()


def _dispatch_call(e1, e2, r1, r2, pstart, pend, h2):
    grid_spec = pltpu.PrefetchScalarGridSpec(
        num_scalar_prefetch=6,
        grid=(N_TILES,),
        in_specs=[pl.BlockSpec((TM, D), lambda i, *_: (i, 0))],
        out_specs=pl.BlockSpec(memory_space=pl.ANY),
        scratch_shapes=[pltpu.VMEM((MOE_BLOCK, D), F32), pltpu.SemaphoreType.DMA(()), pltpu.SemaphoreType.DMA(())],
    )
    return pl.pallas_call(
        _dispatch_kernel,
        grid_spec=grid_spec,
        out_shape=jax.ShapeDtypeStruct((N_SLOTS, D), F32),
        compiler_params=_cparams(("arbitrary",)),
        name="moe_dispatch",
    )(e1, e2, r1, r2, pstart, pend, h2)


def _moe_kernel(l_ref, be_ref, nb_ref, xs_ref, wg_ref, wu_ref, wd_ref, ys_ref, wg_s, wu_s, wd_s):
    i = pl.program_id(0)
    prev = be_ref[jnp.maximum(i - 1, 0)]

    @pl.when(jnp.logical_or(i == 0, be_ref[i] != prev))
    def _():
        wg_s[...] = wg_ref[...].astype(BF16)
        wu_s[...] = wu_ref[...].astype(BF16)
        wd_s[...] = wd_ref[...].astype(BF16)

    @pl.when(i < nb_ref[0])
    def _():
        xb = xs_ref[...].astype(BF16)
        hid = _silu(_dot(xb, wg_s[...])) * _dot(xb, wu_s[...])
        ys_ref[...] = _dot(hid.astype(BF16), wd_s[...])

    @pl.when(i >= nb_ref[0])
    def _():
        ys_ref[...] = jnp.zeros_like(ys_ref)


def _moe_call(l_arr, block_expert, n_used, xs, w_gate, w_up, w_down):
    wspec = lambda a, b: pl.BlockSpec((None, None, a, b), lambda i, l, be, nb: (l[0], be[i], 0, 0))
    grid_spec = pltpu.PrefetchScalarGridSpec(
        num_scalar_prefetch=3,
        grid=(N_SLOT_BLOCKS,),
        in_specs=[pl.BlockSpec((MOE_BLOCK, D), lambda i, l, be, nb: (jnp.minimum(i, nb[0] - 1), 0)),
                  wspec(D, MOE_HIDDEN), wspec(D, MOE_HIDDEN), wspec(MOE_HIDDEN, D)],
        out_specs=pl.BlockSpec((MOE_BLOCK, D), lambda i, l, be, nb: (i, 0)),
        scratch_shapes=[pltpu.VMEM((D, MOE_HIDDEN), BF16), pltpu.VMEM((D, MOE_HIDDEN), BF16),
                        pltpu.VMEM((MOE_HIDDEN, D), BF16)],
    )
    return pl.pallas_call(
        _moe_kernel,
        grid_spec=grid_spec,
        out_shape=jax.ShapeDtypeStruct((N_SLOTS, D), F32),
        compiler_params=_cparams(("arbitrary",)),
        name="moe_experts",
    )(l_arr, block_expert, n_used, xs, w_gate, w_up, w_down)


def _moe_combine_kernel(e1_ref, e2_ref, r1_ref, r2_ref, ps_ref, x1_ref, route_ref, mod_ref, fg_ref, ys_ref,
                        op_ref, ol_ref, buf, sem, *, final):
    i = pl.program_id(0)
    base = i * TM

    def issue(t, carry):
        for which, (e_ref, r_ref) in enumerate(((e1_ref, r1_ref), (e2_ref, r2_ref))):
            src = ps_ref[e_ref[base + t]] + r_ref[base + t]
            pltpu.make_async_copy(ys_ref.at[pl.ds(src, 1), :], buf.at[which, pl.ds(t, 1), :], sem).start()
        return carry

    lax.fori_loop(0, TM, issue, 0, unroll=ROW_DMA_UNROLL)
    for which in range(2):
        pltpu.make_async_copy(ys_ref.at[pl.ds(0, TM), :], buf.at[which], sem).wait()
    w1 = route_ref[:, 2:3]
    w2 = route_ref[:, 3:4]
    y = buf[0] * w1 + buf[1] * w2
    x2 = x1_ref[...] + mod_ref[5:6, :] * y
    if final:
        x2 = x2 * lax.rsqrt(jnp.mean(x2 * x2, axis=-1, keepdims=True) + EPS) * fg_ref[...]

    @pl.when(i < N_PROMPT_TILES)
    def _():
        op_ref[...] = x2

    @pl.when(i >= N_PROMPT_TILES)
    def _():
        ol_ref[...] = x2


def _moe_combine_call(l_arr, e1, e2, r1, r2, pstart, x1, route, mod6, final_g, ys, final):
    grid_spec = pltpu.PrefetchScalarGridSpec(
        num_scalar_prefetch=6,
        grid=(N_TILES,),
        in_specs=[
            pl.BlockSpec((TM, D), lambda i, *_: (i, 0)),
            pl.BlockSpec((TM, LANES), lambda i, *_: (i, 0)),
            pl.BlockSpec((None, None, 6, D), lambda i, l, *_: (l[0], _mod_row(i), 0, 0)),
            pl.BlockSpec((1, D), lambda i, *_: (0, 0)),
            pl.BlockSpec(memory_space=pl.ANY),
        ],
        out_specs=_pair_specs(D),
        scratch_shapes=[pltpu.VMEM((2, TM, D), F32), pltpu.SemaphoreType.DMA(())],
    )

    def body(l_ref, *rest):
        del l_ref
        _moe_combine_kernel(*rest, final=final)

    return pl.pallas_call(
        body,
        grid_spec=grid_spec,
        out_shape=[jax.ShapeDtypeStruct((T_P, D), F32), jax.ShapeDtypeStruct((T_L, D), F32)],
        compiler_params=_cparams(("arbitrary",)),
        name="moe_combine_final" if final else "moe_combine",
    )(l_arr, e1, e2, r1, r2, pstart, x1, route, mod6, final_g, ys)


def _rope_tables():
    n = LATENT_LEN
    rows = n // GRID_W
    row = jnp.repeat(jnp.arange(rows), GRID_W).astype(F32)
    col = jnp.tile(jnp.arange(GRID_W), rows).astype(F32)
    axis_dim = DA_HEAD_DIM // 2
    inv_freq = 1.0 / (ROPE_BASE ** (jnp.arange(0, axis_dim, 2, dtype=F32) / axis_dim))
    ar, ac = row[:, None] * inv_freq, col[:, None] * inv_freq
    cos64 = jnp.concatenate([jnp.cos(ar), jnp.cos(ar), jnp.cos(ac), jnp.cos(ac)], axis=1)
    sin64 = jnp.concatenate([-jnp.sin(ar), jnp.sin(ar), -jnp.sin(ac), jnp.sin(ac)], axis=1)
    reps = DA_QK // DA_HEAD_DIM
    cos = jnp.concatenate([jnp.tile(cos64, (1, reps)), jnp.ones((TM, DA_QK), F32)], axis=0)
    sin = jnp.concatenate([jnp.tile(sin64, (1, reps)), jnp.zeros((TM, DA_QK), F32)], axis=0)
    return cos, sin


def _pad_lanes(a, n=LANES):
    return jnp.pad(a, [(0, 0)] * (a.ndim - 1) + [(0, n - a.shape[-1])])


def kernel(x_prompt, x_sample, cache_k, cache_v, state_ssd, c, c_ctx, w_ada, b_ada, norm1_g, norm2_g, w_in, b_in,
           ssd_conv_w, ssd_conv_b, ssd_dt_bias, ssd_a_log, ssd_d, ssd_norm_g, w_br_ssd, cv_dw_w, cv_dw_b, cv_ln_g,
           cv_ln_b, w_br_conv, da_lambda, da_subln_g, w_br_attn, w_out, moe_w_group, moe_b_group, moe_w_expert,
           moe_b_expert, moe_w_gate, moe_w_up, moe_w_down, final_g):
    L = DEPTH
    x_p, x_l = x_prompt.reshape(T_P, D), x_sample.reshape(T_L, D)

    cvec = jnp.concatenate([c, c_ctx[None, :], jnp.zeros((16 - N_LATENT_SEQ - 1, D), F32)], axis=0)
    mod6 = _ada_call(cvec, w_ada, b_ada).reshape(L, 16, 6, D)

    o_z, o_xbc, o_dt = 0, D, D + SSD_XBC
    o_glu = o_dt + 2 * SSD_HEADS
    o_q = o_glu + 2 * CONV_CH
    o_k, o_v, o_g = o_q + DA_QK, o_q + 2 * DA_QK, o_q + 2 * DA_QK + DA_V

    def regroup(w):
        return jnp.concatenate([w[..., o_z:o_dt], _pad_lanes(w[..., o_dt:o_glu]), w[..., o_glu:o_g]], axis=-1)

    w_proj = regroup(w_in).astype(BF16)
    b_proj = regroup(b_in).reshape(L, 1, N_PROJ)
    w_gates = w_in[..., o_g:].astype(BF16)
    b_gates = b_in[..., o_g:].reshape(L, 1, 3 * D)
    cos_tab, sin_tab = _rope_tables()

    tabs_f, tabs_b = _ssd_tables(True), _ssd_tables(False)
    dt_bias = _pad_lanes(ssd_dt_bias.reshape(L, 1, 2 * SSD_HEADS))
    a_log = _pad_lanes(ssd_a_log.reshape(L, 1, 2 * SSD_HEADS))
    dskip = jnp.repeat(ssd_d, SSD_HEAD_DIM, axis=-1).reshape(L, 1, D)
    hp = np.arange(D) // SSD_HEAD_DIM
    e_f, e_b = (jnp.asarray(np.tile(np.arange(LANES)[:, None] == ho + hp[None, :], (3, 1)).astype(np.float32))
                .astype(BF16) for ho in (0, SSD_HEADS))
    h0t = jnp.transpose(state_ssd, (1, 0, 2, 5, 3, 4)).reshape(L, N_LATENT_SEQ, 2, SSD_STATE, D)
    h0t = jnp.concatenate([h0t, jnp.zeros((L, 1, 2, SSD_STATE, D), F32)], axis=1)

    w_router = _pad_lanes(jnp.concatenate([moe_w_group, moe_w_expert], axis=-1))
    wr_hi = w_router.astype(BF16)
    wr_mid = (w_router - wr_hi.astype(F32)).astype(BF16)
    w_router = jnp.concatenate([wr_hi, wr_hi, wr_mid], axis=1)
    b_router = _pad_lanes(jnp.concatenate([moe_b_group, moe_b_expert], axis=-1)).reshape(L, 1, LANES)
    w_br_ssd_b, w_br_conv_b = w_br_ssd.astype(BF16), w_br_conv.astype(BF16)
    w_br_attn_b, w_out_b = w_br_attn.astype(BF16), w_out.astype(BF16)
    r3 = lambda a: a.reshape(L, 1, a.shape[-1])

    ks_new, vs_new, ss_new = [], [], []
    for layer in range(L):
        l_arr = jnp.full((1,), layer, jnp.int32)
        h, z, xbc, dt, u, q, k, v = _inproj_call(l_arr, x_p, x_l, mod6, r3(norm1_g), w_proj, b_proj,
                                                 cos_tab, sin_tab)
        y_f, st_f, xc = _ssd_call(l_arr, tabs_f, xbc, dt, ssd_conv_w, r3(ssd_conv_b), dt_bias, a_log, dskip,
                                  e_f, h0t, None, True)
        y_b, st_b = _ssd_call(l_arr, tabs_b, None, dt, None, None, dt_bias, a_log, None, e_b, h0t, xc, False)
        uc = _cv_call(l_arr, u, cv_dw_w, r3(cv_dw_b), r3(cv_ln_g), r3(cv_ln_b))
        oa_p = _attn_call(l_arr, q, k, v, None, None, da_lambda, r3(da_subln_g), latent=False)
        oa_l = _attn_call(l_arr, q, k, v, cache_k, cache_v, da_lambda, r3(da_subln_g), latent=True)
        x1, h2, route, counts, e1, e2, r1, r2 = _combine_call(
            l_arr, x_p, x_l, h, y_f, y_b, z, uc, oa_p, oa_l, mod6, w_gates, b_gates, r3(ssd_norm_g), w_br_ssd_b,
            w_br_conv_b, w_br_attn_b, w_out_b, r3(norm2_g), w_router, b_router)
        e1, e2, r1, r2 = (a.reshape(T) for a in (e1, e2, r1, r2))
        cnt = counts[0, ROUTER_LANE0:ROUTER_LANE0 + MOE_E].astype(jnp.int32)
        pcnt = (cnt + MOE_BLOCK - 1) // MOE_BLOCK * MOE_BLOCK
        pend = jnp.cumsum(pcnt)
        pstart = pend - pcnt
        blk0 = jnp.arange(N_SLOT_BLOCKS, dtype=jnp.int32) * MOE_BLOCK
        block_expert = jnp.minimum(jnp.sum((pend[None, :] <= blk0[:, None]).astype(jnp.int32), axis=1), MOE_E - 1)
        n_used = pend[-1:] // MOE_BLOCK

        xs = _dispatch_call(e1, e2, r1, r2, pstart, pend, h2)
        ys = _moe_call(l_arr, block_expert, n_used, xs, moe_w_gate, moe_w_up, moe_w_down)
        x_p, x_l = _moe_combine_call(l_arr, e1, e2, r1, r2, pstart, x1, route, mod6, final_g.reshape(1, D), ys,
                                     layer == L - 1)

        ks_new.append(k[:T_P].reshape(N_PROMPT_SEQ, PROMPT_LEN, DA_HEADS, 2 * DA_HEAD_DIM))
        vs_new.append(v[:T_P].reshape(N_PROMPT_SEQ, PROMPT_LEN, DA_HEADS, DA_V_DIM))
        s = jnp.stack([st_f[:N_PROMPT_SEQ], st_b[:N_PROMPT_SEQ]], axis=1)
        s = s.reshape(N_PROMPT_SEQ, 2, SSD_STATE, SSD_HEADS, SSD_HEAD_DIM)
        ss_new.append(jnp.transpose(s, (0, 1, 3, 4, 2)))

    y_prompt = x_p.reshape(N_PROMPT_SEQ, PROMPT_LEN, D)
    y_sample = x_l.reshape(N_LATENT_SEQ, LATENT_LEN, D)
    return (y_prompt, y_sample, jnp.stack(ks_new, axis=1), jnp.stack(vs_new, axis=1), jnp.stack(ss_new, axis=1))
```

```python
import functools
import math

import jax
import jax.numpy as jnp
import numpy as np
from jax import lax
from jax.experimental import pallas as pl
from jax.experimental.pallas import tpu as pltpu

D = 1024
DEPTH = 4
N_PROMPT_SEQ, PROMPT_LEN = 16, 256
N_LATENT_SEQ, LATENT_LEN = 8, 2048
PAST_LEN = 512
T_P = N_PROMPT_SEQ * PROMPT_LEN
T_L = N_LATENT_SEQ * LATENT_LEN
T = T_P + T_L
GRID_W = 64
EPS = 1e-6
SSD_HEADS, SSD_HEAD_DIM, SSD_STATE, SSD_GROUPS = 16, 64, 64, 2
SSD_CONV_W = 5
SSD_CHUNK = 128
SSD_XBC = D + 2 * SSD_GROUPS * SSD_STATE
CONV_CH, CONV_W = 512, 31
DA_HEADS, DA_HEAD_DIM, DA_V_DIM = 4, 64, 128
DA_QK = DA_HEADS * 2 * DA_HEAD_DIM
DA_V = DA_HEADS * DA_V_DIM
ROPE_BASE = 10000.0
MOE_GROUPS, MOE_EPG, MOE_E, MOE_HIDDEN = 4, 8, 32, 512
ROUTER_LANE0 = MOE_GROUPS

LANES = 128
SUBLANES = 8
TM = 256
N_TILES = T // TM
N_PROMPT_TILES = T_P // TM
TILES_PER_LATENT_SEQ = LATENT_LEN // TM
MOE_BLOCK = 256
N_SLOT_BLOCKS = (2 * T) // MOE_BLOCK + MOE_E
N_SLOTS = N_SLOT_BLOCKS * MOE_BLOCK
VMEM_LIMIT = 56 * 1024 * 1024

F32 = jnp.float32
BF16 = jnp.bfloat16
HI = lax.Precision.HIGHEST
NEG = -1e30


def _cparams(sem, vmem=VMEM_LIMIT):
    return pltpu.CompilerParams(dimension_semantics=sem, vmem_limit_bytes=vmem)


def _mod_row(i):
    return jnp.where(i < N_PROMPT_TILES, N_LATENT_SEQ, (i - N_PROMPT_TILES) // TILES_PER_LATENT_SEQ)


def _pair_specs(n, tile_arg=0):
    def prompt(*a):
        return (jnp.minimum(a[tile_arg], N_PROMPT_TILES - 1), 0)

    def latent(*a):
        return (jnp.maximum(a[tile_arg] - N_PROMPT_TILES, 0), 0)

    return [pl.BlockSpec((TM, n), prompt), pl.BlockSpec((TM, n), latent)]


def _pair_read(i, p_ref, l_ref):
    return jnp.where(i < N_PROMPT_TILES, p_ref[...], l_ref[...])


def _silu(x):
    return x * (1.0 / (1.0 + jnp.exp(-x)))


def _sigmoid(x):
    return 1.0 / (1.0 + jnp.exp(-x))


def _softplus(x):
    return jnp.maximum(x, 0.0) + jnp.log(1.0 + jnp.exp(-jnp.abs(x)))


def _dot(a, b, **kw):
    return jnp.dot(a, b, preferred_element_type=F32, **kw)


def _split3(x):
    hi = x.astype(BF16)
    r1 = x - hi.astype(F32)
    mid = r1.astype(BF16)
    lo = (r1 - mid.astype(F32)).astype(BF16)
    return jnp.concatenate([hi, mid, lo], axis=1)


def _dot_nt(a, b):
    return lax.dot_general(a, b, (((1,), (1,)), ((), ())), preferred_element_type=F32)


def _ada_kernel(c_ref, w_ref, b_ref, o_ref):
    cs = _silu(c_ref[...])
    o_ref[...] = _dot(cs, w_ref[...], precision=HI) + b_ref[...]


def _ada_call(cvec, w_ada, b_ada):
    nj = 6
    return pl.pallas_call(
        _ada_kernel,
        out_shape=jax.ShapeDtypeStruct((DEPTH, 16, 6 * D), F32),
        grid=(DEPTH, nj),
        in_specs=[
            pl.BlockSpec((16, D), lambda l, j: (0, 0)),
            pl.BlockSpec((None, D, D), lambda l, j: (l, 0, j)),
            pl.BlockSpec((None, 1, D), lambda l, j: (l, 0, j)),
        ],
        out_specs=pl.BlockSpec((None, 16, D), lambda l, j: (l, 0, j)),
        compiler_params=_cparams(("arbitrary", "arbitrary")),
        name="ada_mod",
    )(cvec, w_ada, b_ada.reshape(DEPTH, 1, 6 * D))


_C_Z = (0, D)
_C_XBC = (_C_Z[1], _C_Z[1] + SSD_XBC)
_C_DT = (_C_XBC[1], _C_XBC[1] + LANES)
_C_GLU = (_C_DT[1], _C_DT[1] + 2 * CONV_CH)
_C_Q = (_C_GLU[1], _C_GLU[1] + DA_QK)
_C_K = (_C_Q[1], _C_Q[1] + DA_QK)
_C_V = (_C_K[1], _C_K[1] + DA_V)
N_PROJ = _C_V[1]


def _swap16(x):
    cols = []
    for c in range(x.shape[1] // LANES):
        xc = x[:, c * LANES:(c + 1) * LANES]
        lane = lax.broadcasted_iota(jnp.int32, xc.shape, 1)
        from_right = pltpu.roll(xc, LANES - 16, axis=1)
        from_left = pltpu.roll(xc, 16, axis=1)
        cols.append(jnp.where((lane >> 4) % 2 == 0, from_right, from_left))
    return jnp.concatenate(cols, axis=1)


def _inproj_kernel(l_ref, xp_ref, xl_ref, mod_ref, g_ref, w_ref, b_ref, cos_ref, sin_ref,
                   h_ref, z_ref, xbc_ref, dt_ref, u_ref, q_ref, k_ref, v_ref):
    x = _pair_read(pl.program_id(0), xp_ref, xl_ref)
    xn = x * lax.rsqrt(jnp.mean(x * x, axis=-1, keepdims=True) + EPS)
    h = xn * g_ref[...] * (1.0 + mod_ref[1:2, :]) + mod_ref[0:1, :]
    hb = h.astype(BF16)
    h_ref[...] = hb

    def proj(c):
        return _dot(hb, w_ref[:, c[0]:c[1]]) + b_ref[:, c[0]:c[1]]

    z_ref[...] = proj(_C_Z).astype(BF16)
    xbc_ref[...] = proj(_C_XBC)
    dt_ref[...] = proj(_C_DT)
    glu = proj(_C_GLU)
    u_ref[...] = glu[:, :CONV_CH] * _sigmoid(glu[:, CONV_CH:])
    cos = cos_ref[...]
    sin = sin_ref[...]
    q = proj(_C_Q)
    q = q * cos + _swap16(q) * sin
    q_ref[...] = (q * (DA_HEAD_DIM ** -0.5 * math.log2(math.e))).astype(BF16)
    k = proj(_C_K)
    k_ref[...] = k * cos + _swap16(k) * sin
    v_ref[...] = proj(_C_V)


def _inproj_call(l_arr, x_p, x_l, mod6, norm1_g, w_proj, b_proj, cos_tab, sin_tab):
    tok = lambda n: pl.BlockSpec((TM, n), lambda i, l: (i, 0))
    rope_blk = lambda i, l: (jnp.where(i < N_PROMPT_TILES, TILES_PER_LATENT_SEQ,
                                       (i - N_PROMPT_TILES) % TILES_PER_LATENT_SEQ), 0)
    grid_spec = pltpu.PrefetchScalarGridSpec(
        num_scalar_prefetch=1,
        grid=(N_TILES,),
        in_specs=_pair_specs(D) + [
            pl.BlockSpec((None, None, 6, D), lambda i, l: (l[0], _mod_row(i), 0, 0)),
            pl.BlockSpec((None, 1, D), lambda i, l: (l[0], 0, 0)),
            pl.BlockSpec((None, D, N_PROJ), lambda i, l: (l[0], 0, 0)),
            pl.BlockSpec((None, 1, N_PROJ), lambda i, l: (l[0], 0, 0)),
            pl.BlockSpec((TM, DA_QK), rope_blk),
            pl.BlockSpec((TM, DA_QK), rope_blk),
        ],
        out_specs=[tok(D), tok(D), tok(SSD_XBC), tok(LANES), tok(CONV_CH), tok(DA_QK), tok(DA_QK), tok(DA_V)],
    )
    sds = lambda n, dt: jax.ShapeDtypeStruct((T, n), dt)
    return pl.pallas_call(
        _inproj_kernel,
        grid_spec=grid_spec,
        out_shape=[sds(D, BF16), sds(D, BF16), sds(SSD_XBC, F32), sds(LANES, F32), sds(CONV_CH, F32),
                   sds(DA_QK, BF16), sds(DA_QK, F32), sds(DA_V, F32)],
        compiler_params=_cparams(("arbitrary",)),
        name="inproj",
    )(l_arr, x_p, x_l, mod6, norm1_g, w_proj, b_proj, cos_tab, sin_tab)


N_CHUNKS = T // SSD_CHUNK
N_PROMPT_CHUNKS = T_P // SSD_CHUNK
CH_PER_PROMPT = PROMPT_LEN // SSD_CHUNK
CH_PER_LATENT = LATENT_LEN // SSD_CHUNK
HALO = SUBLANES
STATE_DUMP = N_PROMPT_SEQ


def _ssd_tables(fwd):
    cidx, flags, h0, so = (np.zeros((N_CHUNKS,), np.int32) for _ in range(4))
    for j in range(N_CHUNKS):
        c = j if fwd else N_CHUNKS - 1 - j
        if c < N_PROMPT_CHUNKS:
            seq, pos, n = c // CH_PER_PROMPT, c % CH_PER_PROMPT, CH_PER_PROMPT
            h0i, soi = N_LATENT_SEQ, seq
        else:
            cc = c - N_PROMPT_CHUNKS
            seq, pos, n = cc // CH_PER_LATENT, cc % CH_PER_LATENT, CH_PER_LATENT
            h0i, soi = seq, STATE_DUMP
        first = pos == 0 if fwd else pos == n - 1
        cidx[j] = c
        flags[j] = int(first) | (int(pos > 0) << 1) | (int(pos < n - 1) << 2)
        h0[j] = h0i
        so[j] = soi
    return [jnp.asarray(a) for a in (cidx, flags, h0, so)]


def _ssd_conv(flags, xc_ref, xp_ref, xn_ref, cw_ref, cb_ref, xpad):
    q = SSD_CHUNK
    xpad[0:HALO, :] = jnp.where(((flags >> 1) & 1) == 1, xp_ref[...], 0.0)
    xpad[HALO:HALO + q, :] = xc_ref[...]
    xpad[HALO + q:HALO + q + HALO, :] = jnp.where(((flags >> 2) & 1) == 1, xn_ref[...], 0.0)
    acc = jnp.zeros((q, SSD_XBC), F32) + cb_ref[...]
    pad = SSD_CONV_W // 2
    xp = xpad[...]
    rows = q + 2 * HALO
    for k in range(SSD_CONV_W):
        shifted = xp if k == pad else pltpu.roll(xp, (pad - k) % rows, axis=0)
        acc = acc + cw_ref[k:k + 1, :] * shifted[HALO:HALO + q, :]
    return _silu(acc)


def _ssd_scan_chunk(fwd, flags, xc, dt_ref, dtb_ref, alog_ref, dskip_ref, e_ref, h0_ref, y_ref, so_ref, state, lat_s):
    q = SSD_CHUNK
    ho = 0 if fwd else SSD_HEADS

    @pl.when((flags & 1) == 1)
    def _():
        state[...] = h0_ref[...]

    xs = xc[:, :D]
    bm = xc[:, D:D + LANES]
    cm = xc[:, D + LANES:D + 2 * LANES]
    dt = _softplus(dt_ref[...] + dtb_ref[...])
    da = dt * (-jnp.exp(alog_ref[...]))
    row = lax.broadcasted_iota(jnp.int32, (q, q), 0)
    col = lax.broadcasted_iota(jnp.int32, (q, q), 1)
    tri = (col <= row) if fwd else (col >= row)
    p = _dot(jnp.where(tri, 1.0, 0.0).astype(BF16), _split3(da))
    la = (p[:, :LANES] + p[:, LANES:2 * LANES]) + p[:, 2 * LANES:]
    lat_s[...] = la.T
    e3 = e_ref[...]
    la_exp = _dot(_split3(la), e3)
    dt_exp = _dot(_split3(dt), e3)
    la_end = la_exp[q - 1:q, :] if fwd else la_exp[0:1, :]
    decay_end = jnp.exp(la_end - la_exp)
    chunk_decay = jnp.exp(la_end)
    decay_in = jnp.exp(la_exp)
    xdt = xs * dt_exp
    xdt_b = xdt.astype(BF16)
    xdtw_b = (xdt * decay_end).astype(BF16)
    bmt = bm.T

    half = LANES // 2
    lane = lax.broadcasted_iota(jnp.int32, (q, LANES), 1)
    hpg = SSD_HEADS // SSD_GROUPS
    for g in range(SSD_GROUPS):
        c_g = cm[:, g * half:(g + 1) * half].astype(BF16)
        b_g = bm[:, g * half:(g + 1) * half].astype(BF16)
        bt_g = bmt[g * half:(g + 1) * half, :].astype(BF16)
        cb = _dot_nt(c_g, b_g)
        for pp in range(hpg // 2):
            h_a = g * hpg + 2 * pp
            sl = slice(h_a * SSD_HEAD_DIM, (h_a + 2) * SSD_HEAD_DIM)
            xdt_p = xdt_b[:, sl]
            yd = []
            for h in (ho + h_a, ho + h_a + 1):
                seg = la[:, h:h + 1] - lat_s[h:h + 1, :]
                s_h = (cb * jnp.exp(jnp.where(tri, seg, NEG))).astype(BF16)
                yd.append(_dot(s_h, xdt_p))
            y_diag = jnp.where(lane < half, yd[0], yd[1])
            st_in = state[:, sl]
            y = y_diag + _dot(c_g, st_in.astype(BF16)) * decay_in[:, sl]
            state[:, sl] = st_in * chunk_decay[:, sl] + _dot(bt_g, xdtw_b[:, sl])
            if fwd:
                y = y + xs[:, sl] * dskip_ref[:, sl]
            y_ref[:, sl] = y.astype(BF16)
    so_ref[...] = state[...]


def _ssd_fwd_kernel(l_ref, cidx_ref, flags_ref, h0i_ref, soi_ref,
                    xc_ref, xp_ref, xn_ref, cw_ref, cb_ref, dt_ref, dtb_ref, alog_ref, dskip_ref, e_ref, h0_ref,
                    y_ref, so_ref, xco_ref, state, lat_s, xpad):
    flags = flags_ref[pl.program_id(0)]
    xc = _ssd_conv(flags, xc_ref, xp_ref, xn_ref, cw_ref, cb_ref, xpad)
    xco_ref[...] = xc
    _ssd_scan_chunk(True, flags, xc, dt_ref, dtb_ref, alog_ref, dskip_ref, e_ref, h0_ref, y_ref, so_ref, state, lat_s)


def _ssd_bwd_kernel(l_ref, cidx_ref, flags_ref, h0i_ref, soi_ref,
                    xc_ref, dt_ref, dtb_ref, alog_ref, e_ref, h0_ref, y_ref, so_ref, state, lat_s):
    flags = flags_ref[pl.program_id(0)]
    _ssd_scan_chunk(False, flags, xc_ref[...], dt_ref, dtb_ref, alog_ref, None, e_ref, h0_ref, y_ref, so_ref,
                    state, lat_s)


def _ssd_call(l_arr, tabs, xbc, dt, conv_w, conv_b, dt_bias, a_log, dskip, e_mat, h0t, xc_in, fwd):
    nb8 = T // HALO
    per = SSD_CHUNK // HALO
    cur = lambda j, l, ci, fl, h0, so: (ci[j], 0)
    prev = lambda j, l, ci, fl, h0, so: (jnp.maximum(ci[j] * per - 1, 0), 0)
    nxt = lambda j, l, ci, fl, h0, so: (jnp.minimum((ci[j] + 1) * per, nb8 - 1), 0)
    lay = lambda *shape: pl.BlockSpec((None,) + shape, lambda j, l, *_: (l[0],) + (0,) * len(shape))
    chunk = lambda n: pl.BlockSpec((SSD_CHUNK, n), cur)
    common_in = [chunk(LANES), lay(1, LANES), lay(1, LANES)]
    tail_in = [pl.BlockSpec((3 * LANES, D), lambda j, *_: (0, 0)),
               pl.BlockSpec((None, None, None, SSD_STATE, D),
                            lambda j, l, ci, fl, h0, so: (l[0], h0[j], 0 if fwd else 1, 0, 0))]
    out_specs = [chunk(D), pl.BlockSpec((None, SSD_STATE, D), lambda j, l, ci, fl, h0, so: (so[j], 0, 0))]
    out_shape = [jax.ShapeDtypeStruct((T, D), BF16), jax.ShapeDtypeStruct((N_PROMPT_SEQ + 1, SSD_STATE, D), F32)]
    scratch = [pltpu.VMEM((SSD_STATE, D), F32), pltpu.VMEM((LANES, SSD_CHUNK), F32)]
    if fwd:
        in_specs = ([chunk(SSD_XBC), pl.BlockSpec((HALO, SSD_XBC), prev), pl.BlockSpec((HALO, SSD_XBC), nxt),
                     lay(SSD_CONV_W, SSD_XBC), lay(1, SSD_XBC)] + common_in + [lay(1, D)] + tail_in)
        args = (xbc, xbc, xbc, conv_w, conv_b, dt, dt_bias, a_log, dskip, e_mat, h0t)
        out_specs.append(chunk(SSD_XBC))
        out_shape.append(jax.ShapeDtypeStruct((T, SSD_XBC), F32))
        scratch.append(pltpu.VMEM((SSD_CHUNK + 2 * HALO, SSD_XBC), F32))
    else:
        in_specs = [chunk(SSD_XBC)] + common_in + tail_in
        args = (xc_in, dt, dt_bias, a_log, e_mat, h0t)
    grid_spec = pltpu.PrefetchScalarGridSpec(num_scalar_prefetch=5, grid=(N_CHUNKS,), in_specs=in_specs,
                                             out_specs=out_specs, scratch_shapes=scratch)
    return pl.pallas_call(
        _ssd_fwd_kernel if fwd else _ssd_bwd_kernel,
        grid_spec=grid_spec,
        out_shape=out_shape,
        compiler_params=_cparams(("arbitrary",)),
        name="ssd_scan_fwd" if fwd else "ssd_scan_bwd",
    )(l_arr, *tabs, *args)


CV_HALO = 16


def _cv_kernel(l_ref, uc_ref, up_ref, un_ref, w_ref, b_ref, g_ref, beta_ref, o_ref, upad):
    i = pl.program_id(0)
    pos = (i - N_PROMPT_TILES) % TILES_PER_LATENT_SEQ
    is_prompt = i < N_PROMPT_TILES
    no_l = jnp.logical_or(is_prompt, pos == 0)
    no_r = jnp.logical_or(is_prompt, pos == TILES_PER_LATENT_SEQ - 1)
    upad[0:CV_HALO, :] = jnp.where(no_l, 0.0, up_ref[...])
    upad[CV_HALO:CV_HALO + TM, :] = uc_ref[...]
    upad[CV_HALO + TM:CV_HALO + TM + CV_HALO, :] = jnp.where(no_r, 0.0, un_ref[...])
    acc = jnp.zeros((TM, CONV_CH), F32) + b_ref[...]
    pad = CONV_W // 2
    up = upad[...]
    rows = TM + 2 * CV_HALO
    for rot in range(SUBLANES):
        taps = [k for k in range(CONV_W) if (CV_HALO - pad + k) % SUBLANES == rot]
        rolled = up if rot == 0 else pltpu.roll(up, rows - rot, axis=0)
        for k in taps:
            base = CV_HALO - pad + k - rot
            acc = acc + w_ref[k:k + 1, :] * rolled[base:base + TM, :]
    mu = jnp.mean(acc, axis=-1, keepdims=True)
    xc = acc - mu
    var = jnp.mean(xc * xc, axis=-1, keepdims=True)
    y = xc * lax.rsqrt(var + EPS) * g_ref[...] + beta_ref[...]
    o_ref[...] = _silu(y).astype(BF16)


def _cv_call(l_arr, u, w, b, g, beta):
    per = TM // CV_HALO
    nb = T // CV_HALO
    lay = lambda *shape: pl.BlockSpec((None,) + shape, lambda i, l: (l[0],) + (0,) * len(shape))
    grid_spec = pltpu.PrefetchScalarGridSpec(
        num_scalar_prefetch=1,
        grid=(N_TILES,),
        in_specs=[
            pl.BlockSpec((TM, CONV_CH), lambda i, l: (i, 0)),
            pl.BlockSpec((CV_HALO, CONV_CH), lambda i, l: (jnp.maximum(i * per - 1, 0), 0)),
            pl.BlockSpec((CV_HALO, CONV_CH), lambda i, l: (jnp.minimum((i + 1) * per, nb - 1), 0)),
            lay(CONV_W, CONV_CH), lay(1, CONV_CH), lay(1, CONV_CH), lay(1, CONV_CH),
        ],
        out_specs=pl.BlockSpec((TM, CONV_CH), lambda i, l: (i, 0)),
        scratch_shapes=[pltpu.VMEM((TM + 2 * CV_HALO, CONV_CH), F32)],
    )
    return pl.pallas_call(
        _cv_kernel,
        grid_spec=grid_spec,
        out_shape=jax.ShapeDtypeStruct((T, CONV_CH), BF16),
        compiler_params=_cparams(("arbitrary",)),
        name="conformer_conv",
    )(l_arr, u, u, u, w, b, g, beta)


def _lambda_terms(l_ref, lam_ref):
    lf = jnp.full((1, 1), l_ref[0], jnp.int32).astype(F32)
    lam_init = 0.8 - 0.6 * jnp.exp(-0.3 * lf)
    p = lam_ref[...]
    s1 = jnp.sum(p[0:1, :] * p[1:2, :], axis=-1, keepdims=True)
    s2 = jnp.sum(p[2:3, :] * p[3:4, :], axis=-1, keepdims=True)
    lam = jnp.exp(s1) - jnp.exp(s2) + lam_init
    return lam, 1.0 - lam_init


def _attn_body(l_ref, q_ref, k_ref, v_ref, ck_ref, cv_ref, lam_ref, g_ref, o_ref, k_s, v_s, n_ctx):
    @pl.when(pl.program_id(1) == 0)
    def _():
        k_s[n_ctx:, :] = k_ref[...].astype(BF16)
        for h in range(DA_HEADS):
            sl = slice(h * LANES, (h + 1) * LANES)
            if n_ctx:
                k_s[0:n_ctx, sl] = ck_ref[:, h, :].astype(BF16)
                v_s[0:n_ctx, 2 * h * LANES:(2 * h + 1) * LANES] = cv_ref[:, h, :].astype(BF16)
            v_s[n_ctx:, 2 * h * LANES:(2 * h + 1) * LANES] = v_ref[:, sl].astype(BF16)
            v_s[:, (2 * h + 1) * LANES:(2 * h + 2) * LANES] = jnp.ones((v_s.shape[0], LANES), BF16)

    lam, out_scale = _lambda_terms(l_ref, lam_ref)
    tq = q_ref.shape[0]
    lane = lax.broadcasted_iota(jnp.int32, (tq, LANES), 1)
    zero = jnp.zeros((tq, LANES), BF16)
    for h in range(DA_HEADS):
        sl = slice(h * LANES, (h + 1) * LANES)
        qh = q_ref[:, sl]
        kh = k_s[:, sl]
        vh = v_s[:, 2 * h * LANES:(2 * h + 2) * LANES]
        outs = []
        for c in range(2):
            in_c = (lane < DA_HEAD_DIM) if c == 0 else (lane >= DA_HEAD_DIM)
            s = _dot_nt(jnp.where(in_c, qh, zero), kh)
            m = jnp.max(s, axis=-1, keepdims=True)
            pv = _dot(jnp.exp2(s - m).astype(BF16), vh)
            outs.append(pv[:, :LANES] / pv[:, LANES:])
        o = outs[0] - lam * outs[1]
        o = o * lax.rsqrt(jnp.mean(o * o, axis=-1, keepdims=True) + EPS)
        o_ref[:, sl] = (o * g_ref[...] * out_scale).astype(BF16)


def _attn_prompt_kernel(l_ref, q_ref, k_ref, v_ref, lam_ref, g_ref, o_ref, k_s, v_s):
    _attn_body(l_ref, q_ref, k_ref, v_ref, None, None, lam_ref, g_ref, o_ref, k_s, v_s, 0)


def _attn_latent_kernel(l_ref, q_ref, k_ref, v_ref, ck_ref, cv_ref, lam_ref, g_ref, o_ref, k_s, v_s):
    _attn_body(l_ref, q_ref, k_ref, v_ref, ck_ref, cv_ref, lam_ref, g_ref, o_ref, k_s, v_s, PAST_LEN)


def _attn_call(l_arr, q, k, v, cache_k, cache_v, da_lambda, subln_g, latent):
    if latent:
        nseq, seqlen, n_ctx = N_LATENT_SEQ, LATENT_LEN, PAST_LEN
        tile0, seq0 = N_PROMPT_TILES, T_P // LATENT_LEN
    else:
        nseq, seqlen, n_ctx = N_PROMPT_SEQ, PROMPT_LEN, 0
        tile0, seq0 = 0, 0
    nq = seqlen // TM
    lay = lambda *shape: pl.BlockSpec((None,) + shape, lambda b, i, l: (l[0],) + (0,) * len(shape))
    in_specs = [
        pl.BlockSpec((TM, DA_QK), lambda b, i, l: (tile0 + b * nq + i, 0)),
        pl.BlockSpec((seqlen, DA_QK), lambda b, i, l: (seq0 + b, 0)),
        pl.BlockSpec((seqlen, DA_V), lambda b, i, l: (seq0 + b, 0)),
    ]
    args = [q, k, v]
    if latent:
        ctx = pl.BlockSpec((None, None, PAST_LEN, DA_HEADS, LANES), lambda b, i, l: (b, l[0], 0, 0, 0))
        in_specs += [ctx, ctx]
        args += [cache_k, cache_v]
    in_specs += [lay(4, DA_HEAD_DIM), lay(1, DA_V_DIM)]
    args += [da_lambda, subln_g]
    grid_spec = pltpu.PrefetchScalarGridSpec(
        num_scalar_prefetch=1,
        grid=(nseq, nq),
        in_specs=in_specs,
        out_specs=pl.BlockSpec((TM, DA_V), lambda b, i, l: (b * nq + i, 0)),
        scratch_shapes=[pltpu.VMEM((n_ctx + seqlen, DA_QK), BF16), pltpu.VMEM((n_ctx + seqlen, 2 * DA_V), BF16)],
    )
    return pl.pallas_call(
        _attn_latent_kernel if latent else _attn_prompt_kernel,
        grid_spec=grid_spec,
        out_shape=jax.ShapeDtypeStruct((nseq * seqlen, DA_V), BF16),
        compiler_params=_cparams(("arbitrary", "arbitrary")),
        name="diff_attn_latent" if latent else "diff_attn_prompt",
    )(l_arr, *args)


def _combine_kernel(l_ref, xp_ref, xl_ref, h_ref, yf_ref, yb_ref, z_ref, uc_ref, oap_ref, oal_ref, mod_ref,
                    wg_ref, bg_ref, sg_ref, wa_ref, wb_ref, wc_ref, wo_ref, g2_ref, wr_ref, br_ref,
                    x1_ref, h2_ref, route_ref, cnt_ref, e1_ref, e2_ref, r1_ref, r2_ref, carry):
    i = pl.program_id(0)

    @pl.when(i == 0)
    def _():
        carry[...] = jnp.zeros_like(carry)

    y = (yf_ref[...].astype(F32) + yb_ref[...].astype(F32)) * _silu(z_ref[...].astype(F32))
    y = y * lax.rsqrt(jnp.mean(y * y, axis=-1, keepdims=True) + EPS) * sg_ref[...]
    br_a = _dot(y.astype(BF16), wa_ref[...])
    br_b = _dot(uc_ref[...], wb_ref[...])
    br_c = _dot(_pair_read(i, oap_ref, oal_ref), wc_ref[...])
    hb = h_ref[...]

    def gate(n):
        return _sigmoid(_dot(hb, wg_ref[:, n * D:(n + 1) * D]) + bg_ref[:, n * D:(n + 1) * D])

    mix = gate(0) * br_a + gate(1) * br_b + gate(2) * br_c
    mixed = _dot(mix.astype(BF16), wo_ref[...])
    x1 = _pair_read(i, xp_ref, xl_ref) + mod_ref[2:3, :] * mixed
    x1_ref[...] = x1
    xn = x1 * lax.rsqrt(jnp.mean(x1 * x1, axis=-1, keepdims=True) + EPS)
    h2 = xn * g2_ref[...] * (1.0 + mod_ref[4:5, :]) + mod_ref[3:4, :]
    h2_ref[...] = h2

    h2_hi = h2.astype(BF16)
    h2_mid = (h2 - h2_hi.astype(F32)).astype(BF16)
    logits = _dot(jnp.concatenate([h2_hi, h2_mid, h2_hi], axis=1), wr_ref[...]) + br_ref[...]
    lane = lax.broadcasted_iota(jnp.int32, (TM, LANES), 1).astype(F32)

    def first_argmax(vals, vmax):
        return jnp.min(jnp.where(vals == vmax, lane, float(LANES)), axis=-1, keepdims=True)

    glog = jnp.where(lane < MOE_GROUPS, logits, NEG)
    gmax = jnp.max(glog, axis=-1, keepdims=True)
    gsel = first_argmax(glog, gmax)
    p_g = 1.0 / jnp.sum(jnp.exp(glog - gmax), axis=-1, keepdims=True)
    lo = ROUTER_LANE0 + MOE_EPG * gsel
    elog = jnp.where(jnp.logical_and(lane >= lo, lane < lo + MOE_EPG), logits, NEG)
    v1 = jnp.max(elog, axis=-1, keepdims=True)
    i1 = first_argmax(elog, v1)
    elog2 = jnp.where(lane == i1, NEG, elog)
    v2 = jnp.max(elog2, axis=-1, keepdims=True)
    i2 = first_argmax(elog2, v2)
    t2 = jnp.exp(v2 - v1)
    w1 = p_g / (1.0 + t2)
    w2 = p_g * t2 / (1.0 + t2)

    oh1 = lane == i1
    oh2 = lane == i2
    both = jnp.where(jnp.logical_or(oh1, oh2), 1.0, 0.0)
    r = lax.broadcasted_iota(jnp.int32, (TM, TM), 0)
    c = lax.broadcasted_iota(jnp.int32, (TM, TM), 1)
    strict_lower = jnp.where(c < r, 1.0, 0.0).astype(BF16)
    before = _dot(strict_lower, both.astype(BF16)) + carry[0:1, :]
    r1 = jnp.sum(jnp.where(oh1, before, 0.0), axis=-1, keepdims=True)
    r2 = jnp.sum(jnp.where(oh2, before, 0.0), axis=-1, keepdims=True)
    carry[...] = carry[...] + jnp.sum(both, axis=0, keepdims=True)
    cnt_ref[...] = carry[...]

    e1 = i1 - ROUTER_LANE0
    e2 = i2 - ROUTER_LANE0
    route = jnp.zeros((TM, LANES), F32)
    for n, val in enumerate((e1, e2, w1, w2, r1, r2)):
        route = jnp.where(lane == float(n), val, route)
    route_ref[...] = route
    route_t = route.T
    for n, ref in ((0, e1_ref), (1, e2_ref), (4, r1_ref), (5, r2_ref)):
        ref[...] = route_t[n:n + 1, :].astype(jnp.int32)


def _combine_call(l_arr, x_p, x_l, h, y_f, y_b, z, uc, oa_p, oa_l, mod6, w_gates, b_gates, ssd_norm_g, w_br_ssd,
                  w_br_conv, w_br_attn, w_out, norm2_g, w_router, b_router):
    tok = lambda n: pl.BlockSpec((TM, n), lambda i, l: (i, 0))
    lay = lambda *shape: pl.BlockSpec((None,) + shape, lambda i, l: (l[0],) + (0,) * len(shape))
    grid_spec = pltpu.PrefetchScalarGridSpec(
        num_scalar_prefetch=1,
        grid=(N_TILES,),
        in_specs=_pair_specs(D) + [tok(D), tok(D), tok(D), tok(D), tok(CONV_CH)] + _pair_specs(DA_V) + [
            pl.BlockSpec((None, None, 6, D), lambda i, l: (l[0], _mod_row(i), 0, 0)),
            lay(D, 3 * D), lay(1, 3 * D), lay(1, D), lay(D, D), lay(CONV_CH, D), lay(DA_V, D), lay(D, D),
            lay(1, D), lay(3 * D, LANES), lay(1, LANES),
        ],
        out_specs=[tok(D), tok(D), tok(LANES), pl.BlockSpec((SUBLANES, LANES), lambda i, l: (0, 0))]
        + [pl.BlockSpec((None, 1, TM), lambda i, l: (i, 0, 0))] * 4,
        scratch_shapes=[pltpu.VMEM((SUBLANES, LANES), F32)],
    )
    return pl.pallas_call(
        _combine_kernel,
        grid_spec=grid_spec,
        out_shape=[jax.ShapeDtypeStruct((T, D), F32), jax.ShapeDtypeStruct((T, D), F32),
                   jax.ShapeDtypeStruct((T, LANES), F32), jax.ShapeDtypeStruct((SUBLANES, LANES), F32)]
        + [jax.ShapeDtypeStruct((N_TILES, 1, TM), jnp.int32)] * 4,
        compiler_params=_cparams(("arbitrary",)),
        name="branch_combine_router",
    )(l_arr, x_p, x_l, h, y_f, y_b, z, uc, oa_p, oa_l, mod6, w_gates, b_gates, ssd_norm_g, w_br_ssd, w_br_conv,
      w_br_attn, w_out, norm2_g, w_router, b_router)


ROW_DMA_UNROLL = 64


def _dispatch_kernel(e1_ref, e2_ref, r1_ref, r2_ref, ps_ref, pe_ref, h2_ref, xs_ref, zbuf, sem, zsem):
    i = pl.program_id(0)
    base = i * TM

    @pl.when(i == 0)
    def _():
        zbuf[...] = jnp.zeros_like(zbuf)

        def zero_fill(e):
            off = pl.multiple_of(pe_ref[e] - MOE_BLOCK, MOE_BLOCK)
            return pltpu.make_async_copy(zbuf, xs_ref.at[pl.ds(off, MOE_BLOCK), :], zsem)

        def unused_fill(b):
            off = pl.multiple_of(b * MOE_BLOCK, MOE_BLOCK)
            return pltpu.make_async_copy(zbuf, xs_ref.at[pl.ds(off, MOE_BLOCK), :], zsem)

        n_used = pe_ref[MOE_E - 1] // MOE_BLOCK
        for e in range(MOE_E):
            @pl.when(pe_ref[e] > ps_ref[e])
            def _():
                zero_fill(e).start()
        lax.fori_loop(n_used, N_SLOT_BLOCKS, lambda b, c: (unused_fill(b).start(), c)[1], 0)
        for e in range(MOE_E):
            @pl.when(pe_ref[e] > ps_ref[e])
            def _():
                zero_fill(e).wait()
        lax.fori_loop(n_used, N_SLOT_BLOCKS, lambda b, c: (unused_fill(b).wait(), c)[1], 0)

    def issue(t, carry):
        for e_ref, r_ref in ((e1_ref, r1_ref), (e2_ref, r2_ref)):
            dest = ps_ref[e_ref[base + t]] + r_ref[base + t]
            pltpu.make_async_copy(h2_ref.at[pl.ds(t, 1), :], xs_ref.at[pl.ds(dest, 1), :], sem).start()
        return carry

    lax.fori_loop(0, TM, issue, 0, unroll=ROW_DMA_UNROLL)
    for _ in range(2):
        pltpu.make_async_copy(h2_ref, xs_ref.at[pl.ds(0, TM), :], sem).wait()


def _dispatch_call(e1, e2, r1, r2, pstart, pend, h2):
    grid_spec = pltpu.PrefetchScalarGridSpec(
        num_scalar_prefetch=6,
        grid=(N_TILES,),
        in_specs=[pl.BlockSpec((TM, D), lambda i, *_: (i, 0))],
        out_specs=pl.BlockSpec(memory_space=pl.ANY),
        scratch_shapes=[pltpu.VMEM((MOE_BLOCK, D), F32), pltpu.SemaphoreType.DMA(()), pltpu.SemaphoreType.DMA(())],
    )
    return pl.pallas_call(
        _dispatch_kernel,
        grid_spec=grid_spec,
        out_shape=jax.ShapeDtypeStruct((N_SLOTS, D), F32),
        compiler_params=_cparams(("arbitrary",)),
        name="moe_dispatch",
    )(e1, e2, r1, r2, pstart, pend, h2)


def _moe_kernel(l_ref, be_ref, nb_ref, xs_ref, wg_ref, wu_ref, wd_ref, ys_ref, wg_s, wu_s, wd_s):
    i = pl.program_id(0)
    prev = be_ref[jnp.maximum(i - 1, 0)]

    @pl.when(jnp.logical_or(i == 0, be_ref[i] != prev))
    def _():
        wg_s[...] = wg_ref[...].astype(BF16)
        wu_s[...] = wu_ref[...].astype(BF16)
        wd_s[...] = wd_ref[...].astype(BF16)

    @pl.when(i < nb_ref[0])
    def _():
        xb = xs_ref[...].astype(BF16)
        hid = _silu(_dot(xb, wg_s[...])) * _dot(xb, wu_s[...])
        ys_ref[...] = _dot(hid.astype(BF16), wd_s[...])

    @pl.when(i >= nb_ref[0])
    def _():
        ys_ref[...] = jnp.zeros_like(ys_ref)


def _moe_call(l_arr, block_expert, n_used, xs, w_gate, w_up, w_down):
    wspec = lambda a, b: pl.BlockSpec((None, None, a, b), lambda i, l, be, nb: (l[0], be[i], 0, 0))
    grid_spec = pltpu.PrefetchScalarGridSpec(
        num_scalar_prefetch=3,
        grid=(N_SLOT_BLOCKS,),
        in_specs=[pl.BlockSpec((MOE_BLOCK, D), lambda i, l, be, nb: (jnp.minimum(i, nb[0] - 1), 0)),
                  wspec(D, MOE_HIDDEN), wspec(D, MOE_HIDDEN), wspec(MOE_HIDDEN, D)],
        out_specs=pl.BlockSpec((MOE_BLOCK, D), lambda i, l, be, nb: (i, 0)),
        scratch_shapes=[pltpu.VMEM((D, MOE_HIDDEN), BF16), pltpu.VMEM((D, MOE_HIDDEN), BF16),
                        pltpu.VMEM((MOE_HIDDEN, D), BF16)],
    )
    return pl.pallas_call(
        _moe_kernel,
        grid_spec=grid_spec,
        out_shape=jax.ShapeDtypeStruct((N_SLOTS, D), F32),
        compiler_params=_cparams(("arbitrary",)),
        name="moe_experts",
    )(l_arr, block_expert, n_used, xs, w_gate, w_up, w_down)


def _moe_combine_kernel(e1_ref, e2_ref, r1_ref, r2_ref, ps_ref, x1_ref, route_ref, mod_ref, fg_ref, ys_ref,
                        op_ref, ol_ref, buf, sem, *, final):
    i = pl.program_id(0)
    slot = i % 2

    def gather_tile(tile, dst_slot):
        base = tile * TM

        def issue(t, carry):
            for which, (e_ref, r_ref) in enumerate(((e1_ref, r1_ref), (e2_ref, r2_ref))):
                src = ps_ref[e_ref[base + t]] + r_ref[base + t]
                pltpu.make_async_copy(ys_ref.at[pl.ds(src, 1), :], buf.at[dst_slot, which, pl.ds(t, 1), :],
                                      sem.at[dst_slot]).start()
            return carry

        lax.fori_loop(0, TM, issue, 0, unroll=ROW_DMA_UNROLL)

    @pl.when(i == 0)
    def _():
        gather_tile(0, 0)

    @pl.when(i + 1 < N_TILES)
    def _():
        gather_tile(i + 1, 1 - slot)

    for which in range(2):
        pltpu.make_async_copy(ys_ref.at[pl.ds(0, TM), :], buf.at[slot, which], sem.at[slot]).wait()
    w1 = route_ref[:, 2:3]
    w2 = route_ref[:, 3:4]
    y = buf[slot, 0] * w1 + buf[slot, 1] * w2
    x2 = x1_ref[...] + mod_ref[5:6, :] * y
    if final:
        x2 = x2 * lax.rsqrt(jnp.mean(x2 * x2, axis=-1, keepdims=True) + EPS) * fg_ref[...]

    @pl.when(i < N_PROMPT_TILES)
    def _():
        op_ref[...] = x2

    @pl.when(i >= N_PROMPT_TILES)
    def _():
        ol_ref[...] = x2


def _moe_combine_call(l_arr, e1, e2, r1, r2, pstart, x1, route, mod6, final_g, ys, final):
    grid_spec = pltpu.PrefetchScalarGridSpec(
        num_scalar_prefetch=6,
        grid=(N_TILES,),
        in_specs=[
            pl.BlockSpec((TM, D), lambda i, *_: (i, 0)),
            pl.BlockSpec((TM, LANES), lambda i, *_: (i, 0)),
            pl.BlockSpec((None, None, 6, D), lambda i, l, *_: (l[0], _mod_row(i), 0, 0)),
            pl.BlockSpec((1, D), lambda i, *_: (0, 0)),
            pl.BlockSpec(memory_space=pl.ANY),
        ],
        out_specs=_pair_specs(D),
        scratch_shapes=[pltpu.VMEM((2, 2, TM, D), F32), pltpu.SemaphoreType.DMA((2,))],
    )

    def body(l_ref, *rest):
        del l_ref
        _moe_combine_kernel(*rest, final=final)

    return pl.pallas_call(
        body,
        grid_spec=grid_spec,
        out_shape=[jax.ShapeDtypeStruct((T_P, D), F32), jax.ShapeDtypeStruct((T_L, D), F32)],
        compiler_params=_cparams(("arbitrary",)),
        name="moe_combine_final" if final else "moe_combine",
    )(l_arr, e1, e2, r1, r2, pstart, x1, route, mod6, final_g, ys)


def _rope_tables():
    n = LATENT_LEN
    rows = n // GRID_W
    row = jnp.repeat(jnp.arange(rows), GRID_W).astype(F32)
    col = jnp.tile(jnp.arange(GRID_W), rows).astype(F32)
    axis_dim = DA_HEAD_DIM // 2
    inv_freq = 1.0 / (ROPE_BASE ** (jnp.arange(0, axis_dim, 2, dtype=F32) / axis_dim))
    ar, ac = row[:, None] * inv_freq, col[:, None] * inv_freq
    cos64 = jnp.concatenate([jnp.cos(ar), jnp.cos(ar), jnp.cos(ac), jnp.cos(ac)], axis=1)
    sin64 = jnp.concatenate([-jnp.sin(ar), jnp.sin(ar), -jnp.sin(ac), jnp.sin(ac)], axis=1)
    reps = DA_QK // DA_HEAD_DIM
    cos = jnp.concatenate([jnp.tile(cos64, (1, reps)), jnp.ones((TM, DA_QK), F32)], axis=0)
    sin = jnp.concatenate([jnp.tile(sin64, (1, reps)), jnp.zeros((TM, DA_QK), F32)], axis=0)
    return cos, sin


def _pad_lanes(a, n=LANES):
    return jnp.pad(a, [(0, 0)] * (a.ndim - 1) + [(0, n - a.shape[-1])])


def kernel(x_prompt, x_sample, cache_k, cache_v, state_ssd, c, c_ctx, w_ada, b_ada, norm1_g, norm2_g, w_in, b_in,
           ssd_conv_w, ssd_conv_b, ssd_dt_bias, ssd_a_log, ssd_d, ssd_norm_g, w_br_ssd, cv_dw_w, cv_dw_b, cv_ln_g,
           cv_ln_b, w_br_conv, da_lambda, da_subln_g, w_br_attn, w_out, moe_w_group, moe_b_group, moe_w_expert,
           moe_b_expert, moe_w_gate, moe_w_up, moe_w_down, final_g):
    L = DEPTH
    x_p, x_l = x_prompt.reshape(T_P, D), x_sample.reshape(T_L, D)

    cvec = jnp.concatenate([c, c_ctx[None, :], jnp.zeros((16 - N_LATENT_SEQ - 1, D), F32)], axis=0)
    mod6 = _ada_call(cvec, w_ada, b_ada).reshape(L, 16, 6, D)

    o_z, o_xbc, o_dt = 0, D, D + SSD_XBC
    o_glu = o_dt + 2 * SSD_HEADS
    o_q = o_glu + 2 * CONV_CH
    o_k, o_v, o_g = o_q + DA_QK, o_q + 2 * DA_QK, o_q + 2 * DA_QK + DA_V

    def regroup(w):
        return jnp.concatenate([w[..., o_z:o_dt], _pad_lanes(w[..., o_dt:o_glu]), w[..., o_glu:o_g]], axis=-1)

    w_proj = regroup(w_in).astype(BF16)
    b_proj = regroup(b_in).reshape(L, 1, N_PROJ)
    w_gates = w_in[..., o_g:].astype(BF16)
    b_gates = b_in[..., o_g:].reshape(L, 1, 3 * D)
    cos_tab, sin_tab = _rope_tables()

    tabs_f, tabs_b = _ssd_tables(True), _ssd_tables(False)
    dt_bias = _pad_lanes(ssd_dt_bias.reshape(L, 1, 2 * SSD_HEADS))
    a_log = _pad_lanes(ssd_a_log.reshape(L, 1, 2 * SSD_HEADS))
    dskip = jnp.repeat(ssd_d, SSD_HEAD_DIM, axis=-1).reshape(L, 1, D)
    hp = np.arange(D) // SSD_HEAD_DIM
    e_f, e_b = (jnp.asarray(np.tile(np.arange(LANES)[:, None] == ho + hp[None, :], (3, 1)).astype(np.float32))
                .astype(BF16) for ho in (0, SSD_HEADS))
    h0t = jnp.transpose(state_ssd, (1, 0, 2, 5, 3, 4)).reshape(L, N_LATENT_SEQ, 2, SSD_STATE, D)
    h0t = jnp.concatenate([h0t, jnp.zeros((L, 1, 2, SSD_STATE, D), F32)], axis=1)

    w_router = _pad_lanes(jnp.concatenate([moe_w_group, moe_w_expert], axis=-1))
    wr_hi = w_router.astype(BF16)
    wr_mid = (w_router - wr_hi.astype(F32)).astype(BF16)
    w_router = jnp.concatenate([wr_hi, wr_hi, wr_mid], axis=1)
    b_router = _pad_lanes(jnp.concatenate([moe_b_group, moe_b_expert], axis=-1)).reshape(L, 1, LANES)
    w_br_ssd_b, w_br_conv_b = w_br_ssd.astype(BF16), w_br_conv.astype(BF16)
    w_br_attn_b, w_out_b = w_br_attn.astype(BF16), w_out.astype(BF16)
    r3 = lambda a: a.reshape(L, 1, a.shape[-1])

    ks_new, vs_new, ss_new = [], [], []
    for layer in range(L):
        l_arr = jnp.full((1,), layer, jnp.int32)
        h, z, xbc, dt, u, q, k, v = _inproj_call(l_arr, x_p, x_l, mod6, r3(norm1_g), w_proj, b_proj,
                                                 cos_tab, sin_tab)
        y_f, st_f, xc = _ssd_call(l_arr, tabs_f, xbc, dt, ssd_conv_w, r3(ssd_conv_b), dt_bias, a_log, dskip,
                                  e_f, h0t, None, True)
        y_b, st_b = _ssd_call(l_arr, tabs_b, None, dt, None, None, dt_bias, a_log, None, e_b, h0t, xc, False)
        uc = _cv_call(l_arr, u, cv_dw_w, r3(cv_dw_b), r3(cv_ln_g), r3(cv_ln_b))
        oa_p = _attn_call(l_arr, q, k, v, None, None, da_lambda, r3(da_subln_g), latent=False)
        oa_l = _attn_call(l_arr, q, k, v, cache_k, cache_v, da_lambda, r3(da_subln_g), latent=True)
        x1, h2, route, counts, e1, e2, r1, r2 = _combine_call(
            l_arr, x_p, x_l, h, y_f, y_b, z, uc, oa_p, oa_l, mod6, w_gates, b_gates, r3(ssd_norm_g), w_br_ssd_b,
            w_br_conv_b, w_br_attn_b, w_out_b, r3(norm2_g), w_router, b_router)
        e1, e2, r1, r2 = (a.reshape(T) for a in (e1, e2, r1, r2))
        cnt = counts[0, ROUTER_LANE0:ROUTER_LANE0 + MOE_E].astype(jnp.int32)
        pcnt = (cnt + MOE_BLOCK - 1) // MOE_BLOCK * MOE_BLOCK
        pend = jnp.cumsum(pcnt)
        pstart = pend - pcnt
        blk0 = jnp.arange(N_SLOT_BLOCKS, dtype=jnp.int32) * MOE_BLOCK
        block_expert = jnp.minimum(jnp.sum((pend[None, :] <= blk0[:, None]).astype(jnp.int32), axis=1), MOE_E - 1)
        n_used = pend[-1:] // MOE_BLOCK

        xs = _dispatch_call(e1, e2, r1, r2, pstart, pend, h2)
        ys = _moe_call(l_arr, block_expert, n_used, xs, moe_w_gate, moe_w_up, moe_w_down)
        x_p, x_l = _moe_combine_call(l_arr, e1, e2, r1, r2, pstart, x1, route, mod6, final_g.reshape(1, D), ys,
                                     layer == L - 1)

        ks_new.append(k[:T_P].reshape(N_PROMPT_SEQ, PROMPT_LEN, DA_HEADS, 2 * DA_HEAD_DIM))
        vs_new.append(v[:T_P].reshape(N_PROMPT_SEQ, PROMPT_LEN, DA_HEADS, DA_V_DIM))
        s = jnp.stack([st_f[:N_PROMPT_SEQ], st_b[:N_PROMPT_SEQ]], axis=1)
        s = s.reshape(N_PROMPT_SEQ, 2, SSD_STATE, SSD_HEADS, SSD_HEAD_DIM)
        ss_new.append(jnp.transpose(s, (0, 1, 3, 4, 2)))

    y_prompt = x_p.reshape(N_PROMPT_SEQ, PROMPT_LEN, D)
    y_sample = x_l.reshape(N_LATENT_SEQ, LATENT_LEN, D)
    return (y_prompt, y_sample, jnp.stack(ks_new, axis=1), jnp.stack(vs_new, axis=1), jnp.stack(ss_new, axis=1))
```

```python
import functools
import math

import jax
import jax.numpy as jnp
import numpy as np
from jax import lax
from jax.experimental import pallas as pl
from jax.experimental.pallas import tpu as pltpu

D = 1024
DEPTH = 4
N_PROMPT_SEQ, PROMPT_LEN = 16, 256
N_LATENT_SEQ, LATENT_LEN = 8, 2048
PAST_LEN = 512
T_P = N_PROMPT_SEQ * PROMPT_LEN
T_L = N_LATENT_SEQ * LATENT_LEN
T = T_P + T_L
GRID_W = 64
EPS = 1e-6
SSD_HEADS, SSD_HEAD_DIM, SSD_STATE, SSD_GROUPS = 16, 64, 64, 2
SSD_CONV_W = 5
SSD_CHUNK = 128
SSD_XBC = D + 2 * SSD_GROUPS * SSD_STATE
CONV_CH, CONV_W = 512, 31
DA_HEADS, DA_HEAD_DIM, DA_V_DIM = 4, 64, 128
DA_QK = DA_HEADS * 2 * DA_HEAD_DIM
DA_V = DA_HEADS * DA_V_DIM
ROPE_BASE = 10000.0
MOE_GROUPS, MOE_EPG, MOE_E, MOE_HIDDEN = 4, 8, 32, 512
ROUTER_LANE0 = MOE_GROUPS

LANES = 128
SUBLANES = 8
TM = 256
N_TILES = T // TM
N_PROMPT_TILES = T_P // TM
TILES_PER_LATENT_SEQ = LATENT_LEN // TM
MOE_BLOCK = 256
N_SLOT_BLOCKS = (2 * T) // MOE_BLOCK + MOE_E
N_SLOTS = N_SLOT_BLOCKS * MOE_BLOCK
VMEM_LIMIT = 56 * 1024 * 1024

F32 = jnp.float32
BF16 = jnp.bfloat16
HI = lax.Precision.HIGHEST
NEG = -1e30


def _cparams(sem, vmem=VMEM_LIMIT):
    return pltpu.CompilerParams(dimension_semantics=sem, vmem_limit_bytes=vmem)


def _mod_row(i):
    return jnp.where(i < N_PROMPT_TILES, N_LATENT_SEQ, (i - N_PROMPT_TILES) // TILES_PER_LATENT_SEQ)


def _pair_specs(n, tile_arg=0):
    def prompt(*a):
        return (jnp.minimum(a[tile_arg], N_PROMPT_TILES - 1), 0)

    def latent(*a):
        return (jnp.maximum(a[tile_arg] - N_PROMPT_TILES, 0), 0)

    return [pl.BlockSpec((TM, n), prompt), pl.BlockSpec((TM, n), latent)]


def _pair_read(i, p_ref, l_ref):
    return jnp.where(i < N_PROMPT_TILES, p_ref[...], l_ref[...])


def _silu(x):
    return x * (1.0 / (1.0 + jnp.exp(-x)))


def _sigmoid(x):
    return 1.0 / (1.0 + jnp.exp(-x))


def _softplus(x):
    return jnp.maximum(x, 0.0) + jnp.log(1.0 + jnp.exp(-jnp.abs(x)))


def _dot(a, b, **kw):
    return jnp.dot(a, b, preferred_element_type=F32, **kw)


def _split3(x):
    hi = x.astype(BF16)
    r1 = x - hi.astype(F32)
    mid = r1.astype(BF16)
    lo = (r1 - mid.astype(F32)).astype(BF16)
    return jnp.concatenate([hi, mid, lo], axis=1)


def _dot_nt(a, b):
    return lax.dot_general(a, b, (((1,), (1,)), ((), ())), preferred_element_type=F32)


def _ada_kernel(c_ref, w_ref, b_ref, o_ref):
    cs = _silu(c_ref[...])
    o_ref[...] = _dot(cs, w_ref[...], precision=HI) + b_ref[...]


def _ada_call(cvec, w_ada, b_ada):
    nj = 6
    return pl.pallas_call(
        _ada_kernel,
        out_shape=jax.ShapeDtypeStruct((DEPTH, 16, 6 * D), F32),
        grid=(DEPTH, nj),
        in_specs=[
            pl.BlockSpec((16, D), lambda l, j: (0, 0)),
            pl.BlockSpec((None, D, D), lambda l, j: (l, 0, j)),
            pl.BlockSpec((None, 1, D), lambda l, j: (l, 0, j)),
        ],
        out_specs=pl.BlockSpec((None, 16, D), lambda l, j: (l, 0, j)),
        compiler_params=_cparams(("arbitrary", "arbitrary")),
        name="ada_mod",
    )(cvec, w_ada, b_ada.reshape(DEPTH, 1, 6 * D))


_C_Z = (0, D)
_C_XBC = (_C_Z[1], _C_Z[1] + SSD_XBC)
_C_DT = (_C_XBC[1], _C_XBC[1] + LANES)
_C_GLU = (_C_DT[1], _C_DT[1] + 2 * CONV_CH)
_C_Q = (_C_GLU[1], _C_GLU[1] + DA_QK)
_C_K = (_C_Q[1], _C_Q[1] + DA_QK)
_C_V = (_C_K[1], _C_K[1] + DA_V)
N_PROJ = _C_V[1]


def _swap16(x):
    cols = []
    for c in range(x.shape[1] // LANES):
        xc = x[:, c * LANES:(c + 1) * LANES]
        lane = lax.broadcasted_iota(jnp.int32, xc.shape, 1)
        from_right = pltpu.roll(xc, LANES - 16, axis=1)
        from_left = pltpu.roll(xc, 16, axis=1)
        cols.append(jnp.where((lane >> 4) % 2 == 0, from_right, from_left))
    return jnp.concatenate(cols, axis=1)


def _inproj_kernel(l_ref, xp_ref, xl_ref, *rest):
    _inproj_body(_pair_read(pl.program_id(0), xp_ref, xl_ref), *rest)


def _inproj_moe_kernel(l_ref, e1_ref, e2_ref, r1_ref, r2_ref, ps_ref,
                       x1_ref, route_ref, modp_ref, ys_ref, mod_ref, g_ref, w_ref, b_ref, cos_ref, sin_ref,
                       h_ref, z_ref, xbc_ref, dt_ref, u_ref, q_ref, k_ref, v_ref, xop_ref, xol_ref, buf, sem):
    i = pl.program_id(0)
    slot = i % 2

    def gather_rows(tile, dst_slot, t0, t1):
        base = tile * TM

        def issue(t, carry=0):
            for which, (e_ref, r_ref) in enumerate(((e1_ref, r1_ref), (e2_ref, r2_ref))):
                src = ps_ref[e_ref[base + t]] + r_ref[base + t]
                pltpu.make_async_copy(ys_ref.at[pl.ds(src, 1), :], buf.at[dst_slot, which, pl.ds(t, 1), :],
                                      sem.at[dst_slot]).start()
            return carry

        if t0 is None:
            lax.fori_loop(0, TM, issue, 0, unroll=ROW_DMA_UNROLL)
        else:
            for t in range(t0, t1):
                issue(t)

    def wait_tile(s):
        for which in range(2):
            pltpu.make_async_copy(ys_ref.at[pl.ds(0, TM), :], buf.at[s, which], sem.at[s]).wait()

    @pl.when(i == 0)
    def _():
        gather_rows(0, 0, None, None)

    wait_tile(slot)
    y = buf[slot, 0] * route_ref[:, 2:3] + buf[slot, 1] * route_ref[:, 3:4]
    x = x1_ref[...] + modp_ref[5:6, :] * y
    nxt = jnp.minimum(i + 1, N_TILES - 1)

    def spread(c0, c1):
        gather_rows(nxt, 1 - slot, (c0 * TM) // N_PROJ, (c1 * TM) // N_PROJ)

    _inproj_body(x, mod_ref, g_ref, w_ref, b_ref, cos_ref, sin_ref,
                 h_ref, z_ref, xbc_ref, dt_ref, u_ref, q_ref, k_ref, v_ref, before_part=spread)

    @pl.when(i < N_PROMPT_TILES)
    def _():
        xop_ref[...] = x

    @pl.when(i >= N_PROMPT_TILES)
    def _():
        xol_ref[...] = x

    @pl.when(i == N_TILES - 1)
    def _():
        wait_tile(1 - slot)


def _inproj_body(x, mod_ref, g_ref, w_ref, b_ref, cos_ref, sin_ref,
                 h_ref, z_ref, xbc_ref, dt_ref, u_ref, q_ref, k_ref, v_ref, before_part=None):
    xn = x * lax.rsqrt(jnp.mean(x * x, axis=-1, keepdims=True) + EPS)
    h = xn * g_ref[...] * (1.0 + mod_ref[1:2, :]) + mod_ref[0:1, :]
    hb = h.astype(BF16)
    h_ref[...] = hb

    def proj(c):
        if before_part is not None:
            before_part(c[0], c[1])
        return _dot(hb, w_ref[:, c[0]:c[1]]) + b_ref[:, c[0]:c[1]]

    z_ref[...] = proj(_C_Z).astype(BF16)
    xbc_ref[...] = proj(_C_XBC)
    dt_ref[...] = proj(_C_DT)
    glu = proj(_C_GLU)
    u_ref[...] = glu[:, :CONV_CH] * _sigmoid(glu[:, CONV_CH:])
    cos = cos_ref[...]
    sin = sin_ref[...]
    q = proj(_C_Q)
    q = q * cos + _swap16(q) * sin
    q_ref[...] = (q * (DA_HEAD_DIM ** -0.5 * math.log2(math.e))).astype(BF16)
    k = proj(_C_K)
    k_ref[...] = k * cos + _swap16(k) * sin
    v_ref[...] = proj(_C_V)


def _inproj_call(l_arr, x_p, x_l, mod6, norm1_g, w_proj, b_proj, cos_tab, sin_tab, moe=None):
    tok = lambda n: pl.BlockSpec((TM, n), lambda i, *_: (i, 0))
    rope_blk = lambda i, *_: (jnp.where(i < N_PROMPT_TILES, TILES_PER_LATENT_SEQ,
                                        (i - N_PROMPT_TILES) % TILES_PER_LATENT_SEQ), 0)
    mod_blk = lambda back: pl.BlockSpec((None, None, 6, D), lambda i, l, *_: (l[0] - back, _mod_row(i), 0, 0))
    lay = lambda *shape: pl.BlockSpec((None,) + shape, lambda i, l, *_: (l[0],) + (0,) * len(shape))
    proj_in = [mod_blk(0), lay(1, D), lay(D, N_PROJ), lay(1, N_PROJ),
               pl.BlockSpec((TM, DA_QK), rope_blk), pl.BlockSpec((TM, DA_QK), rope_blk)]
    proj_args = (mod6, norm1_g, w_proj, b_proj, cos_tab, sin_tab)
    out_specs = [tok(D), tok(D), tok(SSD_XBC), tok(LANES), tok(CONV_CH), tok(DA_QK), tok(DA_QK), tok(DA_V)]
    sds = lambda n, dt: jax.ShapeDtypeStruct((T, n), dt)
    out_shape = [sds(D, BF16), sds(D, BF16), sds(SSD_XBC, F32), sds(LANES, F32), sds(CONV_CH, F32),
                 sds(DA_QK, BF16), sds(DA_QK, F32), sds(DA_V, F32)]
    if moe is None:
        grid_spec = pltpu.PrefetchScalarGridSpec(num_scalar_prefetch=1, grid=(N_TILES,),
                                                 in_specs=_pair_specs(D) + proj_in, out_specs=out_specs)
        return pl.pallas_call(_inproj_kernel, grid_spec=grid_spec, out_shape=out_shape,
                              compiler_params=_cparams(("arbitrary",)), name="inproj",
                              )(l_arr, x_p, x_l, *proj_args)
    e1, e2, r1, r2, pstart, x1, route, ys = moe
    grid_spec = pltpu.PrefetchScalarGridSpec(
        num_scalar_prefetch=6, grid=(N_TILES,),
        in_specs=[tok(D), tok(LANES), mod_blk(1), pl.BlockSpec(memory_space=pl.ANY)] + proj_in,
        out_specs=out_specs + _pair_specs(D),
        scratch_shapes=[pltpu.VMEM((2, 2, TM, D), F32), pltpu.SemaphoreType.DMA((2,))])
    return pl.pallas_call(
        _inproj_moe_kernel, grid_spec=grid_spec,
        out_shape=out_shape + [jax.ShapeDtypeStruct((T_P, D), F32), jax.ShapeDtypeStruct((T_L, D), F32)],
        compiler_params=_cparams(("arbitrary",)), name="moe_combine_inproj",
    )(l_arr, e1, e2, r1, r2, pstart, x1, route, mod6, ys, *proj_args)


N_CHUNKS = T // SSD_CHUNK
N_PROMPT_CHUNKS = T_P // SSD_CHUNK
CH_PER_PROMPT = PROMPT_LEN // SSD_CHUNK
CH_PER_LATENT = LATENT_LEN // SSD_CHUNK
HALO = SUBLANES
STATE_DUMP = N_PROMPT_SEQ


def _ssd_tables(fwd):
    cidx, flags, h0, so = (np.zeros((N_CHUNKS,), np.int32) for _ in range(4))
    for j in range(N_CHUNKS):
        c = j if fwd else N_CHUNKS - 1 - j
        if c < N_PROMPT_CHUNKS:
            seq, pos, n = c // CH_PER_PROMPT, c % CH_PER_PROMPT, CH_PER_PROMPT
            h0i, soi = N_LATENT_SEQ, seq
        else:
            cc = c - N_PROMPT_CHUNKS
            seq, pos, n = cc // CH_PER_LATENT, cc % CH_PER_LATENT, CH_PER_LATENT
            h0i, soi = seq, STATE_DUMP
        first = pos == 0 if fwd else pos == n - 1
        cidx[j] = c
        flags[j] = int(first) | (int(pos > 0) << 1) | (int(pos < n - 1) << 2)
        h0[j] = h0i
        so[j] = soi
    return [jnp.asarray(a) for a in (cidx, flags, h0, so)]


def _ssd_conv(flags, xc_ref, xp_ref, xn_ref, cw_ref, cb_ref, xpad):
    q = SSD_CHUNK
    xpad[0:HALO, :] = jnp.where(((flags >> 1) & 1) == 1, xp_ref[...], 0.0)
    xpad[HALO:HALO + q, :] = xc_ref[...]
    xpad[HALO + q:HALO + q + HALO, :] = jnp.where(((flags >> 2) & 1) == 1, xn_ref[...], 0.0)
    acc = jnp.zeros((q, SSD_XBC), F32) + cb_ref[...]
    pad = SSD_CONV_W // 2
    xp = xpad[...]
    rows = q + 2 * HALO
    for k in range(SSD_CONV_W):
        shifted = xp if k == pad else pltpu.roll(xp, (pad - k) % rows, axis=0)
        acc = acc + cw_ref[k:k + 1, :] * shifted[HALO:HALO + q, :]
    return _silu(acc)


def _ssd_scan_chunk(fwd, flags, xc, dt_ref, dtb_ref, alog_ref, dskip_ref, e_ref, h0_ref, y_ref, so_ref, state, lat_s):
    q = SSD_CHUNK
    ho = 0 if fwd else SSD_HEADS

    @pl.when((flags & 1) == 1)
    def _():
        state[...] = h0_ref[...]

    xs = xc[:, :D]
    bm = xc[:, D:D + LANES]
    cm = xc[:, D + LANES:D + 2 * LANES]
    dt = _softplus(dt_ref[...] + dtb_ref[...])
    da = dt * (-jnp.exp(alog_ref[...]))
    row = lax.broadcasted_iota(jnp.int32, (q, q), 0)
    col = lax.broadcasted_iota(jnp.int32, (q, q), 1)
    tri = (col <= row) if fwd else (col >= row)
    p = _dot(jnp.where(tri, 1.0, 0.0).astype(BF16), _split3(da))
    la = ((p[:, :LANES] + p[:, LANES:2 * LANES]) + p[:, 2 * LANES:]) * math.log2(math.e)
    lat_s[...] = la.T
    e3 = e_ref[...]
    la_exp = _dot(_split3(la), e3)
    dt_exp = _dot(_split3(dt), e3)
    la_end = la_exp[q - 1:q, :] if fwd else la_exp[0:1, :]
    decay_end = jnp.exp2(la_end - la_exp)
    chunk_decay = jnp.exp2(la_end)
    decay_in = jnp.exp2(la_exp)
    xdt = xs * dt_exp
    xdt_b = xdt.astype(BF16)
    xdtw_b = (xdt * decay_end).astype(BF16)
    bmt = bm.T

    half = LANES // 2
    lane = lax.broadcasted_iota(jnp.int32, (q, LANES), 1)
    hpg = SSD_HEADS // SSD_GROUPS
    st_all = state[...]
    st_new = []
    for g in range(SSD_GROUPS):
        c_g = cm[:, g * half:(g + 1) * half].astype(BF16)
        b_g = bm[:, g * half:(g + 1) * half].astype(BF16)
        bt_g = bmt[g * half:(g + 1) * half, :].astype(BF16)
        cb = _dot_nt(c_g, b_g)
        for pp in range(hpg // 2):
            h_a = g * hpg + 2 * pp
            sl = slice(h_a * SSD_HEAD_DIM, (h_a + 2) * SSD_HEAD_DIM)
            xdt_p = xdt_b[:, sl]
            yd = []
            for h in (ho + h_a, ho + h_a + 1):
                seg = la[:, h:h + 1] - lat_s[h:h + 1, :]
                s_h = (cb * jnp.exp2(jnp.where(tri, seg, NEG))).astype(BF16)
                yd.append(_dot(s_h, xdt_p))
            y_diag = jnp.where(lane < half, yd[0], yd[1])
            st_in = st_all[:, sl]
            y = y_diag + _dot(c_g, st_in.astype(BF16)) * decay_in[:, sl]
            st_new.append(st_in * chunk_decay[:, sl] + _dot(bt_g, xdtw_b[:, sl]))
            if fwd:
                y = y + xs[:, sl] * dskip_ref[:, sl]
            y_ref[:, sl] = y.astype(BF16)
    st_out = jnp.concatenate(st_new, axis=1)
    state[...] = st_out
    so_ref[...] = st_out


def _ssd_fwd_kernel(l_ref, cidx_ref, flags_ref, h0i_ref, soi_ref,
                    xc_ref, xp_ref, xn_ref, cw_ref, cb_ref, dt_ref, dtb_ref, alog_ref, dskip_ref, e_ref, h0_ref,
                    y_ref, so_ref, xco_ref, state, lat_s, xpad):
    flags = flags_ref[pl.program_id(0)]
    xc = _ssd_conv(flags, xc_ref, xp_ref, xn_ref, cw_ref, cb_ref, xpad)
    xco_ref[...] = xc
    _ssd_scan_chunk(True, flags, xc, dt_ref, dtb_ref, alog_ref, dskip_ref, e_ref, h0_ref, y_ref, so_ref, state, lat_s)


def _ssd_bwd_kernel(l_ref, cidx_ref, flags_ref, h0i_ref, soi_ref,
                    xc_ref, dt_ref, dtb_ref, alog_ref, e_ref, h0_ref, y_ref, so_ref, state, lat_s):
    flags = flags_ref[pl.program_id(0)]
    _ssd_scan_chunk(False, flags, xc_ref[...], dt_ref, dtb_ref, alog_ref, None, e_ref, h0_ref, y_ref, so_ref,
                    state, lat_s)


def _ssd_call(l_arr, tabs, xbc, dt, conv_w, conv_b, dt_bias, a_log, dskip, e_mat, h0t, xc_in, fwd):
    nb8 = T // HALO
    per = SSD_CHUNK // HALO
    cur = lambda j, l, ci, fl, h0, so: (ci[j], 0)
    prev = lambda j, l, ci, fl, h0, so: (jnp.maximum(ci[j] * per - 1, 0), 0)
    nxt = lambda j, l, ci, fl, h0, so: (jnp.minimum((ci[j] + 1) * per, nb8 - 1), 0)
    lay = lambda *shape: pl.BlockSpec((None,) + shape, lambda j, l, *_: (l[0],) + (0,) * len(shape))
    chunk = lambda n: pl.BlockSpec((SSD_CHUNK, n), cur)
    common_in = [chunk(LANES), lay(1, LANES), lay(1, LANES)]
    tail_in = [pl.BlockSpec((3 * LANES, D), lambda j, *_: (0, 0)),
               pl.BlockSpec((None, None, None, SSD_STATE, D),
                            lambda j, l, ci, fl, h0, so: (l[0], h0[j], 0 if fwd else 1, 0, 0))]
    out_specs = [chunk(D), pl.BlockSpec((None, SSD_STATE, D), lambda j, l, ci, fl, h0, so: (so[j], 0, 0))]
    out_shape = [jax.ShapeDtypeStruct((T, D), BF16), jax.ShapeDtypeStruct((N_PROMPT_SEQ + 1, SSD_STATE, D), F32)]
    scratch = [pltpu.VMEM((SSD_STATE, D), F32), pltpu.VMEM((LANES, SSD_CHUNK), F32)]
    if fwd:
        in_specs = ([chunk(SSD_XBC), pl.BlockSpec((HALO, SSD_XBC), prev), pl.BlockSpec((HALO, SSD_XBC), nxt),
                     lay(SSD_CONV_W, SSD_XBC), lay(1, SSD_XBC)] + common_in + [lay(1, D)] + tail_in)
        args = (xbc, xbc, xbc, conv_w, conv_b, dt, dt_bias, a_log, dskip, e_mat, h0t)
        out_specs.append(chunk(SSD_XBC))
        out_shape.append(jax.ShapeDtypeStruct((T, SSD_XBC), F32))
        scratch.append(pltpu.VMEM((SSD_CHUNK + 2 * HALO, SSD_XBC), F32))
    else:
        in_specs = [chunk(SSD_XBC)] + common_in + tail_in
        args = (xc_in, dt, dt_bias, a_log, e_mat, h0t)
    grid_spec = pltpu.PrefetchScalarGridSpec(num_scalar_prefetch=5, grid=(N_CHUNKS,), in_specs=in_specs,
                                             out_specs=out_specs, scratch_shapes=scratch)
    return pl.pallas_call(
        _ssd_fwd_kernel if fwd else _ssd_bwd_kernel,
        grid_spec=grid_spec,
        out_shape=out_shape,
        compiler_params=_cparams(("arbitrary",)),
        name="ssd_scan_fwd" if fwd else "ssd_scan_bwd",
    )(l_arr, *tabs, *args)


CV_HALO = 16


def _cv_kernel(l_ref, uc_ref, up_ref, un_ref, w_ref, b_ref, g_ref, beta_ref, o_ref, upad):
    i = pl.program_id(0)
    pos = (i - N_PROMPT_TILES) % TILES_PER_LATENT_SEQ
    is_prompt = i < N_PROMPT_TILES
    no_l = jnp.logical_or(is_prompt, pos == 0)
    no_r = jnp.logical_or(is_prompt, pos == TILES_PER_LATENT_SEQ - 1)
    upad[0:CV_HALO, :] = jnp.where(no_l, 0.0, up_ref[...])
    upad[CV_HALO:CV_HALO + TM, :] = uc_ref[...]
    upad[CV_HALO + TM:CV_HALO + TM + CV_HALO, :] = jnp.where(no_r, 0.0, un_ref[...])
    acc = jnp.zeros((TM, CONV_CH), F32) + b_ref[...]
    pad = CONV_W // 2
    up = upad[...]
    rows = TM + 2 * CV_HALO
    for rot in range(SUBLANES):
        taps = [k for k in range(CONV_W) if (CV_HALO - pad + k) % SUBLANES == rot]
        rolled = up if rot == 0 else pltpu.roll(up, rows - rot, axis=0)
        for k in taps:
            base = CV_HALO - pad + k - rot
            acc = acc + w_ref[k:k + 1, :] * rolled[base:base + TM, :]
    mu = jnp.mean(acc, axis=-1, keepdims=True)
    xc = acc - mu
    var = jnp.mean(xc * xc, axis=-1, keepdims=True)
    y = xc * lax.rsqrt(var + EPS) * g_ref[...] + beta_ref[...]
    o_ref[...] = _silu(y).astype(BF16)


def _cv_call(l_arr, u, w, b, g, beta):
    per = TM // CV_HALO
    nb = T // CV_HALO
    lay = lambda *shape: pl.BlockSpec((None,) + shape, lambda i, l: (l[0],) + (0,) * len(shape))
    grid_spec = pltpu.PrefetchScalarGridSpec(
        num_scalar_prefetch=1,
        grid=(N_TILES,),
        in_specs=[
            pl.BlockSpec((TM, CONV_CH), lambda i, l: (i, 0)),
            pl.BlockSpec((CV_HALO, CONV_CH), lambda i, l: (jnp.maximum(i * per - 1, 0), 0)),
            pl.BlockSpec((CV_HALO, CONV_CH), lambda i, l: (jnp.minimum((i + 1) * per, nb - 1), 0)),
            lay(CONV_W, CONV_CH), lay(1, CONV_CH), lay(1, CONV_CH), lay(1, CONV_CH),
        ],
        out_specs=pl.BlockSpec((TM, CONV_CH), lambda i, l: (i, 0)),
        scratch_shapes=[pltpu.VMEM((TM + 2 * CV_HALO, CONV_CH), F32)],
    )
    return pl.pallas_call(
        _cv_kernel,
        grid_spec=grid_spec,
        out_shape=jax.ShapeDtypeStruct((T, CONV_CH), BF16),
        compiler_params=_cparams(("arbitrary",)),
        name="conformer_conv",
    )(l_arr, u, u, u, w, b, g, beta)


def _lambda_terms(l_ref, lam_ref):
    lf = jnp.full((1, 1), l_ref[0], jnp.int32).astype(F32)
    lam_init = 0.8 - 0.6 * jnp.exp(-0.3 * lf)
    p = lam_ref[...]
    s1 = jnp.sum(p[0:1, :] * p[1:2, :], axis=-1, keepdims=True)
    s2 = jnp.sum(p[2:3, :] * p[3:4, :], axis=-1, keepdims=True)
    lam = jnp.exp(s1) - jnp.exp(s2) + lam_init
    return lam, 1.0 - lam_init


def _attn_body(l_ref, q_ref, k_ref, v_ref, ck_ref, cv_ref, lam_ref, g_ref, o_ref, k_s, v_s, n_ctx):
    @pl.when(pl.program_id(1) == 0)
    def _():
        k_s[n_ctx:, :] = k_ref[...].astype(BF16)
        for h in range(DA_HEADS):
            sl = slice(h * LANES, (h + 1) * LANES)
            if n_ctx:
                k_s[0:n_ctx, sl] = ck_ref[:, h, :].astype(BF16)
                v_s[0:n_ctx, 2 * h * LANES:(2 * h + 1) * LANES] = cv_ref[:, h, :].astype(BF16)
            v_s[n_ctx:, 2 * h * LANES:(2 * h + 1) * LANES] = v_ref[:, sl].astype(BF16)
            v_s[:, (2 * h + 1) * LANES:(2 * h + 2) * LANES] = jnp.ones((v_s.shape[0], LANES), BF16)

    lam, out_scale = _lambda_terms(l_ref, lam_ref)
    tq = q_ref.shape[0]
    lane = lax.broadcasted_iota(jnp.int32, (tq, LANES), 1)
    zero = jnp.zeros((tq, LANES), BF16)
    for h in range(DA_HEADS):
        sl = slice(h * LANES, (h + 1) * LANES)
        qh = q_ref[:, sl]
        kh = k_s[:, sl]
        vh = v_s[:, 2 * h * LANES:(2 * h + 2) * LANES]
        outs = []
        for c in range(2):
            in_c = (lane < DA_HEAD_DIM) if c == 0 else (lane >= DA_HEAD_DIM)
            s = _dot_nt(jnp.where(in_c, qh, zero), kh)
            m = jnp.max(s, axis=-1, keepdims=True)
            pv = _dot(jnp.exp2(s - m).astype(BF16), vh)
            outs.append(pv[:, :LANES] / pv[:, LANES:])
        o = outs[0] - lam * outs[1]
        o = o * lax.rsqrt(jnp.mean(o * o, axis=-1, keepdims=True) + EPS)
        o_ref[:, sl] = (o * g_ref[...] * out_scale).astype(BF16)


def _attn_prompt_kernel(l_ref, q_ref, k_ref, v_ref, lam_ref, g_ref, o_ref, k_s, v_s):
    _attn_body(l_ref, q_ref, k_ref, v_ref, None, None, lam_ref, g_ref, o_ref, k_s, v_s, 0)


def _attn_latent_kernel(l_ref, q_ref, k_ref, v_ref, ck_ref, cv_ref, lam_ref, g_ref, o_ref, k_s, v_s):
    _attn_body(l_ref, q_ref, k_ref, v_ref, ck_ref, cv_ref, lam_ref, g_ref, o_ref, k_s, v_s, PAST_LEN)


def _attn_call(l_arr, q, k, v, cache_k, cache_v, da_lambda, subln_g, latent):
    if latent:
        nseq, seqlen, n_ctx = N_LATENT_SEQ, LATENT_LEN, PAST_LEN
        tile0, seq0 = N_PROMPT_TILES, T_P // LATENT_LEN
    else:
        nseq, seqlen, n_ctx = N_PROMPT_SEQ, PROMPT_LEN, 0
        tile0, seq0 = 0, 0
    nq = seqlen // TM
    lay = lambda *shape: pl.BlockSpec((None,) + shape, lambda b, i, l: (l[0],) + (0,) * len(shape))
    in_specs = [
        pl.BlockSpec((TM, DA_QK), lambda b, i, l: (tile0 + b * nq + i, 0)),
        pl.BlockSpec((seqlen, DA_QK), lambda b, i, l: (seq0 + b, 0)),
        pl.BlockSpec((seqlen, DA_V), lambda b, i, l: (seq0 + b, 0)),
    ]
    args = [q, k, v]
    if latent:
        ctx = pl.BlockSpec((None, None, PAST_LEN, DA_HEADS, LANES), lambda b, i, l: (b, l[0], 0, 0, 0))
        in_specs += [ctx, ctx]
        args += [cache_k, cache_v]
    in_specs += [lay(4, DA_HEAD_DIM), lay(1, DA_V_DIM)]
    args += [da_lambda, subln_g]
    grid_spec = pltpu.PrefetchScalarGridSpec(
        num_scalar_prefetch=1,
        grid=(nseq, nq),
        in_specs=in_specs,
        out_specs=pl.BlockSpec((TM, DA_V), lambda b, i, l: (b * nq + i, 0)),
        scratch_shapes=[pltpu.VMEM((n_ctx + seqlen, DA_QK), BF16), pltpu.VMEM((n_ctx + seqlen, 2 * DA_V), BF16)],
    )
    return pl.pallas_call(
        _attn_latent_kernel if latent else _attn_prompt_kernel,
        grid_spec=grid_spec,
        out_shape=jax.ShapeDtypeStruct((nseq * seqlen, DA_V), BF16),
        compiler_params=_cparams(("arbitrary", "arbitrary")),
        name="diff_attn_latent" if latent else "diff_attn_prompt",
    )(l_arr, *args)


def _combine_kernel(l_ref, xp_ref, xl_ref, h_ref, yf_ref, yb_ref, z_ref, uc_ref, oap_ref, oal_ref, mod_ref,
                    wg_ref, bg_ref, sg_ref, wa_ref, wb_ref, wc_ref, wo_ref, g2_ref, wr_ref, br_ref,
                    x1_ref, h2_ref, route_ref, cnt_ref, e1_ref, e2_ref, r1_ref, r2_ref, carry):
    i = pl.program_id(0)

    @pl.when(i == 0)
    def _():
        carry[...] = jnp.zeros_like(carry)

    y = (yf_ref[...].astype(F32) + yb_ref[...].astype(F32)) * _silu(z_ref[...].astype(F32))
    y = y * lax.rsqrt(jnp.mean(y * y, axis=-1, keepdims=True) + EPS) * sg_ref[...]
    br_a = _dot(y.astype(BF16), wa_ref[...])
    br_b = _dot(uc_ref[...], wb_ref[...])
    br_c = _dot(_pair_read(i, oap_ref, oal_ref), wc_ref[...])
    hb = h_ref[...]

    def gate(n):
        return _sigmoid(_dot(hb, wg_ref[:, n * D:(n + 1) * D]) + bg_ref[:, n * D:(n + 1) * D])

    mix = gate(0) * br_a + gate(1) * br_b + gate(2) * br_c
    mixed = _dot(mix.astype(BF16), wo_ref[...])
    x1 = _pair_read(i, xp_ref, xl_ref) + mod_ref[2:3, :] * mixed
    x1_ref[...] = x1
    xn = x1 * lax.rsqrt(jnp.mean(x1 * x1, axis=-1, keepdims=True) + EPS)
    h2 = xn * g2_ref[...] * (1.0 + mod_ref[4:5, :]) + mod_ref[3:4, :]
    h2_ref[...] = h2

    h2_hi = h2.astype(BF16)
    h2_mid = (h2 - h2_hi.astype(F32)).astype(BF16)
    logits = _dot(jnp.concatenate([h2_hi, h2_mid, h2_hi], axis=1), wr_ref[...]) + br_ref[...]
    lane = lax.broadcasted_iota(jnp.int32, (TM, LANES), 1).astype(F32)

    def first_argmax(vals, vmax):
        return jnp.min(jnp.where(vals == vmax, lane, float(LANES)), axis=-1, keepdims=True)

    glog = jnp.where(lane < MOE_GROUPS, logits, NEG)
    gmax = jnp.max(glog, axis=-1, keepdims=True)
    gsel = first_argmax(glog, gmax)
    p_g = 1.0 / jnp.sum(jnp.exp(glog - gmax), axis=-1, keepdims=True)
    lo = ROUTER_LANE0 + MOE_EPG * gsel
    elog = jnp.where(jnp.logical_and(lane >= lo, lane < lo + MOE_EPG), logits, NEG)
    v1 = jnp.max(elog, axis=-1, keepdims=True)
    i1 = first_argmax(elog, v1)
    elog2 = jnp.where(lane == i1, NEG, elog)
    v2 = jnp.max(elog2, axis=-1, keepdims=True)
    i2 = first_argmax(elog2, v2)
    t2 = jnp.exp(v2 - v1)
    w1 = p_g / (1.0 + t2)
    w2 = p_g * t2 / (1.0 + t2)

    oh1 = lane == i1
    oh2 = lane == i2
    both = jnp.where(jnp.logical_or(oh1, oh2), 1.0, 0.0)
    r = lax.broadcasted_iota(jnp.int32, (TM, TM), 0)
    c = lax.broadcasted_iota(jnp.int32, (TM, TM), 1)
    strict_lower = jnp.where(c < r, 1.0, 0.0).astype(BF16)
    before = _dot(strict_lower, both.astype(BF16)) + carry[0:1, :]
    r1 = jnp.sum(jnp.where(oh1, before, 0.0), axis=-1, keepdims=True)
    r2 = jnp.sum(jnp.where(oh2, before, 0.0), axis=-1, keepdims=True)
    carry[...] = carry[...] + jnp.sum(both, axis=0, keepdims=True)
    cnt_ref[...] = carry[...]

    e1 = i1 - ROUTER_LANE0
    e2 = i2 - ROUTER_LANE0
    route = jnp.zeros((TM, LANES), F32)
    for n, val in enumerate((e1, e2, w1, w2, r1, r2)):
        route = jnp.where(lane == float(n), val, route)
    route_ref[...] = route
    route_t = route.T
    for n, ref in ((0, e1_ref), (1, e2_ref), (4, r1_ref), (5, r2_ref)):
        ref[...] = route_t[n:n + 1, :].astype(jnp.int32)


def _combine_call(l_arr, x_p, x_l, h, y_f, y_b, z, uc, oa_p, oa_l, mod6, w_gates, b_gates, ssd_norm_g, w_br_ssd,
                  w_br_conv, w_br_attn, w_out, norm2_g, w_router, b_router):
    tok = lambda n: pl.BlockSpec((TM, n), lambda i, l: (i, 0))
    lay = lambda *shape: pl.BlockSpec((None,) + shape, lambda i, l: (l[0],) + (0,) * len(shape))
    grid_spec = pltpu.PrefetchScalarGridSpec(
        num_scalar_prefetch=1,
        grid=(N_TILES,),
        in_specs=_pair_specs(D) + [tok(D), tok(D), tok(D), tok(D), tok(CONV_CH)] + _pair_specs(DA_V) + [
            pl.BlockSpec((None, None, 6, D), lambda i, l: (l[0], _mod_row(i), 0, 0)),
            lay(D, 3 * D), lay(1, 3 * D), lay(1, D), lay(D, D), lay(CONV_CH, D), lay(DA_V, D), lay(D, D),
            lay(1, D), lay(3 * D, LANES), lay(1, LANES),
        ],
        out_specs=[tok(D), tok(D), tok(LANES), pl.BlockSpec((SUBLANES, LANES), lambda i, l: (0, 0))]
        + [pl.BlockSpec((None, 1, TM), lambda i, l: (i, 0, 0))] * 4,
        scratch_shapes=[pltpu.VMEM((SUBLANES, LANES), F32)],
    )
    return pl.pallas_call(
        _combine_kernel,
        grid_spec=grid_spec,
        out_shape=[jax.ShapeDtypeStruct((T, D), F32), jax.ShapeDtypeStruct((T, D), F32),
                   jax.ShapeDtypeStruct((T, LANES), F32), jax.ShapeDtypeStruct((SUBLANES, LANES), F32)]
        + [jax.ShapeDtypeStruct((N_TILES, 1, TM), jnp.int32)] * 4,
        compiler_params=_cparams(("arbitrary",)),
        name="branch_combine_router",
    )(l_arr, x_p, x_l, h, y_f, y_b, z, uc, oa_p, oa_l, mod6, w_gates, b_gates, ssd_norm_g, w_br_ssd, w_br_conv,
      w_br_attn, w_out, norm2_g, w_router, b_router)


ROW_DMA_UNROLL = 64


def _dispatch_kernel(e1_ref, e2_ref, r1_ref, r2_ref, ps_ref, pe_ref, h2_ref, xs_ref, zbuf, sem, zsem):
    i = pl.program_id(0)
    base = i * TM

    @pl.when(i == 0)
    def _():
        zbuf[...] = jnp.zeros_like(zbuf)

        def zero_fill(e):
            off = pl.multiple_of(pe_ref[e] - MOE_BLOCK, MOE_BLOCK)
            return pltpu.make_async_copy(zbuf, xs_ref.at[pl.ds(off, MOE_BLOCK), :], zsem)

        def unused_fill(b):
            off = pl.multiple_of(b * MOE_BLOCK, MOE_BLOCK)
            return pltpu.make_async_copy(zbuf, xs_ref.at[pl.ds(off, MOE_BLOCK), :], zsem)

        n_used = pe_ref[MOE_E - 1] // MOE_BLOCK
        for e in range(MOE_E):
            @pl.when(pe_ref[e] > ps_ref[e])
            def _():
                zero_fill(e).start()
        lax.fori_loop(n_used, N_SLOT_BLOCKS, lambda b, c: (unused_fill(b).start(), c)[1], 0)
        for e in range(MOE_E):
            @pl.when(pe_ref[e] > ps_ref[e])
            def _():
                zero_fill(e).wait()
        lax.fori_loop(n_used, N_SLOT_BLOCKS, lambda b, c: (unused_fill(b).wait(), c)[1], 0)

    def issue(t, carry):
        for e_ref, r_ref in ((e1_ref, r1_ref), (e2_ref, r2_ref)):
            dest = ps_ref[e_ref[base + t]] + r_ref[base + t]
            pltpu.make_async_copy(h2_ref.at[pl.ds(t, 1), :], xs_ref.at[pl.ds(dest, 1), :], sem).start()
        return carry

    lax.fori_loop(0, TM, issue, 0, unroll=ROW_DMA_UNROLL)
    for _ in range(2):
        pltpu.make_async_copy(h2_ref, xs_ref.at[pl.ds(0, TM), :], sem).wait()


def _dispatch_call(e1, e2, r1, r2, pstart, pend, h2):
    grid_spec = pltpu.PrefetchScalarGridSpec(
        num_scalar_prefetch=6,
        grid=(N_TILES,),
        in_specs=[pl.BlockSpec((TM, D), lambda i, *_: (i, 0))],
        out_specs=pl.BlockSpec(memory_space=pl.ANY),
        scratch_shapes=[pltpu.VMEM((MOE_BLOCK, D), F32), pltpu.SemaphoreType.DMA(()), pltpu.SemaphoreType.DMA(())],
    )
    return pl.pallas_call(
        _dispatch_kernel,
        grid_spec=grid_spec,
        out_shape=jax.ShapeDtypeStruct((N_SLOTS, D), F32),
        compiler_params=_cparams(("arbitrary",)),
        name="moe_dispatch",
    )(e1, e2, r1, r2, pstart, pend, h2)


def _moe_kernel(l_ref, be_ref, nb_ref, xs_ref, wg_ref, wu_ref, wd_ref, ys_ref, wg_s, wu_s, wd_s):
    i = pl.program_id(0)
    prev = be_ref[jnp.maximum(i - 1, 0)]

    @pl.when(jnp.logical_or(i == 0, be_ref[i] != prev))
    def _():
        wg_s[...] = wg_ref[...].astype(BF16)
        wu_s[...] = wu_ref[...].astype(BF16)
        wd_s[...] = wd_ref[...].astype(BF16)

    @pl.when(i < nb_ref[0])
    def _():
        xb = xs_ref[...].astype(BF16)
        hid = _silu(_dot(xb, wg_s[...])) * _dot(xb, wu_s[...])
        ys_ref[...] = _dot(hid.astype(BF16), wd_s[...])

    @pl.when(i >= nb_ref[0])
    def _():
        ys_ref[...] = jnp.zeros_like(ys_ref)


def _moe_call(l_arr, block_expert, n_used, xs, w_gate, w_up, w_down):
    wspec = lambda a, b: pl.BlockSpec((None, None, a, b), lambda i, l, be, nb: (l[0], be[i], 0, 0))
    grid_spec = pltpu.PrefetchScalarGridSpec(
        num_scalar_prefetch=3,
        grid=(N_SLOT_BLOCKS,),
        in_specs=[pl.BlockSpec((MOE_BLOCK, D), lambda i, l, be, nb: (jnp.minimum(i, nb[0] - 1), 0)),
                  wspec(D, MOE_HIDDEN), wspec(D, MOE_HIDDEN), wspec(MOE_HIDDEN, D)],
        out_specs=pl.BlockSpec((MOE_BLOCK, D), lambda i, l, be, nb: (i, 0)),
        scratch_shapes=[pltpu.VMEM((D, MOE_HIDDEN), BF16), pltpu.VMEM((D, MOE_HIDDEN), BF16),
                        pltpu.VMEM((MOE_HIDDEN, D), BF16)],
    )
    return pl.pallas_call(
        _moe_kernel,
        grid_spec=grid_spec,
        out_shape=jax.ShapeDtypeStruct((N_SLOTS, D), F32),
        compiler_params=_cparams(("arbitrary",)),
        name="moe_experts",
    )(l_arr, block_expert, n_used, xs, w_gate, w_up, w_down)


def _moe_combine_kernel(l_ref, e1_ref, e2_ref, r1_ref, r2_ref, ps_ref, x1_ref, route_ref, mod_ref, fg_ref, ys_ref,
                        op_ref, ol_ref, buf, sem):
    i = pl.program_id(0)
    slot = i % 2

    def gather_tile(tile, dst_slot):
        base = tile * TM

        def issue(t, carry):
            for which, (e_ref, r_ref) in enumerate(((e1_ref, r1_ref), (e2_ref, r2_ref))):
                src = ps_ref[e_ref[base + t]] + r_ref[base + t]
                pltpu.make_async_copy(ys_ref.at[pl.ds(src, 1), :], buf.at[dst_slot, which, pl.ds(t, 1), :],
                                      sem.at[dst_slot]).start()
            return carry

        lax.fori_loop(0, TM, issue, 0, unroll=ROW_DMA_UNROLL)

    @pl.when(i == 0)
    def _():
        gather_tile(0, 0)

    @pl.when(i + 1 < N_TILES)
    def _():
        gather_tile(i + 1, 1 - slot)

    for which in range(2):
        pltpu.make_async_copy(ys_ref.at[pl.ds(0, TM), :], buf.at[slot, which], sem.at[slot]).wait()
    w1 = route_ref[:, 2:3]
    w2 = route_ref[:, 3:4]
    y = buf[slot, 0] * w1 + buf[slot, 1] * w2
    x2 = x1_ref[...] + mod_ref[5:6, :] * y
    x2 = x2 * lax.rsqrt(jnp.mean(x2 * x2, axis=-1, keepdims=True) + EPS) * fg_ref[...]

    @pl.when(i < N_PROMPT_TILES)
    def _():
        op_ref[...] = x2

    @pl.when(i >= N_PROMPT_TILES)
    def _():
        ol_ref[...] = x2


def _moe_combine_call(l_arr, e1, e2, r1, r2, pstart, x1, route, mod6, final_g, ys):
    grid_spec = pltpu.PrefetchScalarGridSpec(
        num_scalar_prefetch=6,
        grid=(N_TILES,),
        in_specs=[
            pl.BlockSpec((TM, D), lambda i, *_: (i, 0)),
            pl.BlockSpec((TM, LANES), lambda i, *_: (i, 0)),
            pl.BlockSpec((None, None, 6, D), lambda i, l, *_: (l[0], _mod_row(i), 0, 0)),
            pl.BlockSpec((1, D), lambda i, *_: (0, 0)),
            pl.BlockSpec(memory_space=pl.ANY),
        ],
        out_specs=_pair_specs(D),
        scratch_shapes=[pltpu.VMEM((2, 2, TM, D), F32), pltpu.SemaphoreType.DMA((2,))],
    )

    return pl.pallas_call(
        _moe_combine_kernel,
        grid_spec=grid_spec,
        out_shape=[jax.ShapeDtypeStruct((T_P, D), F32), jax.ShapeDtypeStruct((T_L, D), F32)],
        compiler_params=_cparams(("arbitrary",)),
        name="moe_combine_final",
    )(l_arr, e1, e2, r1, r2, pstart, x1, route, mod6, final_g, ys)


def _rope_tables():
    n = LATENT_LEN
    rows = n // GRID_W
    row = jnp.repeat(jnp.arange(rows), GRID_W).astype(F32)
    col = jnp.tile(jnp.arange(GRID_W), rows).astype(F32)
    axis_dim = DA_HEAD_DIM // 2
    inv_freq = 1.0 / (ROPE_BASE ** (jnp.arange(0, axis_dim, 2, dtype=F32) / axis_dim))
    ar, ac = row[:, None] * inv_freq, col[:, None] * inv_freq
    cos64 = jnp.concatenate([jnp.cos(ar), jnp.cos(ar), jnp.cos(ac), jnp.cos(ac)], axis=1)
    sin64 = jnp.concatenate([-jnp.sin(ar), jnp.sin(ar), -jnp.sin(ac), jnp.sin(ac)], axis=1)
    reps = DA_QK // DA_HEAD_DIM
    cos = jnp.concatenate([jnp.tile(cos64, (1, reps)), jnp.ones((TM, DA_QK), F32)], axis=0)
    sin = jnp.concatenate([jnp.tile(sin64, (1, reps)), jnp.zeros((TM, DA_QK), F32)], axis=0)
    return cos, sin


def _pad_lanes(a, n=LANES):
    return jnp.pad(a, [(0, 0)] * (a.ndim - 1) + [(0, n - a.shape[-1])])


def kernel(x_prompt, x_sample, cache_k, cache_v, state_ssd, c, c_ctx, w_ada, b_ada, norm1_g, norm2_g, w_in, b_in,
           ssd_conv_w, ssd_conv_b, ssd_dt_bias, ssd_a_log, ssd_d, ssd_norm_g, w_br_ssd, cv_dw_w, cv_dw_b, cv_ln_g,
           cv_ln_b, w_br_conv, da_lambda, da_subln_g, w_br_attn, w_out, moe_w_group, moe_b_group, moe_w_expert,
           moe_b_expert, moe_w_gate, moe_w_up, moe_w_down, final_g):
    L = DEPTH
    x_p, x_l = x_prompt.reshape(T_P, D), x_sample.reshape(T_L, D)

    cvec = jnp.concatenate([c, c_ctx[None, :], jnp.zeros((16 - N_LATENT_SEQ - 1, D), F32)], axis=0)
    mod6 = _ada_call(cvec, w_ada, b_ada).reshape(L, 16, 6, D)

    o_z, o_xbc, o_dt = 0, D, D + SSD_XBC
    o_glu = o_dt + 2 * SSD_HEADS
    o_q = o_glu + 2 * CONV_CH
    o_k, o_v, o_g = o_q + DA_QK, o_q + 2 * DA_QK, o_q + 2 * DA_QK + DA_V

    def regroup(w):
        return jnp.concatenate([w[..., o_z:o_dt], _pad_lanes(w[..., o_dt:o_glu]), w[..., o_glu:o_g]], axis=-1)

    w_proj = regroup(w_in).astype(BF16)
    b_proj = regroup(b_in).reshape(L, 1, N_PROJ)
    w_gates = w_in[..., o_g:].astype(BF16)
    b_gates = b_in[..., o_g:].reshape(L, 1, 3 * D)
    cos_tab, sin_tab = _rope_tables()

    tabs_f, tabs_b = _ssd_tables(True), _ssd_tables(False)
    dt_bias = _pad_lanes(ssd_dt_bias.reshape(L, 1, 2 * SSD_HEADS))
    a_log = _pad_lanes(ssd_a_log.reshape(L, 1, 2 * SSD_HEADS))
    dskip = jnp.repeat(ssd_d, SSD_HEAD_DIM, axis=-1).reshape(L, 1, D)
    hp = np.arange(D) // SSD_HEAD_DIM
    e_f, e_b = (jnp.asarray(np.tile(np.arange(LANES)[:, None] == ho + hp[None, :], (3, 1)).astype(np.float32))
                .astype(BF16) for ho in (0, SSD_HEADS))
    h0t = jnp.transpose(state_ssd, (1, 0, 2, 5, 3, 4)).reshape(L, N_LATENT_SEQ, 2, SSD_STATE, D)
    h0t = jnp.concatenate([h0t, jnp.zeros((L, 1, 2, SSD_STATE, D), F32)], axis=1)

    w_router = _pad_lanes(jnp.concatenate([moe_w_group, moe_w_expert], axis=-1))
    wr_hi = w_router.astype(BF16)
    wr_mid = (w_router - wr_hi.astype(F32)).astype(BF16)
    w_router = jnp.concatenate([wr_hi, wr_hi, wr_mid], axis=1)
    b_router = _pad_lanes(jnp.concatenate([moe_b_group, moe_b_expert], axis=-1)).reshape(L, 1, LANES)
    w_br_ssd_b, w_br_conv_b = w_br_ssd.astype(BF16), w_br_conv.astype(BF16)
    w_br_attn_b, w_out_b = w_br_attn.astype(BF16), w_out.astype(BF16)
    r3 = lambda a: a.reshape(L, 1, a.shape[-1])

    ks_new, vs_new, ss_new = [], [], []
    moe_prev = None
    for layer in range(L):
        l_arr = jnp.full((1,), layer, jnp.int32)
        if moe_prev is None:
            h, z, xbc, dt, u, q, k, v = _inproj_call(l_arr, x_p, x_l, mod6, r3(norm1_g), w_proj, b_proj,
                                                     cos_tab, sin_tab)
        else:
            h, z, xbc, dt, u, q, k, v, x_p, x_l = _inproj_call(l_arr, None, None, mod6, r3(norm1_g), w_proj, b_proj,
                                                               cos_tab, sin_tab, moe=moe_prev)
        y_f, st_f, xc = _ssd_call(l_arr, tabs_f, xbc, dt, ssd_conv_w, r3(ssd_conv_b), dt_bias, a_log, dskip,
                                  e_f, h0t, None, True)
        y_b, st_b = _ssd_call(l_arr, tabs_b, None, dt, None, None, dt_bias, a_log, None, e_b, h0t, xc, False)
        uc = _cv_call(l_arr, u, cv_dw_w, r3(cv_dw_b), r3(cv_ln_g), r3(cv_ln_b))
        oa_p = _attn_call(l_arr, q, k, v, None, None, da_lambda, r3(da_subln_g), latent=False)
        oa_l = _attn_call(l_arr, q, k, v, cache_k, cache_v, da_lambda, r3(da_subln_g), latent=True)
        x1, h2, route, counts, e1, e2, r1, r2 = _combine_call(
            l_arr, x_p, x_l, h, y_f, y_b, z, uc, oa_p, oa_l, mod6, w_gates, b_gates, r3(ssd_norm_g), w_br_ssd_b,
            w_br_conv_b, w_br_attn_b, w_out_b, r3(norm2_g), w_router, b_router)
        e1, e2, r1, r2 = (a.reshape(T) for a in (e1, e2, r1, r2))
        cnt = counts[0, ROUTER_LANE0:ROUTER_LANE0 + MOE_E].astype(jnp.int32)
        pcnt = (cnt + MOE_BLOCK - 1) // MOE_BLOCK * MOE_BLOCK
        pend = jnp.cumsum(pcnt)
        pstart = pend - pcnt
        blk0 = jnp.arange(N_SLOT_BLOCKS, dtype=jnp.int32) * MOE_BLOCK
        block_expert = jnp.minimum(jnp.sum((pend[None, :] <= blk0[:, None]).astype(jnp.int32), axis=1), MOE_E - 1)
        n_used = pend[-1:] // MOE_BLOCK

        xs = _dispatch_call(e1, e2, r1, r2, pstart, pend, h2)
        ys = _moe_call(l_arr, block_expert, n_used, xs, moe_w_gate, moe_w_up, moe_w_down)
        if layer == L - 1:
            x_p, x_l = _moe_combine_call(l_arr, e1, e2, r1, r2, pstart, x1, route, mod6, final_g.reshape(1, D), ys)
        else:
            moe_prev = (e1, e2, r1, r2, pstart, x1, route, ys)

        ks_new.append(k[:T_P].reshape(N_PROMPT_SEQ, PROMPT_LEN, DA_HEADS, 2 * DA_HEAD_DIM))
        vs_new.append(v[:T_P].reshape(N_PROMPT_SEQ, PROMPT_LEN, DA_HEADS, DA_V_DIM))
        s = jnp.stack([st_f[:N_PROMPT_SEQ], st_b[:N_PROMPT_SEQ]], axis=1)
        s = s.reshape(N_PROMPT_SEQ, 2, SSD_STATE, SSD_HEADS, SSD_HEAD_DIM)
        ss_new.append(jnp.transpose(s, (0, 1, 3, 4, 2)))

    y_prompt = x_p.reshape(N_PROMPT_SEQ, PROMPT_LEN, D)
    y_sample = x_l.reshape(N_LATENT_SEQ, LATENT_LEN, D)
    return (y_prompt, y_sample, jnp.stack(ks_new, axis=1), jnp.stack(vs_new, axis=1), jnp.stack(ss_new, axis=1))
```

```python
import functools
import math

import jax
import jax.numpy as jnp
import numpy as np
from jax import lax
from jax.experimental import pallas as pl
from jax.experimental.pallas import tpu as pltpu

D = 1024
DEPTH = 4
N_PROMPT_SEQ, PROMPT_LEN = 16, 256
N_LATENT_SEQ, LATENT_LEN = 8, 2048
PAST_LEN = 512
T_P = N_PROMPT_SEQ * PROMPT_LEN
T_L = N_LATENT_SEQ * LATENT_LEN
T = T_P + T_L
GRID_W = 64
EPS = 1e-6
SSD_HEADS, SSD_HEAD_DIM, SSD_STATE, SSD_GROUPS = 16, 64, 64, 2
SSD_CONV_W = 5
SSD_CHUNK = 128
SSD_XBC = D + 2 * SSD_GROUPS * SSD_STATE
CONV_CH, CONV_W = 512, 31
DA_HEADS, DA_HEAD_DIM, DA_V_DIM = 4, 64, 128
DA_QK = DA_HEADS * 2 * DA_HEAD_DIM
DA_V = DA_HEADS * DA_V_DIM
ROPE_BASE = 10000.0
MOE_GROUPS, MOE_EPG, MOE_E, MOE_HIDDEN = 4, 8, 32, 512
ROUTER_LANE0 = MOE_GROUPS

LANES = 128
SUBLANES = 8
TM = 256
N_TILES = T // TM
N_PROMPT_TILES = T_P // TM
TILES_PER_LATENT_SEQ = LATENT_LEN // TM
MOE_BLOCK = 256
N_SLOT_BLOCKS = (2 * T) // MOE_BLOCK + MOE_E
N_SLOTS = N_SLOT_BLOCKS * MOE_BLOCK
VMEM_LIMIT = 56 * 1024 * 1024

F32 = jnp.float32
BF16 = jnp.bfloat16
HI = lax.Precision.HIGHEST
NEG = -1e30


def _cparams(sem, vmem=VMEM_LIMIT):
    return pltpu.CompilerParams(dimension_semantics=sem, vmem_limit_bytes=vmem)


def _mod_row(i):
    return jnp.where(i < N_PROMPT_TILES, N_LATENT_SEQ, (i - N_PROMPT_TILES) // TILES_PER_LATENT_SEQ)


def _pair_specs(n, tile_arg=0):
    def prompt(*a):
        return (jnp.minimum(a[tile_arg], N_PROMPT_TILES - 1), 0)

    def latent(*a):
        return (jnp.maximum(a[tile_arg] - N_PROMPT_TILES, 0), 0)

    return [pl.BlockSpec((TM, n), prompt), pl.BlockSpec((TM, n), latent)]


def _pair_read(i, p_ref, l_ref):
    return jnp.where(i < N_PROMPT_TILES, p_ref[...], l_ref[...])


def _silu(x):
    return x * (1.0 / (1.0 + jnp.exp(-x)))


def _sigmoid(x):
    return 1.0 / (1.0 + jnp.exp(-x))


def _softplus(x):
    return jnp.maximum(x, 0.0) + jnp.log(1.0 + jnp.exp(-jnp.abs(x)))


def _dot(a, b, **kw):
    return jnp.dot(a, b, preferred_element_type=F32, **kw)


def _split3(x):
    hi = x.astype(BF16)
    r1 = x - hi.astype(F32)
    mid = r1.astype(BF16)
    lo = (r1 - mid.astype(F32)).astype(BF16)
    return jnp.concatenate([hi, mid, lo], axis=1)


def _dot_nt(a, b):
    return lax.dot_general(a, b, (((1,), (1,)), ((), ())), preferred_element_type=F32)


def _ada_kernel(c_ref, w_ref, b_ref, o_ref):
    cs = _silu(c_ref[...])
    o_ref[...] = _dot(cs, w_ref[...], precision=HI) + b_ref[...]


def _ada_call(cvec, w_ada, b_ada):
    nj = 6
    return pl.pallas_call(
        _ada_kernel,
        out_shape=jax.ShapeDtypeStruct((DEPTH, 16, 6 * D), F32),
        grid=(DEPTH, nj),
        in_specs=[
            pl.BlockSpec((16, D), lambda l, j: (0, 0)),
            pl.BlockSpec((None, D, D), lambda l, j: (l, 0, j)),
            pl.BlockSpec((None, 1, D), lambda l, j: (l, 0, j)),
        ],
        out_specs=pl.BlockSpec((None, 16, D), lambda l, j: (l, 0, j)),
        compiler_params=_cparams(("arbitrary", "arbitrary")),
        name="ada_mod",
    )(cvec, w_ada, b_ada.reshape(DEPTH, 1, 6 * D))


_C_Z = (0, D)
_C_XBC = (_C_Z[1], _C_Z[1] + SSD_XBC)
_C_DT = (_C_XBC[1], _C_XBC[1] + LANES)
_C_GLU = (_C_DT[1], _C_DT[1] + 2 * CONV_CH)
_C_Q = (_C_GLU[1], _C_GLU[1] + DA_QK)
_C_K = (_C_Q[1], _C_Q[1] + DA_QK)
_C_V = (_C_K[1], _C_K[1] + DA_V)
N_PROJ = _C_V[1]


def _swap16(x):
    cols = []
    for c in range(x.shape[1] // LANES):
        xc = x[:, c * LANES:(c + 1) * LANES]
        lane = lax.broadcasted_iota(jnp.int32, xc.shape, 1)
        from_right = pltpu.roll(xc, LANES - 16, axis=1)
        from_left = pltpu.roll(xc, 16, axis=1)
        cols.append(jnp.where((lane >> 4) % 2 == 0, from_right, from_left))
    return jnp.concatenate(cols, axis=1)


def _inproj_kernel(l_ref, xp_ref, xl_ref, *rest):
    _inproj_body(_pair_read(pl.program_id(0), xp_ref, xl_ref), *rest)


def _inproj_body(x, mod_ref, g_ref, w_ref, b_ref, cos_ref, sin_ref,
                 h_ref, z_ref, xbc_ref, dt_ref, u_ref, q_ref, k_ref, v_ref):
    xn = x * lax.rsqrt(jnp.mean(x * x, axis=-1, keepdims=True) + EPS)
    h = xn * g_ref[...] * (1.0 + mod_ref[1:2, :]) + mod_ref[0:1, :]
    hb = h.astype(BF16)
    h_ref[...] = hb

    def proj(c):
        return _dot(hb, w_ref[:, c[0]:c[1]]) + b_ref[:, c[0]:c[1]]

    z_ref[...] = proj(_C_Z).astype(BF16)
    xbc_ref[...] = proj(_C_XBC)
    dt_ref[...] = proj(_C_DT)
    glu = proj(_C_GLU)
    u_ref[...] = glu[:, :CONV_CH] * _sigmoid(glu[:, CONV_CH:])
    cos = cos_ref[...]
    sin = sin_ref[...]
    q = proj(_C_Q)
    q = q * cos + _swap16(q) * sin
    q_ref[...] = (q * (DA_HEAD_DIM ** -0.5 * math.log2(math.e))).astype(BF16)
    k = proj(_C_K)
    k_ref[...] = k * cos + _swap16(k) * sin
    v_ref[...] = proj(_C_V)


def _inproj_call(l_arr, x_p, x_l, mod6, norm1_g, w_proj, b_proj, cos_tab, sin_tab):
    tok = lambda n: pl.BlockSpec((TM, n), lambda i, l: (i, 0))
    rope_blk = lambda i, l: (jnp.where(i < N_PROMPT_TILES, TILES_PER_LATENT_SEQ,
                                       (i - N_PROMPT_TILES) % TILES_PER_LATENT_SEQ), 0)
    lay = lambda *shape: pl.BlockSpec((None,) + shape, lambda i, l: (l[0],) + (0,) * len(shape))
    grid_spec = pltpu.PrefetchScalarGridSpec(
        num_scalar_prefetch=1,
        grid=(N_TILES,),
        in_specs=_pair_specs(D) + [
            pl.BlockSpec((None, None, 6, D), lambda i, l: (l[0], _mod_row(i), 0, 0)),
            lay(1, D), lay(D, N_PROJ), lay(1, N_PROJ),
            pl.BlockSpec((TM, DA_QK), rope_blk), pl.BlockSpec((TM, DA_QK), rope_blk),
        ],
        out_specs=[tok(D), tok(D), tok(SSD_XBC), tok(LANES), tok(CONV_CH), tok(DA_QK), tok(DA_QK), tok(DA_V)],
    )
    sds = lambda n, dt: jax.ShapeDtypeStruct((T, n), dt)
    return pl.pallas_call(
        _inproj_kernel,
        grid_spec=grid_spec,
        out_shape=[sds(D, BF16), sds(D, BF16), sds(SSD_XBC, F32), sds(LANES, F32), sds(CONV_CH, F32),
                   sds(DA_QK, BF16), sds(DA_QK, F32), sds(DA_V, F32)],
        compiler_params=_cparams(("arbitrary",)),
        name="inproj",
    )(l_arr, x_p, x_l, mod6, norm1_g, w_proj, b_proj, cos_tab, sin_tab)


SSD_STEP = TM
CHUNKS_PER_STEP = SSD_STEP // SSD_CHUNK
N_SSD_STEPS = T // SSD_STEP
N_PROMPT_STEPS = T_P // SSD_STEP
STEPS_PER_PROMPT = PROMPT_LEN // SSD_STEP
STEPS_PER_LATENT = LATENT_LEN // SSD_STEP
HALO = SUBLANES
STATE_DUMP = N_PROMPT_SEQ


def _ssd_tables(fwd):
    cidx, flags, h0, so = (np.zeros((N_SSD_STEPS,), np.int32) for _ in range(4))
    for j in range(N_SSD_STEPS):
        c = j if fwd else N_SSD_STEPS - 1 - j
        if c < N_PROMPT_STEPS:
            seq, pos, n = c // STEPS_PER_PROMPT, c % STEPS_PER_PROMPT, STEPS_PER_PROMPT
            h0i, soi = N_LATENT_SEQ, seq
        else:
            cc = c - N_PROMPT_STEPS
            seq, pos, n = cc // STEPS_PER_LATENT, cc % STEPS_PER_LATENT, STEPS_PER_LATENT
            h0i, soi = seq, STATE_DUMP
        first = pos == 0 if fwd else pos == n - 1
        cidx[j] = c
        flags[j] = int(first) | (int(pos > 0) << 1) | (int(pos < n - 1) << 2)
        h0[j] = h0i
        so[j] = soi
    return [jnp.asarray(a) for a in (cidx, flags, h0, so)]


def _ssd_conv(flags, xc_ref, xp_ref, xn_ref, cw_ref, cb_ref, xpad):
    q = SSD_STEP
    xpad[0:HALO, :] = jnp.where(((flags >> 1) & 1) == 1, xp_ref[...], 0.0)
    xpad[HALO:HALO + q, :] = xc_ref[...]
    xpad[HALO + q:HALO + q + HALO, :] = jnp.where(((flags >> 2) & 1) == 1, xn_ref[...], 0.0)
    acc = jnp.zeros((q, SSD_XBC), F32) + cb_ref[...]
    pad = SSD_CONV_W // 2
    xp = xpad[...]
    rows = q + 2 * HALO
    for k in range(SSD_CONV_W):
        shifted = xp if k == pad else pltpu.roll(xp, (pad - k) % rows, axis=0)
        acc = acc + cw_ref[k:k + 1, :] * shifted[HALO:HALO + q, :]
    return _silu(acc)


def _ssd_scan_step(fwd, flags, xc, dt_ref, dtb_ref, alog_ref, dskip_ref, e_ref, h0_ref, y_ref, so_ref, state, lat_s):
    @pl.when((flags & 1) == 1)
    def _():
        state[...] = h0_ref[...]

    dt = _softplus(dt_ref[...] + dtb_ref[...])
    da = dt * (-jnp.exp(alog_ref[...]))
    st = state[...]
    for k in (range(CHUNKS_PER_STEP) if fwd else reversed(range(CHUNKS_PER_STEP))):
        rows = slice(k * SSD_CHUNK, (k + 1) * SSD_CHUNK)
        st = _ssd_scan_chunk(fwd, xc[rows], dt[rows], da[rows], st, dskip_ref, e_ref, y_ref, rows, lat_s.at[k])
    state[...] = st
    so_ref[...] = st


def _ssd_scan_chunk(fwd, xc, dt, da, st_all, dskip_ref, e_ref, y_ref, rows, lat_s):
    q = SSD_CHUNK
    ho = 0 if fwd else SSD_HEADS
    xs = xc[:, :D]
    bm = xc[:, D:D + LANES]
    cm = xc[:, D + LANES:D + 2 * LANES]
    row = lax.broadcasted_iota(jnp.int32, (q, q), 0)
    col = lax.broadcasted_iota(jnp.int32, (q, q), 1)
    tri = (col <= row) if fwd else (col >= row)
    p = _dot(jnp.where(tri, 1.0, 0.0).astype(BF16), _split3(da))
    la = ((p[:, :LANES] + p[:, LANES:2 * LANES]) + p[:, 2 * LANES:]) * math.log2(math.e)
    lat_s[...] = la.T
    e3 = e_ref[...]
    la_exp = _dot(_split3(la), e3)
    dt_exp = _dot(_split3(dt), e3)
    la_end = la_exp[q - 1:q, :] if fwd else la_exp[0:1, :]
    decay_end = jnp.exp2(la_end - la_exp)
    chunk_decay = jnp.exp2(la_end)
    decay_in = jnp.exp2(la_exp)
    xdt = xs * dt_exp
    xdt_b = xdt.astype(BF16)
    xdtw_b = (xdt * decay_end).astype(BF16)
    bmt = bm.T

    half = LANES // 2
    lane = lax.broadcasted_iota(jnp.int32, (q, LANES), 1)
    hpg = SSD_HEADS // SSD_GROUPS
    st_new = []
    for g in range(SSD_GROUPS):
        c_g = cm[:, g * half:(g + 1) * half].astype(BF16)
        b_g = bm[:, g * half:(g + 1) * half].astype(BF16)
        bt_g = bmt[g * half:(g + 1) * half, :].astype(BF16)
        cb = _dot_nt(c_g, b_g)
        for pp in range(hpg // 2):
            h_a = g * hpg + 2 * pp
            sl = slice(h_a * SSD_HEAD_DIM, (h_a + 2) * SSD_HEAD_DIM)
            xdt_p = xdt_b[:, sl]
            yd = []
            for h in (ho + h_a, ho + h_a + 1):
                seg = la[:, h:h + 1] - lat_s[h:h + 1, :]
                s_h = (cb * jnp.exp2(jnp.where(tri, seg, NEG))).astype(BF16)
                yd.append(_dot(s_h, xdt_p))
            y_diag = jnp.where(lane < half, yd[0], yd[1])
            st_in = st_all[:, sl]
            y = y_diag + _dot(c_g, st_in.astype(BF16)) * decay_in[:, sl]
            st_new.append(st_in * chunk_decay[:, sl] + _dot(bt_g, xdtw_b[:, sl]))
            if fwd:
                y = y + xs[:, sl] * dskip_ref[:, sl]
            y_ref[rows, sl] = y.astype(BF16)
    return jnp.concatenate(st_new, axis=1)


def _ssd_fwd_kernel(l_ref, cidx_ref, flags_ref, h0i_ref, soi_ref,
                    xc_ref, xp_ref, xn_ref, cw_ref, cb_ref, dt_ref, dtb_ref, alog_ref, dskip_ref, e_ref, h0_ref,
                    y_ref, so_ref, xco_ref, state, lat_s, xpad):
    flags = flags_ref[pl.program_id(0)]
    xc = _ssd_conv(flags, xc_ref, xp_ref, xn_ref, cw_ref, cb_ref, xpad)
    xco_ref[...] = xc
    _ssd_scan_step(True, flags, xc, dt_ref, dtb_ref, alog_ref, dskip_ref, e_ref, h0_ref, y_ref, so_ref, state, lat_s)


def _ssd_bwd_kernel(l_ref, cidx_ref, flags_ref, h0i_ref, soi_ref,
                    xc_ref, dt_ref, dtb_ref, alog_ref, e_ref, h0_ref, y_ref, so_ref, state, lat_s):
    flags = flags_ref[pl.program_id(0)]
    _ssd_scan_step(False, flags, xc_ref[...], dt_ref, dtb_ref, alog_ref, None, e_ref, h0_ref, y_ref, so_ref,
                   state, lat_s)


def _ssd_call(l_arr, tabs, xbc, dt, conv_w, conv_b, dt_bias, a_log, dskip, e_mat, h0t, xc_in, fwd):
    nb8 = T // HALO
    per = SSD_STEP // HALO
    cur = lambda j, l, ci, fl, h0, so: (ci[j], 0)
    prev = lambda j, l, ci, fl, h0, so: (jnp.maximum(ci[j] * per - 1, 0), 0)
    nxt = lambda j, l, ci, fl, h0, so: (jnp.minimum((ci[j] + 1) * per, nb8 - 1), 0)
    lay = lambda *shape: pl.BlockSpec((None,) + shape, lambda j, l, *_: (l[0],) + (0,) * len(shape))
    chunk = lambda n: pl.BlockSpec((SSD_STEP, n), cur)
    common_in = [chunk(LANES), lay(1, LANES), lay(1, LANES)]
    tail_in = [pl.BlockSpec((3 * LANES, D), lambda j, *_: (0, 0)),
               pl.BlockSpec((None, None, None, SSD_STATE, D),
                            lambda j, l, ci, fl, h0, so: (l[0], h0[j], 0 if fwd else 1, 0, 0))]
    out_specs = [chunk(D), pl.BlockSpec((None, SSD_STATE, D), lambda j, l, ci, fl, h0, so: (so[j], 0, 0))]
    out_shape = [jax.ShapeDtypeStruct((T, D), BF16), jax.ShapeDtypeStruct((N_PROMPT_SEQ + 1, SSD_STATE, D), F32)]
    scratch = [pltpu.VMEM((SSD_STATE, D), F32), pltpu.VMEM((CHUNKS_PER_STEP, LANES, SSD_CHUNK), F32)]
    if fwd:
        in_specs = ([chunk(SSD_XBC), pl.BlockSpec((HALO, SSD_XBC), prev), pl.BlockSpec((HALO, SSD_XBC), nxt),
                     lay(SSD_CONV_W, SSD_XBC), lay(1, SSD_XBC)] + common_in + [lay(1, D)] + tail_in)
        args = (xbc, xbc, xbc, conv_w, conv_b, dt, dt_bias, a_log, dskip, e_mat, h0t)
        out_specs.append(chunk(SSD_XBC))
        out_shape.append(jax.ShapeDtypeStruct((T, SSD_XBC), F32))
        scratch.append(pltpu.VMEM((SSD_STEP + 2 * HALO, SSD_XBC), F32))
    else:
        in_specs = [chunk(SSD_XBC)] + common_in + tail_in
        args = (xc_in, dt, dt_bias, a_log, e_mat, h0t)
    grid_spec = pltpu.PrefetchScalarGridSpec(num_scalar_prefetch=5, grid=(N_SSD_STEPS,), in_specs=in_specs,
                                             out_specs=out_specs, scratch_shapes=scratch)
    return pl.pallas_call(
        _ssd_fwd_kernel if fwd else _ssd_bwd_kernel,
        grid_spec=grid_spec,
        out_shape=out_shape,
        compiler_params=_cparams(("arbitrary",)),
        name="ssd_scan_fwd" if fwd else "ssd_scan_bwd",
    )(l_arr, *tabs, *args)


CV_HALO = 16


def _cv_kernel(l_ref, uc_ref, up_ref, un_ref, w_ref, b_ref, g_ref, beta_ref, o_ref, upad):
    i = pl.program_id(0)
    pos = (i - N_PROMPT_TILES) % TILES_PER_LATENT_SEQ
    is_prompt = i < N_PROMPT_TILES
    no_l = jnp.logical_or(is_prompt, pos == 0)
    no_r = jnp.logical_or(is_prompt, pos == TILES_PER_LATENT_SEQ - 1)
    upad[0:CV_HALO, :] = jnp.where(no_l, 0.0, up_ref[...])
    upad[CV_HALO:CV_HALO + TM, :] = uc_ref[...]
    upad[CV_HALO + TM:CV_HALO + TM + CV_HALO, :] = jnp.where(no_r, 0.0, un_ref[...])
    acc = jnp.zeros((TM, CONV_CH), F32) + b_ref[...]
    pad = CONV_W // 2
    up = upad[...]
    rows = TM + 2 * CV_HALO
    for rot in range(SUBLANES):
        taps = [k for k in range(CONV_W) if (CV_HALO - pad + k) % SUBLANES == rot]
        rolled = up if rot == 0 else pltpu.roll(up, rows - rot, axis=0)
        for k in taps:
            base = CV_HALO - pad + k - rot
            acc = acc + w_ref[k:k + 1, :] * rolled[base:base + TM, :]
    mu = jnp.mean(acc, axis=-1, keepdims=True)
    xc = acc - mu
    var = jnp.mean(xc * xc, axis=-1, keepdims=True)
    y = xc * lax.rsqrt(var + EPS) * g_ref[...] + beta_ref[...]
    o_ref[...] = _silu(y).astype(BF16)


def _cv_call(l_arr, u, w, b, g, beta):
    per = TM // CV_HALO
    nb = T // CV_HALO
    lay = lambda *shape: pl.BlockSpec((None,) + shape, lambda i, l: (l[0],) + (0,) * len(shape))
    grid_spec = pltpu.PrefetchScalarGridSpec(
        num_scalar_prefetch=1,
        grid=(N_TILES,),
        in_specs=[
            pl.BlockSpec((TM, CONV_CH), lambda i, l: (i, 0)),
            pl.BlockSpec((CV_HALO, CONV_CH), lambda i, l: (jnp.maximum(i * per - 1, 0), 0)),
            pl.BlockSpec((CV_HALO, CONV_CH), lambda i, l: (jnp.minimum((i + 1) * per, nb - 1), 0)),
            lay(CONV_W, CONV_CH), lay(1, CONV_CH), lay(1, CONV_CH), lay(1, CONV_CH),
        ],
        out_specs=pl.BlockSpec((TM, CONV_CH), lambda i, l: (i, 0)),
        scratch_shapes=[pltpu.VMEM((TM + 2 * CV_HALO, CONV_CH), F32)],
    )
    return pl.pallas_call(
        _cv_kernel,
        grid_spec=grid_spec,
        out_shape=jax.ShapeDtypeStruct((T, CONV_CH), BF16),
        compiler_params=_cparams(("arbitrary",)),
        name="conformer_conv",
    )(l_arr, u, u, u, w, b, g, beta)


def _lambda_terms(l_ref, lam_ref):
    lf = jnp.full((1, 1), l_ref[0], jnp.int32).astype(F32)
    lam_init = 0.8 - 0.6 * jnp.exp(-0.3 * lf)
    p = lam_ref[...]
    s1 = jnp.sum(p[0:1, :] * p[1:2, :], axis=-1, keepdims=True)
    s2 = jnp.sum(p[2:3, :] * p[3:4, :], axis=-1, keepdims=True)
    lam = jnp.exp(s1) - jnp.exp(s2) + lam_init
    return lam, 1.0 - lam_init


def _attn_body(l_ref, q_ref, k_ref, v_ref, ck_ref, cv_ref, lam_ref, g_ref, o_ref, k_s, v_s, n_ctx):
    @pl.when(pl.program_id(1) == 0)
    def _():
        k_s[n_ctx:, :] = k_ref[...].astype(BF16)
        for h in range(DA_HEADS):
            sl = slice(h * LANES, (h + 1) * LANES)
            if n_ctx:
                k_s[0:n_ctx, sl] = ck_ref[:, h, :].astype(BF16)
                v_s[0:n_ctx, 2 * h * LANES:(2 * h + 1) * LANES] = cv_ref[:, h, :].astype(BF16)
            v_s[n_ctx:, 2 * h * LANES:(2 * h + 1) * LANES] = v_ref[:, sl].astype(BF16)
            v_s[:, (2 * h + 1) * LANES:(2 * h + 2) * LANES] = jnp.ones((v_s.shape[0], LANES), BF16)

    lam, out_scale = _lambda_terms(l_ref, lam_ref)
    tq = q_ref.shape[0]
    lane = lax.broadcasted_iota(jnp.int32, (tq, LANES), 1)
    zero = jnp.zeros((tq, LANES), BF16)
    for h in range(DA_HEADS):
        sl = slice(h * LANES, (h + 1) * LANES)
        qh = q_ref[:, sl]
        kh = k_s[:, sl]
        vh = v_s[:, 2 * h * LANES:(2 * h + 2) * LANES]
        outs = []
        for c in range(2):
            in_c = (lane < DA_HEAD_DIM) if c == 0 else (lane >= DA_HEAD_DIM)
            s = _dot_nt(jnp.where(in_c, qh, zero), kh)
            m = jnp.max(s, axis=-1, keepdims=True)
            pv = _dot(jnp.exp2(s - m).astype(BF16), vh)
            outs.append(pv[:, :LANES] / pv[:, LANES:])
        o = outs[0] - lam * outs[1]
        o = o * lax.rsqrt(jnp.mean(o * o, axis=-1, keepdims=True) + EPS)
        o_ref[:, sl] = (o * g_ref[...] * out_scale).astype(BF16)


def _attn_prompt_kernel(l_ref, q_ref, k_ref, v_ref, lam_ref, g_ref, o_ref, k_s, v_s):
    _attn_body(l_ref, q_ref, k_ref, v_ref, None, None, lam_ref, g_ref, o_ref, k_s, v_s, 0)


def _attn_latent_kernel(l_ref, q_ref, k_ref, v_ref, ck_ref, cv_ref, lam_ref, g_ref, o_ref, k_s, v_s):
    _attn_body(l_ref, q_ref, k_ref, v_ref, ck_ref, cv_ref, lam_ref, g_ref, o_ref, k_s, v_s, PAST_LEN)


def _attn_call(l_arr, q, k, v, cache_k, cache_v, da_lambda, subln_g, latent):
    if latent:
        nseq, seqlen, n_ctx = N_LATENT_SEQ, LATENT_LEN, PAST_LEN
        tile0, seq0 = N_PROMPT_TILES, T_P // LATENT_LEN
    else:
        nseq, seqlen, n_ctx = N_PROMPT_SEQ, PROMPT_LEN, 0
        tile0, seq0 = 0, 0
    nq = seqlen // TM
    lay = lambda *shape: pl.BlockSpec((None,) + shape, lambda b, i, l: (l[0],) + (0,) * len(shape))
    in_specs = [
        pl.BlockSpec((TM, DA_QK), lambda b, i, l: (tile0 + b * nq + i, 0)),
        pl.BlockSpec((seqlen, DA_QK), lambda b, i, l: (seq0 + b, 0)),
        pl.BlockSpec((seqlen, DA_V), lambda b, i, l: (seq0 + b, 0)),
    ]
    args = [q, k, v]
    if latent:
        ctx = pl.BlockSpec((None, None, PAST_LEN, DA_HEADS, LANES), lambda b, i, l: (b, l[0], 0, 0, 0))
        in_specs += [ctx, ctx]
        args += [cache_k, cache_v]
    in_specs += [lay(4, DA_HEAD_DIM), lay(1, DA_V_DIM)]
    args += [da_lambda, subln_g]
    grid_spec = pltpu.PrefetchScalarGridSpec(
        num_scalar_prefetch=1,
        grid=(nseq, nq),
        in_specs=in_specs,
        out_specs=pl.BlockSpec((TM, DA_V), lambda b, i, l: (b * nq + i, 0)),
        scratch_shapes=[pltpu.VMEM((n_ctx + seqlen, DA_QK), BF16), pltpu.VMEM((n_ctx + seqlen, 2 * DA_V), BF16)],
    )
    return pl.pallas_call(
        _attn_latent_kernel if latent else _attn_prompt_kernel,
        grid_spec=grid_spec,
        out_shape=jax.ShapeDtypeStruct((nseq * seqlen, DA_V), BF16),
        compiler_params=_cparams(("arbitrary", "arbitrary")),
        name="diff_attn_latent" if latent else "diff_attn_prompt",
    )(l_arr, *args)


def _combine_kernel(l_ref, xp_ref, xl_ref, h_ref, yf_ref, yb_ref, z_ref, uc_ref, oap_ref, oal_ref, mod_ref,
                    wg_ref, bg_ref, sg_ref, wa_ref, wb_ref, wc_ref, wo_ref, g2_ref, wr_ref, br_ref,
                    x1_ref, h2_ref, route_ref, cnt_ref, e1_ref, e2_ref, r1_ref, r2_ref, carry):
    i = pl.program_id(0)

    @pl.when(i == 0)
    def _():
        carry[...] = jnp.zeros_like(carry)

    y = (yf_ref[...].astype(F32) + yb_ref[...].astype(F32)) * _silu(z_ref[...].astype(F32))
    y = y * lax.rsqrt(jnp.mean(y * y, axis=-1, keepdims=True) + EPS) * sg_ref[...]
    br_a = _dot(y.astype(BF16), wa_ref[...])
    br_b = _dot(uc_ref[...], wb_ref[...])
    br_c = _dot(_pair_read(i, oap_ref, oal_ref), wc_ref[...])
    hb = h_ref[...]

    def gate(n):
        return _sigmoid(_dot(hb, wg_ref[:, n * D:(n + 1) * D]) + bg_ref[:, n * D:(n + 1) * D])

    mix = gate(0) * br_a + gate(1) * br_b + gate(2) * br_c
    mixed = _dot(mix.astype(BF16), wo_ref[...])
    x1 = _pair_read(i, xp_ref, xl_ref) + mod_ref[2:3, :] * mixed
    x1_ref[...] = x1
    xn = x1 * lax.rsqrt(jnp.mean(x1 * x1, axis=-1, keepdims=True) + EPS)
    h2 = xn * g2_ref[...] * (1.0 + mod_ref[4:5, :]) + mod_ref[3:4, :]
    h2_ref[...] = h2

    h2_hi = h2.astype(BF16)
    h2_mid = (h2 - h2_hi.astype(F32)).astype(BF16)
    logits = _dot(jnp.concatenate([h2_hi, h2_mid, h2_hi], axis=1), wr_ref[...]) + br_ref[...]
    lane = lax.broadcasted_iota(jnp.int32, (TM, LANES), 1).astype(F32)

    def first_argmax(vals, vmax):
        return jnp.min(jnp.where(vals == vmax, lane, float(LANES)), axis=-1, keepdims=True)

    glog = jnp.where(lane < MOE_GROUPS, logits, NEG)
    gmax = jnp.max(glog, axis=-1, keepdims=True)
    gsel = first_argmax(glog, gmax)
    p_g = 1.0 / jnp.sum(jnp.exp(glog - gmax), axis=-1, keepdims=True)
    lo = ROUTER_LANE0 + MOE_EPG * gsel
    elog = jnp.where(jnp.logical_and(lane >= lo, lane < lo + MOE_EPG), logits, NEG)
    v1 = jnp.max(elog, axis=-1, keepdims=True)
    i1 = first_argmax(elog, v1)
    elog2 = jnp.where(lane == i1, NEG, elog)
    v2 = jnp.max(elog2, axis=-1, keepdims=True)
    i2 = first_argmax(elog2, v2)
    t2 = jnp.exp(v2 - v1)
    w1 = p_g / (1.0 + t2)
    w2 = p_g * t2 / (1.0 + t2)

    oh1 = lane == i1
    oh2 = lane == i2
    both = jnp.where(jnp.logical_or(oh1, oh2), 1.0, 0.0)
    r = lax.broadcasted_iota(jnp.int32, (TM, TM), 0)
    c = lax.broadcasted_iota(jnp.int32, (TM, TM), 1)
    strict_lower = jnp.where(c < r, 1.0, 0.0).astype(BF16)
    before = _dot(strict_lower, both.astype(BF16)) + carry[0:1, :]
    r1 = jnp.sum(jnp.where(oh1, before, 0.0), axis=-1, keepdims=True)
    r2 = jnp.sum(jnp.where(oh2, before, 0.0), axis=-1, keepdims=True)
    carry[...] = carry[...] + jnp.sum(both, axis=0, keepdims=True)
    cnt_ref[...] = carry[...]

    e1 = i1 - ROUTER_LANE0
    e2 = i2 - ROUTER_LANE0
    route = jnp.zeros((TM, LANES), F32)
    for n, val in enumerate((e1, e2, w1, w2, r1, r2)):
        route = jnp.where(lane == float(n), val, route)
    route_ref[...] = route
    route_t = route.T
    for n, ref in ((0, e1_ref), (1, e2_ref), (4, r1_ref), (5, r2_ref)):
        ref[...] = route_t[n:n + 1, :].astype(jnp.int32)


def _combine_call(l_arr, x_p, x_l, h, y_f, y_b, z, uc, oa_p, oa_l, mod6, w_gates, b_gates, ssd_norm_g, w_br_ssd,
                  w_br_conv, w_br_attn, w_out, norm2_g, w_router, b_router):
    tok = lambda n: pl.BlockSpec((TM, n), lambda i, l: (i, 0))
    lay = lambda *shape: pl.BlockSpec((None,) + shape, lambda i, l: (l[0],) + (0,) * len(shape))
    grid_spec = pltpu.PrefetchScalarGridSpec(
        num_scalar_prefetch=1,
        grid=(N_TILES,),
        in_specs=_pair_specs(D) + [tok(D), tok(D), tok(D), tok(D), tok(CONV_CH)] + _pair_specs(DA_V) + [
            pl.BlockSpec((None, None, 6, D), lambda i, l: (l[0], _mod_row(i), 0, 0)),
            lay(D, 3 * D), lay(1, 3 * D), lay(1, D), lay(D, D), lay(CONV_CH, D), lay(DA_V, D), lay(D, D),
            lay(1, D), lay(3 * D, LANES), lay(1, LANES),
        ],
        out_specs=[tok(D), tok(D), tok(LANES), pl.BlockSpec((SUBLANES, LANES), lambda i, l: (0, 0))]
        + [pl.BlockSpec((None, 1, TM), lambda i, l: (i, 0, 0))] * 4,
        scratch_shapes=[pltpu.VMEM((SUBLANES, LANES), F32)],
    )
    return pl.pallas_call(
        _combine_kernel,
        grid_spec=grid_spec,
        out_shape=[jax.ShapeDtypeStruct((T, D), F32), jax.ShapeDtypeStruct((T, D), F32),
                   jax.ShapeDtypeStruct((T, LANES), F32), jax.ShapeDtypeStruct((SUBLANES, LANES), F32)]
        + [jax.ShapeDtypeStruct((N_TILES, 1, TM), jnp.int32)] * 4,
        compiler_params=_cparams(("arbitrary",)),
        name="branch_combine_router",
    )(l_arr, x_p, x_l, h, y_f, y_b, z, uc, oa_p, oa_l, mod6, w_gates, b_gates, ssd_norm_g, w_br_ssd, w_br_conv,
      w_br_attn, w_out, norm2_g, w_router, b_router)


ROW_DMA_UNROLL = 64


def _dispatch_kernel(e1_ref, e2_ref, r1_ref, r2_ref, ps_ref, pe_ref, h2_ref, xs_ref, zbuf, sem, zsem):
    i = pl.program_id(0)
    base = i * TM

    @pl.when(i == 0)
    def _():
        zbuf[...] = jnp.zeros_like(zbuf)

        def zero_fill(e):
            off = pl.multiple_of(pe_ref[e] - MOE_BLOCK, MOE_BLOCK)
            return pltpu.make_async_copy(zbuf, xs_ref.at[pl.ds(off, MOE_BLOCK), :], zsem)

        def unused_fill(b):
            off = pl.multiple_of(b * MOE_BLOCK, MOE_BLOCK)
            return pltpu.make_async_copy(zbuf, xs_ref.at[pl.ds(off, MOE_BLOCK), :], zsem)

        n_used = pe_ref[MOE_E - 1] // MOE_BLOCK
        for e in range(MOE_E):
            @pl.when(pe_ref[e] > ps_ref[e])
            def _():
                zero_fill(e).start()
        lax.fori_loop(n_used, N_SLOT_BLOCKS, lambda b, c: (unused_fill(b).start(), c)[1], 0)
        for e in range(MOE_E):
            @pl.when(pe_ref[e] > ps_ref[e])
            def _():
                zero_fill(e).wait()
        lax.fori_loop(n_used, N_SLOT_BLOCKS, lambda b, c: (unused_fill(b).wait(), c)[1], 0)

    def issue(t, carry):
        for e_ref, r_ref in ((e1_ref, r1_ref), (e2_ref, r2_ref)):
            dest = ps_ref[e_ref[base + t]] + r_ref[base + t]
            pltpu.make_async_copy(h2_ref.at[pl.ds(t, 1), :], xs_ref.at[pl.ds(dest, 1), :], sem).start()
        return carry

    lax.fori_loop(0, TM, issue, 0, unroll=ROW_DMA_UNROLL)
    for _ in range(2):
        pltpu.make_async_copy(h2_ref, xs_ref.at[pl.ds(0, TM), :], sem).wait()


def _dispatch_call(e1, e2, r1, r2, pstart, pend, h2):
    grid_spec = pltpu.PrefetchScalarGridSpec(
        num_scalar_prefetch=6,
        grid=(N_TILES,),
        in_specs=[pl.BlockSpec((TM, D), lambda i, *_: (i, 0))],
        out_specs=pl.BlockSpec(memory_space=pl.ANY),
        scratch_shapes=[pltpu.VMEM((MOE_BLOCK, D), F32), pltpu.SemaphoreType.DMA(()), pltpu.SemaphoreType.DMA(())],
    )
    return pl.pallas_call(
        _dispatch_kernel,
        grid_spec=grid_spec,
        out_shape=jax.ShapeDtypeStruct((N_SLOTS, D), F32),
        compiler_params=_cparams(("arbitrary",)),
        name="moe_dispatch",
    )(e1, e2, r1, r2, pstart, pend, h2)


def _moe_kernel(l_ref, be_ref, nb_ref, xs_ref, wg_ref, wu_ref, wd_ref, ys_ref, wg_s, wu_s, wd_s):
    i = pl.program_id(0)
    prev = be_ref[jnp.maximum(i - 1, 0)]

    @pl.when(jnp.logical_or(i == 0, be_ref[i] != prev))
    def _():
        wg_s[...] = wg_ref[...].astype(BF16)
        wu_s[...] = wu_ref[...].astype(BF16)
        wd_s[...] = wd_ref[...].astype(BF16)

    @pl.when(i < nb_ref[0])
    def _():
        xb = xs_ref[...].astype(BF16)
        hid = _silu(_dot(xb, wg_s[...])) * _dot(xb, wu_s[...])
        ys_ref[...] = _dot(hid.astype(BF16), wd_s[...])

    @pl.when(i >= nb_ref[0])
    def _():
        ys_ref[...] = jnp.zeros_like(ys_ref)


def _moe_call(l_arr, block_expert, n_used, xs, w_gate, w_up, w_down):
    wspec = lambda a, b: pl.BlockSpec((None, None, a, b), lambda i, l, be, nb: (l[0], be[i], 0, 0))
    grid_spec = pltpu.PrefetchScalarGridSpec(
        num_scalar_prefetch=3,
        grid=(N_SLOT_BLOCKS,),
        in_specs=[pl.BlockSpec((MOE_BLOCK, D), lambda i, l, be, nb: (jnp.minimum(i, nb[0] - 1), 0)),
                  wspec(D, MOE_HIDDEN), wspec(D, MOE_HIDDEN), wspec(MOE_HIDDEN, D)],
        out_specs=pl.BlockSpec((MOE_BLOCK, D), lambda i, l, be, nb: (i, 0)),
        scratch_shapes=[pltpu.VMEM((D, MOE_HIDDEN), BF16), pltpu.VMEM((D, MOE_HIDDEN), BF16),
                        pltpu.VMEM((MOE_HIDDEN, D), BF16)],
    )
    return pl.pallas_call(
        _moe_kernel,
        grid_spec=grid_spec,
        out_shape=jax.ShapeDtypeStruct((N_SLOTS, D), F32),
        compiler_params=_cparams(("arbitrary",)),
        name="moe_experts",
    )(l_arr, block_expert, n_used, xs, w_gate, w_up, w_down)


def _moe_combine_kernel(l_ref, e1_ref, e2_ref, r1_ref, r2_ref, ps_ref, x1_ref, route_ref, mod_ref, fg_ref, ys_ref,
                        op_ref, ol_ref, buf, sem, *, final):
    i = pl.program_id(0)
    slot = i % 2

    def gather_tile(tile, dst_slot):
        base = tile * TM

        def issue(t, carry):
            for which, (e_ref, r_ref) in enumerate(((e1_ref, r1_ref), (e2_ref, r2_ref))):
                src = ps_ref[e_ref[base + t]] + r_ref[base + t]
                pltpu.make_async_copy(ys_ref.at[pl.ds(src, 1), :], buf.at[dst_slot, which, pl.ds(t, 1), :],
                                      sem.at[dst_slot]).start()
            return carry

        lax.fori_loop(0, TM, issue, 0, unroll=ROW_DMA_UNROLL)

    @pl.when(i == 0)
    def _():
        gather_tile(0, 0)

    @pl.when(i + 1 < N_TILES)
    def _():
        gather_tile(i + 1, 1 - slot)

    for which in range(2):
        pltpu.make_async_copy(ys_ref.at[pl.ds(0, TM), :], buf.at[slot, which], sem.at[slot]).wait()
    w1 = route_ref[:, 2:3]
    w2 = route_ref[:, 3:4]
    y = buf[slot, 0] * w1 + buf[slot, 1] * w2
    x2 = x1_ref[...] + mod_ref[5:6, :] * y
    if final:
        x2 = x2 * lax.rsqrt(jnp.mean(x2 * x2, axis=-1, keepdims=True) + EPS) * fg_ref[...]

    @pl.when(i < N_PROMPT_TILES)
    def _():
        op_ref[...] = x2

    @pl.when(i >= N_PROMPT_TILES)
    def _():
        ol_ref[...] = x2


def _moe_combine_call(l_arr, e1, e2, r1, r2, pstart, x1, route, mod6, final_g, ys, final):
    grid_spec = pltpu.PrefetchScalarGridSpec(
        num_scalar_prefetch=6,
        grid=(N_TILES,),
        in_specs=[
            pl.BlockSpec((TM, D), lambda i, *_: (i, 0)),
            pl.BlockSpec((TM, LANES), lambda i, *_: (i, 0)),
            pl.BlockSpec((None, None, 6, D), lambda i, l, *_: (l[0], _mod_row(i), 0, 0)),
            pl.BlockSpec((1, D), lambda i, *_: (0, 0)),
            pl.BlockSpec(memory_space=pl.ANY),
        ],
        out_specs=_pair_specs(D),
        scratch_shapes=[pltpu.VMEM((2, 2, TM, D), F32), pltpu.SemaphoreType.DMA((2,))],
    )
    return pl.pallas_call(
        functools.partial(_moe_combine_kernel, final=final),
        grid_spec=grid_spec,
        out_shape=[jax.ShapeDtypeStruct((T_P, D), F32), jax.ShapeDtypeStruct((T_L, D), F32)],
        compiler_params=_cparams(("arbitrary",)),
        name="moe_combine_final" if final else "moe_combine",
    )(l_arr, e1, e2, r1, r2, pstart, x1, route, mod6, final_g, ys)


def _rope_tables():
    n = LATENT_LEN
    rows = n // GRID_W
    row = jnp.repeat(jnp.arange(rows), GRID_W).astype(F32)
    col = jnp.tile(jnp.arange(GRID_W), rows).astype(F32)
    axis_dim = DA_HEAD_DIM // 2
    inv_freq = 1.0 / (ROPE_BASE ** (jnp.arange(0, axis_dim, 2, dtype=F32) / axis_dim))
    ar, ac = row[:, None] * inv_freq, col[:, None] * inv_freq
    cos64 = jnp.concatenate([jnp.cos(ar), jnp.cos(ar), jnp.cos(ac), jnp.cos(ac)], axis=1)
    sin64 = jnp.concatenate([-jnp.sin(ar), jnp.sin(ar), -jnp.sin(ac), jnp.sin(ac)], axis=1)
    reps = DA_QK // DA_HEAD_DIM
    cos = jnp.concatenate([jnp.tile(cos64, (1, reps)), jnp.ones((TM, DA_QK), F32)], axis=0)
    sin = jnp.concatenate([jnp.tile(sin64, (1, reps)), jnp.zeros((TM, DA_QK), F32)], axis=0)
    return cos, sin


def _pad_lanes(a, n=LANES):
    return jnp.pad(a, [(0, 0)] * (a.ndim - 1) + [(0, n - a.shape[-1])])


def kernel(x_prompt, x_sample, cache_k, cache_v, state_ssd, c, c_ctx, w_ada, b_ada, norm1_g, norm2_g, w_in, b_in,
           ssd_conv_w, ssd_conv_b, ssd_dt_bias, ssd_a_log, ssd_d, ssd_norm_g, w_br_ssd, cv_dw_w, cv_dw_b, cv_ln_g,
           cv_ln_b, w_br_conv, da_lambda, da_subln_g, w_br_attn, w_out, moe_w_group, moe_b_group, moe_w_expert,
           moe_b_expert, moe_w_gate, moe_w_up, moe_w_down, final_g):
    L = DEPTH
    x_p, x_l = x_prompt.reshape(T_P, D), x_sample.reshape(T_L, D)

    cvec = jnp.concatenate([c, c_ctx[None, :], jnp.zeros((16 - N_LATENT_SEQ - 1, D), F32)], axis=0)
    mod6 = _ada_call(cvec, w_ada, b_ada).reshape(L, 16, 6, D)

    o_z, o_xbc, o_dt = 0, D, D + SSD_XBC
    o_glu = o_dt + 2 * SSD_HEADS
    o_q = o_glu + 2 * CONV_CH
    o_k, o_v, o_g = o_q + DA_QK, o_q + 2 * DA_QK, o_q + 2 * DA_QK + DA_V

    def regroup(w):
        return jnp.concatenate([w[..., o_z:o_dt], _pad_lanes(w[..., o_dt:o_glu]), w[..., o_glu:o_g]], axis=-1)

    w_proj = regroup(w_in).astype(BF16)
    b_proj = regroup(b_in).reshape(L, 1, N_PROJ)
    w_gates = w_in[..., o_g:].astype(BF16)
    b_gates = b_in[..., o_g:].reshape(L, 1, 3 * D)
    cos_tab, sin_tab = _rope_tables()

    tabs_f, tabs_b = _ssd_tables(True), _ssd_tables(False)
    dt_bias = _pad_lanes(ssd_dt_bias.reshape(L, 1, 2 * SSD_HEADS))
    a_log = _pad_lanes(ssd_a_log.reshape(L, 1, 2 * SSD_HEADS))
    dskip = jnp.repeat(ssd_d, SSD_HEAD_DIM, axis=-1).reshape(L, 1, D)
    hp = np.arange(D) // SSD_HEAD_DIM
    e_f, e_b = (jnp.asarray(np.tile(np.arange(LANES)[:, None] == ho + hp[None, :], (3, 1)).astype(np.float32))
                .astype(BF16) for ho in (0, SSD_HEADS))
    h0t = jnp.transpose(state_ssd, (1, 0, 2, 5, 3, 4)).reshape(L, N_LATENT_SEQ, 2, SSD_STATE, D)
    h0t = jnp.concatenate([h0t, jnp.zeros((L, 1, 2, SSD_STATE, D), F32)], axis=1)

    w_router = _pad_lanes(jnp.concatenate([moe_w_group, moe_w_expert], axis=-1))
    wr_hi = w_router.astype(BF16)
    wr_mid = (w_router - wr_hi.astype(F32)).astype(BF16)
    w_router = jnp.concatenate([wr_hi, wr_hi, wr_mid], axis=1)
    b_router = _pad_lanes(jnp.concatenate([moe_b_group, moe_b_expert], axis=-1)).reshape(L, 1, LANES)
    w_br_ssd_b, w_br_conv_b = w_br_ssd.astype(BF16), w_br_conv.astype(BF16)
    w_br_attn_b, w_out_b = w_br_attn.astype(BF16), w_out.astype(BF16)
    r3 = lambda a: a.reshape(L, 1, a.shape[-1])

    ks_new, vs_new, ss_new = [], [], []
    for layer in range(L):
        l_arr = jnp.full((1,), layer, jnp.int32)
        h, z, xbc, dt, u, q, k, v = _inproj_call(l_arr, x_p, x_l, mod6, r3(norm1_g), w_proj, b_proj,
                                                 cos_tab, sin_tab)
        y_f, st_f, xc = _ssd_call(l_arr, tabs_f, xbc, dt, ssd_conv_w, r3(ssd_conv_b), dt_bias, a_log, dskip,
                                  e_f, h0t, None, True)
        y_b, st_b = _ssd_call(l_arr, tabs_b, None, dt, None, None, dt_bias, a_log, None, e_b, h0t, xc, False)
        uc = _cv_call(l_arr, u, cv_dw_w, r3(cv_dw_b), r3(cv_ln_g), r3(cv_ln_b))
        oa_p = _attn_call(l_arr, q, k, v, None, None, da_lambda, r3(da_subln_g), latent=False)
        oa_l = _attn_call(l_arr, q, k, v, cache_k, cache_v, da_lambda, r3(da_subln_g), latent=True)
        x1, h2, route, counts, e1, e2, r1, r2 = _combine_call(
            l_arr, x_p, x_l, h, y_f, y_b, z, uc, oa_p, oa_l, mod6, w_gates, b_gates, r3(ssd_norm_g), w_br_ssd_b,
            w_br_conv_b, w_br_attn_b, w_out_b, r3(norm2_g), w_router, b_router)
        e1, e2, r1, r2 = (a.reshape(T) for a in (e1, e2, r1, r2))
        cnt = counts[0, ROUTER_LANE0:ROUTER_LANE0 + MOE_E].astype(jnp.int32)
        pcnt = (cnt + MOE_BLOCK - 1) // MOE_BLOCK * MOE_BLOCK
        pend = jnp.cumsum(pcnt)
        pstart = pend - pcnt
        blk0 = jnp.arange(N_SLOT_BLOCKS, dtype=jnp.int32) * MOE_BLOCK
        block_expert = jnp.minimum(jnp.sum((pend[None, :] <= blk0[:, None]).astype(jnp.int32), axis=1), MOE_E - 1)
        n_used = pend[-1:] // MOE_BLOCK

        xs = _dispatch_call(e1, e2, r1, r2, pstart, pend, h2)
        ys = _moe_call(l_arr, block_expert, n_used, xs, moe_w_gate, moe_w_up, moe_w_down)
        x_p, x_l = _moe_combine_call(l_arr, e1, e2, r1, r2, pstart, x1, route, mod6, final_g.reshape(1, D), ys,
                                     layer == L - 1)

        ks_new.append(k[:T_P].reshape(N_PROMPT_SEQ, PROMPT_LEN, DA_HEADS, 2 * DA_HEAD_DIM))
        vs_new.append(v[:T_P].reshape(N_PROMPT_SEQ, PROMPT_LEN, DA_HEADS, DA_V_DIM))
        s = jnp.stack([st_f[:N_PROMPT_SEQ], st_b[:N_PROMPT_SEQ]], axis=1)
        s = s.reshape(N_PROMPT_SEQ, 2, SSD_STATE, SSD_HEADS, SSD_HEAD_DIM)
        ss_new.append(jnp.transpose(s, (0, 1, 3, 4, 2)))

    y_prompt = x_p.reshape(N_PROMPT_SEQ, PROMPT_LEN, D)
    y_sample = x_l.reshape(N_LATENT_SEQ, LATENT_LEN, D)
    return (y_prompt, y_sample, jnp.stack(ks_new, axis=1), jnp.stack(vs_new, axis=1), jnp.stack(ss_new, axis=1))
```

```python
import functools
import math

import jax
import jax.numpy as jnp
import numpy as np
from jax import lax
from jax.experimental import pallas as pl
from jax.experimental.pallas import tpu as pltpu

D = 1024
DEPTH = 4
N_PROMPT_SEQ, PROMPT_LEN = 16, 256
N_LATENT_SEQ, LATENT_LEN = 8, 2048
PAST_LEN = 512
T_P = N_PROMPT_SEQ * PROMPT_LEN
T_L = N_LATENT_SEQ * LATENT_LEN
T = T_P + T_L
GRID_W = 64
EPS = 1e-6
SSD_HEADS, SSD_HEAD_DIM, SSD_STATE, SSD_GROUPS = 16, 64, 64, 2
SSD_CONV_W = 5
SSD_CHUNK = 128
SSD_XBC = D + 2 * SSD_GROUPS * SSD_STATE
CONV_CH, CONV_W = 512, 31
DA_HEADS, DA_HEAD_DIM, DA_V_DIM = 4, 64, 128
DA_QK = DA_HEADS * 2 * DA_HEAD_DIM
DA_V = DA_HEADS * DA_V_DIM
ROPE_BASE = 10000.0
MOE_GROUPS, MOE_EPG, MOE_E, MOE_HIDDEN = 4, 8, 32, 512
ROUTER_LANE0 = MOE_GROUPS

LANES = 128
SUBLANES = 8
TM = 256
N_TILES = T // TM
N_PROMPT_TILES = T_P // TM
TILES_PER_LATENT_SEQ = LATENT_LEN // TM
MOE_BLOCK = 256
N_SLOT_BLOCKS = (2 * T) // MOE_BLOCK + MOE_E
N_SLOTS = N_SLOT_BLOCKS * MOE_BLOCK
VMEM_LIMIT = 56 * 1024 * 1024

F32 = jnp.float32
BF16 = jnp.bfloat16
HI = lax.Precision.HIGHEST
NEG = -1e30


def _cparams(sem, vmem=VMEM_LIMIT):
    return pltpu.CompilerParams(dimension_semantics=sem, vmem_limit_bytes=vmem)


def _mod_row(i):
    return jnp.where(i < N_PROMPT_TILES, N_LATENT_SEQ, (i - N_PROMPT_TILES) // TILES_PER_LATENT_SEQ)


def _pair_specs(n, tile_arg=0):
    def prompt(*a):
        return (jnp.minimum(a[tile_arg], N_PROMPT_TILES - 1), 0)

    def latent(*a):
        return (jnp.maximum(a[tile_arg] - N_PROMPT_TILES, 0), 0)

    return [pl.BlockSpec((TM, n), prompt), pl.BlockSpec((TM, n), latent)]


def _pair_read(i, p_ref, l_ref):
    return jnp.where(i < N_PROMPT_TILES, p_ref[...], l_ref[...])


def _silu(x):
    return x * (1.0 / (1.0 + jnp.exp(-x)))


def _sigmoid(x):
    return 1.0 / (1.0 + jnp.exp(-x))


def _softplus(x):
    return jnp.maximum(x, 0.0) + jnp.log(1.0 + jnp.exp(-jnp.abs(x)))


def _dot(a, b, **kw):
    return jnp.dot(a, b, preferred_element_type=F32, **kw)


def _split3(x):
    hi = x.astype(BF16)
    r1 = x - hi.astype(F32)
    mid = r1.astype(BF16)
    lo = (r1 - mid.astype(F32)).astype(BF16)
    return jnp.concatenate([hi, mid, lo], axis=1)


def _dot_nt(a, b):
    return lax.dot_general(a, b, (((1,), (1,)), ((), ())), preferred_element_type=F32)


def _ada_kernel(c_ref, w_ref, b_ref, o_ref):
    cs = _silu(c_ref[...])
    o_ref[...] = _dot(cs, w_ref[...], precision=HI) + b_ref[...]


def _ada_call(cvec, w_ada, b_ada):
    nj = 6
    return pl.pallas_call(
        _ada_kernel,
        out_shape=jax.ShapeDtypeStruct((DEPTH, 16, 6 * D), F32),
        grid=(DEPTH, nj),
        in_specs=[
            pl.BlockSpec((16, D), lambda l, j: (0, 0)),
            pl.BlockSpec((None, D, D), lambda l, j: (l, 0, j)),
            pl.BlockSpec((None, 1, D), lambda l, j: (l, 0, j)),
        ],
        out_specs=pl.BlockSpec((None, 16, D), lambda l, j: (l, 0, j)),
        compiler_params=_cparams(("arbitrary", "arbitrary")),
        name="ada_mod",
    )(cvec, w_ada, b_ada.reshape(DEPTH, 1, 6 * D))


_C_Z = (0, D)
_C_XBC = (_C_Z[1], _C_Z[1] + SSD_XBC)
_C_DT = (_C_XBC[1], _C_XBC[1] + LANES)
_C_GLU = (_C_DT[1], _C_DT[1] + 2 * CONV_CH)
_C_Q = (_C_GLU[1], _C_GLU[1] + DA_QK)
_C_K = (_C_Q[1], _C_Q[1] + DA_QK)
_C_V = (_C_K[1], _C_K[1] + DA_V)
N_PROJ = _C_V[1]


def _swap16(x):
    cols = []
    for c in range(x.shape[1] // LANES):
        xc = x[:, c * LANES:(c + 1) * LANES]
        lane = lax.broadcasted_iota(jnp.int32, xc.shape, 1)
        from_right = pltpu.roll(xc, LANES - 16, axis=1)
        from_left = pltpu.roll(xc, 16, axis=1)
        cols.append(jnp.where((lane >> 4) % 2 == 0, from_right, from_left))
    return jnp.concatenate(cols, axis=1)


def _inproj_kernel(l_ref, xp_ref, xl_ref, mod_ref, g_ref, w_ref, b_ref, cos_ref, sin_ref,
                   h_ref, z_ref, xbc_ref, dt_ref, u_ref, q_ref, k_ref, v_ref, kc_ref, vc_ref):
    i = pl.program_id(0)
    x = _pair_read(i, xp_ref, xl_ref)
    xn = x * lax.rsqrt(jnp.mean(x * x, axis=-1, keepdims=True) + EPS)
    h = xn * g_ref[...] * (1.0 + mod_ref[1:2, :]) + mod_ref[0:1, :]
    hb = h.astype(BF16)
    h_ref[...] = hb

    def proj(c):
        return _dot(hb, w_ref[:, c[0]:c[1]]) + b_ref[:, c[0]:c[1]]

    z_ref[...] = proj(_C_Z).astype(BF16)
    xbc_ref[...] = proj(_C_XBC)
    dt_ref[...] = proj(_C_DT)
    glu = proj(_C_GLU)
    u_ref[...] = glu[:, :CONV_CH] * _sigmoid(glu[:, CONV_CH:])
    cos = cos_ref[...]
    sin = sin_ref[...]
    q = proj(_C_Q)
    q = q * cos + _swap16(q) * sin
    q_ref[...] = (q * (DA_HEAD_DIM ** -0.5 * math.log2(math.e))).astype(BF16)
    k = proj(_C_K)
    k = k * cos + _swap16(k) * sin
    k_ref[...] = k
    v = proj(_C_V)
    v_ref[...] = v

    @pl.when(i < N_PROMPT_TILES)
    def _():
        kc_ref[...] = k.reshape(TM, DA_HEADS, LANES)
        vc_ref[...] = v.reshape(TM, DA_HEADS, LANES)


def _inproj_call(l_arr, x_p, x_l, mod6, norm1_g, w_proj, b_proj, cos_tab, sin_tab):
    tok = lambda n: pl.BlockSpec((TM, n), lambda i, l: (i, 0))
    rope_blk = lambda i, l: (jnp.where(i < N_PROMPT_TILES, TILES_PER_LATENT_SEQ,
                                       (i - N_PROMPT_TILES) % TILES_PER_LATENT_SEQ), 0)
    lay = lambda *shape: pl.BlockSpec((None,) + shape, lambda i, l: (l[0],) + (0,) * len(shape))
    grid_spec = pltpu.PrefetchScalarGridSpec(
        num_scalar_prefetch=1,
        grid=(N_TILES,),
        in_specs=_pair_specs(D) + [
            pl.BlockSpec((None, None, 6, D), lambda i, l: (l[0], _mod_row(i), 0, 0)),
            lay(1, D), lay(D, N_PROJ), lay(1, N_PROJ),
            pl.BlockSpec((TM, DA_QK), rope_blk), pl.BlockSpec((TM, DA_QK), rope_blk),
        ],
        out_specs=[tok(D), tok(D), tok(SSD_XBC), tok(LANES), tok(CONV_CH), tok(DA_QK), tok(DA_QK), tok(DA_V)]
        + [pl.BlockSpec((None, PROMPT_LEN, DA_HEADS, LANES),
                        lambda i, l: (jnp.minimum(i, N_PROMPT_SEQ - 1), 0, 0, 0))] * 2,
    )
    sds = lambda n, dt: jax.ShapeDtypeStruct((T, n), dt)
    cache = jax.ShapeDtypeStruct((N_PROMPT_SEQ, PROMPT_LEN, DA_HEADS, LANES), F32)
    return pl.pallas_call(
        _inproj_kernel,
        grid_spec=grid_spec,
        out_shape=[sds(D, BF16), sds(D, BF16), sds(SSD_XBC, F32), sds(LANES, F32), sds(CONV_CH, F32),
                   sds(DA_QK, BF16), sds(DA_QK, F32), sds(DA_V, F32), cache, cache],
        compiler_params=_cparams(("arbitrary",)),
        name="inproj",
    )(l_arr, x_p, x_l, mod6, norm1_g, w_proj, b_proj, cos_tab, sin_tab)


SSD_STEP = TM
CHUNKS_PER_STEP = SSD_STEP // SSD_CHUNK
N_SSD_STEPS = T // SSD_STEP
N_PROMPT_STEPS = T_P // SSD_STEP
STEPS_PER_PROMPT = PROMPT_LEN // SSD_STEP
STEPS_PER_LATENT = LATENT_LEN // SSD_STEP
HALO = SUBLANES
STATE_DUMP = N_PROMPT_SEQ


def _ssd_tables(fwd):
    cidx, flags, h0, so = (np.zeros((N_SSD_STEPS,), np.int32) for _ in range(4))
    for j in range(N_SSD_STEPS):
        c = j if fwd else N_SSD_STEPS - 1 - j
        if c < N_PROMPT_STEPS:
            seq, pos, n = c // STEPS_PER_PROMPT, c % STEPS_PER_PROMPT, STEPS_PER_PROMPT
            h0i, soi = N_LATENT_SEQ, seq
        else:
            cc = c - N_PROMPT_STEPS
            seq, pos, n = cc // STEPS_PER_LATENT, cc % STEPS_PER_LATENT, STEPS_PER_LATENT
            h0i, soi = seq, STATE_DUMP
        first = pos == 0 if fwd else pos == n - 1
        cidx[j] = c
        flags[j] = int(first) | (int(pos > 0) << 1) | (int(pos < n - 1) << 2)
        h0[j] = h0i
        so[j] = soi
    return [jnp.asarray(a) for a in (cidx, flags, h0, so)]


def _ssd_conv(flags, xc_ref, xp_ref, xn_ref, cw_ref, cb_ref, xpad):
    q = SSD_STEP
    xpad[0:HALO, :] = jnp.where(((flags >> 1) & 1) == 1, xp_ref[...], 0.0)
    xpad[HALO:HALO + q, :] = xc_ref[...]
    xpad[HALO + q:HALO + q + HALO, :] = jnp.where(((flags >> 2) & 1) == 1, xn_ref[...], 0.0)
    acc = jnp.zeros((q, SSD_XBC), F32) + cb_ref[...]
    pad = SSD_CONV_W // 2
    xp = xpad[...]
    rows = q + 2 * HALO
    for k in range(SSD_CONV_W):
        shifted = xp if k == pad else pltpu.roll(xp, (pad - k) % rows, axis=0)
        acc = acc + cw_ref[k:k + 1, :] * shifted[HALO:HALO + q, :]
    return _silu(acc)


def _ssd_scan_step(fwd, flags, xc, dt_ref, dtb_ref, alog_ref, dskip_ref, e_ref, h0_ref, y_ref, so_ref, state, lat_s):
    @pl.when((flags & 1) == 1)
    def _():
        state[...] = h0_ref[...]

    dt = _softplus(dt_ref[...] + dtb_ref[...])
    da = dt * (-jnp.exp(alog_ref[...]))
    st = state[...]
    for k in (range(CHUNKS_PER_STEP) if fwd else reversed(range(CHUNKS_PER_STEP))):
        rows = slice(k * SSD_CHUNK, (k + 1) * SSD_CHUNK)
        st = _ssd_scan_chunk(fwd, xc[rows], dt[rows], da[rows], st, dskip_ref, e_ref, y_ref, rows, lat_s.at[k])
    state[...] = st
    so_ref[...] = st


def _ssd_scan_chunk(fwd, xc, dt, da, st_all, dskip_ref, e_ref, y_ref, rows, lat_s):
    q = SSD_CHUNK
    ho = 0 if fwd else SSD_HEADS
    xs = xc[:, :D]
    bm = xc[:, D:D + LANES]
    cm = xc[:, D + LANES:D + 2 * LANES]
    row = lax.broadcasted_iota(jnp.int32, (q, q), 0)
    col = lax.broadcasted_iota(jnp.int32, (q, q), 1)
    tri = (col <= row) if fwd else (col >= row)
    p = _dot(jnp.where(tri, 1.0, 0.0).astype(BF16), _split3(da))
    la = ((p[:, :LANES] + p[:, LANES:2 * LANES]) + p[:, 2 * LANES:]) * math.log2(math.e)
    lat_s[...] = la.T
    e3 = e_ref[...]
    la_exp = _dot(_split3(la), e3)
    dt_exp = _dot(_split3(dt), e3)
    la_end = la_exp[q - 1:q, :] if fwd else la_exp[0:1, :]
    decay_end = jnp.exp2(la_end - la_exp)
    chunk_decay = jnp.exp2(la_end)
    decay_in = jnp.exp2(la_exp)
    xdt = xs * dt_exp
    xdt_b = xdt.astype(BF16)
    xdtw_b = (xdt * decay_end).astype(BF16)
    bmt = bm.T

    half = LANES // 2
    lane = lax.broadcasted_iota(jnp.int32, (q, LANES), 1)
    hpg = SSD_HEADS // SSD_GROUPS
    st_new = []
    for g in range(SSD_GROUPS):
        c_g = cm[:, g * half:(g + 1) * half].astype(BF16)
        b_g = bm[:, g * half:(g + 1) * half].astype(BF16)
        bt_g = bmt[g * half:(g + 1) * half, :].astype(BF16)
        cb = _dot_nt(c_g, b_g)
        for pp in range(hpg // 2):
            h_a = g * hpg + 2 * pp
            sl = slice(h_a * SSD_HEAD_DIM, (h_a + 2) * SSD_HEAD_DIM)
            xdt_p = xdt_b[:, sl]
            yd = []
            for h in (ho + h_a, ho + h_a + 1):
                seg = la[:, h:h + 1] - lat_s[h:h + 1, :]
                s_h = (cb * jnp.exp2(jnp.where(tri, seg, NEG))).astype(BF16)
                yd.append(_dot(s_h, xdt_p))
            y_diag = jnp.where(lane < half, yd[0], yd[1])
            st_in = st_all[:, sl]
            y = y_diag + _dot(c_g, st_in.astype(BF16)) * decay_in[:, sl]
            st_new.append(st_in * chunk_decay[:, sl] + _dot(bt_g, xdtw_b[:, sl]))
            if fwd:
                y = y + xs[:, sl] * dskip_ref[:, sl]
            y_ref[rows, sl] = y.astype(BF16)
    return jnp.concatenate(st_new, axis=1)


def _ssd_fwd_kernel(l_ref, cidx_ref, flags_ref, h0i_ref, soi_ref,
                    xc_ref, xp_ref, xn_ref, cw_ref, cb_ref, dt_ref, dtb_ref, alog_ref, dskip_ref, e_ref, h0_ref,
                    y_ref, so_ref, xco_ref, state, lat_s, xpad):
    flags = flags_ref[pl.program_id(0)]
    xc = _ssd_conv(flags, xc_ref, xp_ref, xn_ref, cw_ref, cb_ref, xpad)
    xco_ref[...] = xc
    _ssd_scan_step(True, flags, xc, dt_ref, dtb_ref, alog_ref, dskip_ref, e_ref, h0_ref, y_ref, so_ref, state, lat_s)


def _ssd_bwd_kernel(l_ref, cidx_ref, flags_ref, h0i_ref, soi_ref,
                    xc_ref, dt_ref, dtb_ref, alog_ref, e_ref, h0_ref, y_ref, so_ref, state, lat_s):
    flags = flags_ref[pl.program_id(0)]
    _ssd_scan_step(False, flags, xc_ref[...], dt_ref, dtb_ref, alog_ref, None, e_ref, h0_ref, y_ref, so_ref,
                   state, lat_s)


def _ssd_call(l_arr, tabs, xbc, dt, conv_w, conv_b, dt_bias, a_log, dskip, e_mat, h0t, xc_in, fwd):
    nb8 = T // HALO
    per = SSD_STEP // HALO
    cur = lambda j, l, ci, fl, h0, so: (ci[j], 0)
    prev = lambda j, l, ci, fl, h0, so: (jnp.maximum(ci[j] * per - 1, 0), 0)
    nxt = lambda j, l, ci, fl, h0, so: (jnp.minimum((ci[j] + 1) * per, nb8 - 1), 0)
    lay = lambda *shape: pl.BlockSpec((None,) + shape, lambda j, l, *_: (l[0],) + (0,) * len(shape))
    chunk = lambda n: pl.BlockSpec((SSD_STEP, n), cur)
    common_in = [chunk(LANES), lay(1, LANES), lay(1, LANES)]
    tail_in = [pl.BlockSpec((3 * LANES, D), lambda j, *_: (0, 0)),
               pl.BlockSpec((None, None, None, SSD_STATE, D),
                            lambda j, l, ci, fl, h0, so: (l[0], h0[j], 0 if fwd else 1, 0, 0))]
    out_specs = [chunk(D), pl.BlockSpec((None, SSD_STATE, D), lambda j, l, ci, fl, h0, so: (so[j], 0, 0))]
    out_shape = [jax.ShapeDtypeStruct((T, D), BF16), jax.ShapeDtypeStruct((N_PROMPT_SEQ + 1, SSD_STATE, D), F32)]
    scratch = [pltpu.VMEM((SSD_STATE, D), F32), pltpu.VMEM((CHUNKS_PER_STEP, LANES, SSD_CHUNK), F32)]
    if fwd:
        in_specs = ([chunk(SSD_XBC), pl.BlockSpec((HALO, SSD_XBC), prev), pl.BlockSpec((HALO, SSD_XBC), nxt),
                     lay(SSD_CONV_W, SSD_XBC), lay(1, SSD_XBC)] + common_in + [lay(1, D)] + tail_in)
        args = (xbc, xbc, xbc, conv_w, conv_b, dt, dt_bias, a_log, dskip, e_mat, h0t)
        out_specs.append(chunk(SSD_XBC))
        out_shape.append(jax.ShapeDtypeStruct((T, SSD_XBC), F32))
        scratch.append(pltpu.VMEM((SSD_STEP + 2 * HALO, SSD_XBC), F32))
    else:
        in_specs = [chunk(SSD_XBC)] + common_in + tail_in
        args = (xc_in, dt, dt_bias, a_log, e_mat, h0t)
    grid_spec = pltpu.PrefetchScalarGridSpec(num_scalar_prefetch=5, grid=(N_SSD_STEPS,), in_specs=in_specs,
                                             out_specs=out_specs, scratch_shapes=scratch)
    return pl.pallas_call(
        _ssd_fwd_kernel if fwd else _ssd_bwd_kernel,
        grid_spec=grid_spec,
        out_shape=out_shape,
        compiler_params=_cparams(("arbitrary",)),
        name="ssd_scan_fwd" if fwd else "ssd_scan_bwd",
    )(l_arr, *tabs, *args)


CV_HALO = 16


def _conformer_conv(i, uc_ref, up_ref, un_ref, w_ref, b_ref, g_ref, beta_ref, upad):
    pos = (i - N_PROMPT_TILES) % TILES_PER_LATENT_SEQ
    is_prompt = i < N_PROMPT_TILES
    no_l = jnp.logical_or(is_prompt, pos == 0)
    no_r = jnp.logical_or(is_prompt, pos == TILES_PER_LATENT_SEQ - 1)
    upad[0:CV_HALO, :] = jnp.where(no_l, 0.0, up_ref[...])
    upad[CV_HALO:CV_HALO + TM, :] = uc_ref[...]
    upad[CV_HALO + TM:CV_HALO + TM + CV_HALO, :] = jnp.where(no_r, 0.0, un_ref[...])
    acc = jnp.zeros((TM, CONV_CH), F32) + b_ref[...]
    pad = CONV_W // 2
    up = upad[...]
    rows = TM + 2 * CV_HALO
    for rot in range(SUBLANES):
        taps = [k for k in range(CONV_W) if (CV_HALO - pad + k) % SUBLANES == rot]
        rolled = up if rot == 0 else pltpu.roll(up, rows - rot, axis=0)
        for k in taps:
            base = CV_HALO - pad + k - rot
            acc = acc + w_ref[k:k + 1, :] * rolled[base:base + TM, :]
    mu = jnp.mean(acc, axis=-1, keepdims=True)
    xc = acc - mu
    var = jnp.mean(xc * xc, axis=-1, keepdims=True)
    y = xc * lax.rsqrt(var + EPS) * g_ref[...] + beta_ref[...]
    return _silu(y).astype(BF16)


def _conformer_conv_specs():
    per = TM // CV_HALO
    nb = T // CV_HALO
    return [pl.BlockSpec((TM, CONV_CH), lambda i, l: (i, 0)),
            pl.BlockSpec((CV_HALO, CONV_CH), lambda i, l: (jnp.maximum(i * per - 1, 0), 0)),
            pl.BlockSpec((CV_HALO, CONV_CH), lambda i, l: (jnp.minimum((i + 1) * per, nb - 1), 0))]


def _lambda_terms(l_ref, lam_ref):
    lf = jnp.full((1, 1), l_ref[0], jnp.int32).astype(F32)
    lam_init = 0.8 - 0.6 * jnp.exp(-0.3 * lf)
    p = lam_ref[...]
    s1 = jnp.sum(p[0:1, :] * p[1:2, :], axis=-1, keepdims=True)
    s2 = jnp.sum(p[2:3, :] * p[3:4, :], axis=-1, keepdims=True)
    lam = jnp.exp(s1) - jnp.exp(s2) + lam_init
    return lam, 1.0 - lam_init


def _attn_body(l_ref, q_ref, k_ref, v_ref, ck_ref, cv_ref, lam_ref, g_ref, o_ref, k_s, v_s, n_ctx):
    @pl.when(pl.program_id(1) == 0)
    def _():
        k_s[n_ctx:, :] = k_ref[...].astype(BF16)
        for h in range(DA_HEADS):
            sl = slice(h * LANES, (h + 1) * LANES)
            if n_ctx:
                k_s[0:n_ctx, sl] = ck_ref[:, h, :].astype(BF16)
                v_s[0:n_ctx, 2 * h * LANES:(2 * h + 1) * LANES] = cv_ref[:, h, :].astype(BF16)
            v_s[n_ctx:, 2 * h * LANES:(2 * h + 1) * LANES] = v_ref[:, sl].astype(BF16)
            v_s[:, (2 * h + 1) * LANES:(2 * h + 2) * LANES] = jnp.ones((v_s.shape[0], LANES), BF16)

    lam, out_scale = _lambda_terms(l_ref, lam_ref)
    tq = q_ref.shape[0]
    lane = lax.broadcasted_iota(jnp.int32, (tq, LANES), 1)
    zero = jnp.zeros((tq, LANES), BF16)
    for h in range(DA_HEADS):
        sl = slice(h * LANES, (h + 1) * LANES)
        qh = q_ref[:, sl]
        kh = k_s[:, sl]
        vh = v_s[:, 2 * h * LANES:(2 * h + 2) * LANES]
        outs = []
        for c in range(2):
            in_c = (lane < DA_HEAD_DIM) if c == 0 else (lane >= DA_HEAD_DIM)
            s = _dot_nt(jnp.where(in_c, qh, zero), kh)
            m = jnp.max(s, axis=-1, keepdims=True)
            pv = _dot(jnp.exp2(s - m).astype(BF16), vh)
            outs.append(pv[:, :LANES] / pv[:, LANES:])
        o = outs[0] - lam * outs[1]
        o = o * lax.rsqrt(jnp.mean(o * o, axis=-1, keepdims=True) + EPS)
        o_ref[:, sl] = (o * g_ref[...] * out_scale).astype(BF16)


def _attn_prompt_kernel(l_ref, q_ref, k_ref, v_ref, lam_ref, g_ref, o_ref, k_s, v_s):
    _attn_body(l_ref, q_ref, k_ref, v_ref, None, None, lam_ref, g_ref, o_ref, k_s, v_s, 0)


def _attn_latent_kernel(l_ref, q_ref, k_ref, v_ref, ck_ref, cv_ref, lam_ref, g_ref, o_ref, k_s, v_s):
    _attn_body(l_ref, q_ref, k_ref, v_ref, ck_ref, cv_ref, lam_ref, g_ref, o_ref, k_s, v_s, PAST_LEN)


def _attn_call(l_arr, q, k, v, cache_k, cache_v, da_lambda, subln_g, latent):
    if latent:
        nseq, seqlen, n_ctx = N_LATENT_SEQ, LATENT_LEN, PAST_LEN
        tile0, seq0 = N_PROMPT_TILES, T_P // LATENT_LEN
    else:
        nseq, seqlen, n_ctx = N_PROMPT_SEQ, PROMPT_LEN, 0
        tile0, seq0 = 0, 0
    nq = seqlen // TM
    lay = lambda *shape: pl.BlockSpec((None,) + shape, lambda b, i, l: (l[0],) + (0,) * len(shape))
    in_specs = [
        pl.BlockSpec((TM, DA_QK), lambda b, i, l: (tile0 + b * nq + i, 0)),
        pl.BlockSpec((seqlen, DA_QK), lambda b, i, l: (seq0 + b, 0)),
        pl.BlockSpec((seqlen, DA_V), lambda b, i, l: (seq0 + b, 0)),
    ]
    args = [q, k, v]
    if latent:
        ctx = pl.BlockSpec((None, None, PAST_LEN, DA_HEADS, LANES), lambda b, i, l: (b, l[0], 0, 0, 0))
        in_specs += [ctx, ctx]
        args += [cache_k, cache_v]
    in_specs += [lay(4, DA_HEAD_DIM), lay(1, DA_V_DIM)]
    args += [da_lambda, subln_g]
    grid_spec = pltpu.PrefetchScalarGridSpec(
        num_scalar_prefetch=1,
        grid=(nseq, nq),
        in_specs=in_specs,
        out_specs=pl.BlockSpec((TM, DA_V), lambda b, i, l: (b * nq + i, 0)),
        scratch_shapes=[pltpu.VMEM((n_ctx + seqlen, DA_QK), BF16), pltpu.VMEM((n_ctx + seqlen, 2 * DA_V), BF16)],
    )
    return pl.pallas_call(
        _attn_latent_kernel if latent else _attn_prompt_kernel,
        grid_spec=grid_spec,
        out_shape=jax.ShapeDtypeStruct((nseq * seqlen, DA_V), BF16),
        compiler_params=_cparams(("arbitrary", "arbitrary")),
        name="diff_attn_latent" if latent else "diff_attn_prompt",
    )(l_arr, *args)


def _combine_kernel(l_ref, xp_ref, xl_ref, h_ref, yf_ref, yb_ref, z_ref, u_ref, up_ref, un_ref, oap_ref, oal_ref,
                    mod_ref, cvw_ref, cvb_ref, cvg_ref, cvbeta_ref,
                    wg_ref, bg_ref, sg_ref, wa_ref, wb_ref, wc_ref, wo_ref, g2_ref, wr_ref, br_ref,
                    x1_ref, h2_ref, route_ref, cnt_ref, e1_ref, e2_ref, r1_ref, r2_ref, carry, upad):
    i = pl.program_id(0)

    @pl.when(i == 0)
    def _():
        carry[...] = jnp.zeros_like(carry)

    uc = _conformer_conv(i, u_ref, up_ref, un_ref, cvw_ref, cvb_ref, cvg_ref, cvbeta_ref, upad)

    y = (yf_ref[...].astype(F32) + yb_ref[...].astype(F32)) * _silu(z_ref[...].astype(F32))
    y = y * lax.rsqrt(jnp.mean(y * y, axis=-1, keepdims=True) + EPS) * sg_ref[...]
    br_a = _dot(y.astype(BF16), wa_ref[...])
    br_b = _dot(uc, wb_ref[...])
    br_c = _dot(_pair_read(i, oap_ref, oal_ref), wc_ref[...])
    hb = h_ref[...]

    def gate(n):
        return _sigmoid(_dot(hb, wg_ref[:, n * D:(n + 1) * D]) + bg_ref[:, n * D:(n + 1) * D])

    mix = gate(0) * br_a + gate(1) * br_b + gate(2) * br_c
    mixed = _dot(mix.astype(BF16), wo_ref[...])
    x1 = _pair_read(i, xp_ref, xl_ref) + mod_ref[2:3, :] * mixed
    x1_ref[...] = x1
    xn = x1 * lax.rsqrt(jnp.mean(x1 * x1, axis=-1, keepdims=True) + EPS)
    h2 = xn * g2_ref[...] * (1.0 + mod_ref[4:5, :]) + mod_ref[3:4, :]
    h2_ref[...] = h2

    h2_hi = h2.astype(BF16)
    h2_mid = (h2 - h2_hi.astype(F32)).astype(BF16)
    logits = _dot(jnp.concatenate([h2_hi, h2_mid, h2_hi], axis=1), wr_ref[...]) + br_ref[...]
    lane = lax.broadcasted_iota(jnp.int32, (TM, LANES), 1).astype(F32)

    def first_argmax(vals, vmax):
        return jnp.min(jnp.where(vals == vmax, lane, float(LANES)), axis=-1, keepdims=True)

    glog = jnp.where(lane < MOE_GROUPS, logits, NEG)
    gmax = jnp.max(glog, axis=-1, keepdims=True)
    gsel = first_argmax(glog, gmax)
    p_g = 1.0 / jnp.sum(jnp.exp(glog - gmax), axis=-1, keepdims=True)
    lo = ROUTER_LANE0 + MOE_EPG * gsel
    elog = jnp.where(jnp.logical_and(lane >= lo, lane < lo + MOE_EPG), logits, NEG)
    v1 = jnp.max(elog, axis=-1, keepdims=True)
    i1 = first_argmax(elog, v1)
    elog2 = jnp.where(lane == i1, NEG, elog)
    v2 = jnp.max(elog2, axis=-1, keepdims=True)
    i2 = first_argmax(elog2, v2)
    t2 = jnp.exp(v2 - v1)
    w1 = p_g / (1.0 + t2)
    w2 = p_g * t2 / (1.0 + t2)

    oh1 = lane == i1
    oh2 = lane == i2
    both = jnp.where(jnp.logical_or(oh1, oh2), 1.0, 0.0)
    r = lax.broadcasted_iota(jnp.int32, (TM, TM), 0)
    c = lax.broadcasted_iota(jnp.int32, (TM, TM), 1)
    strict_lower = jnp.where(c < r, 1.0, 0.0).astype(BF16)
    before = _dot(strict_lower, both.astype(BF16)) + carry[0:1, :]
    r1 = jnp.sum(jnp.where(oh1, before, 0.0), axis=-1, keepdims=True)
    r2 = jnp.sum(jnp.where(oh2, before, 0.0), axis=-1, keepdims=True)
    carry[...] = carry[...] + jnp.sum(both, axis=0, keepdims=True)
    cnt_ref[...] = carry[...]

    e1 = i1 - ROUTER_LANE0
    e2 = i2 - ROUTER_LANE0
    route = jnp.zeros((TM, LANES), F32)
    for n, val in enumerate((e1, e2, w1, w2, r1, r2)):
        route = jnp.where(lane == float(n), val, route)
    route_ref[...] = route
    route_t = route.T
    for n, ref in ((0, e1_ref), (1, e2_ref), (4, r1_ref), (5, r2_ref)):
        ref[...] = route_t[n:n + 1, :].astype(jnp.int32)


def _combine_call(l_arr, x_p, x_l, h, y_f, y_b, z, u, oa_p, oa_l, mod6, cv_w, cv_b, cv_g, cv_beta, w_gates, b_gates,
                  ssd_norm_g, w_br_ssd, w_br_conv, w_br_attn, w_out, norm2_g, w_router, b_router):
    tok = lambda n: pl.BlockSpec((TM, n), lambda i, l: (i, 0))
    lay = lambda *shape: pl.BlockSpec((None,) + shape, lambda i, l: (l[0],) + (0,) * len(shape))
    grid_spec = pltpu.PrefetchScalarGridSpec(
        num_scalar_prefetch=1,
        grid=(N_TILES,),
        in_specs=_pair_specs(D) + [tok(D), tok(D), tok(D), tok(D)] + _conformer_conv_specs() + _pair_specs(DA_V) + [
            pl.BlockSpec((None, None, 6, D), lambda i, l: (l[0], _mod_row(i), 0, 0)),
            lay(CONV_W, CONV_CH), lay(1, CONV_CH), lay(1, CONV_CH), lay(1, CONV_CH),
            lay(D, 3 * D), lay(1, 3 * D), lay(1, D), lay(D, D), lay(CONV_CH, D), lay(DA_V, D), lay(D, D),
            lay(1, D), lay(3 * D, LANES), lay(1, LANES),
        ],
        out_specs=[tok(D), tok(D), tok(LANES), pl.BlockSpec((SUBLANES, LANES), lambda i, l: (0, 0))]
        + [pl.BlockSpec((None, 1, TM), lambda i, l: (i, 0, 0))] * 4,
        scratch_shapes=[pltpu.VMEM((SUBLANES, LANES), F32), pltpu.VMEM((TM + 2 * CV_HALO, CONV_CH), F32)],
    )
    return pl.pallas_call(
        _combine_kernel,
        grid_spec=grid_spec,
        out_shape=[jax.ShapeDtypeStruct((T, D), F32), jax.ShapeDtypeStruct((T, D), F32),
                   jax.ShapeDtypeStruct((T, LANES), F32), jax.ShapeDtypeStruct((SUBLANES, LANES), F32)]
        + [jax.ShapeDtypeStruct((N_TILES, 1, TM), jnp.int32)] * 4,
        compiler_params=_cparams(("arbitrary",)),
        name="branch_combine_router",
    )(l_arr, x_p, x_l, h, y_f, y_b, z, u, u, u, oa_p, oa_l, mod6, cv_w, cv_b, cv_g, cv_beta, w_gates, b_gates,
      ssd_norm_g, w_br_ssd, w_br_conv, w_br_attn, w_out, norm2_g, w_router, b_router)


ROW_DMA_GROUP = 64


def _for_each_row(fn):
    def group(g, carry):
        t0 = pl.multiple_of(g * ROW_DMA_GROUP, ROW_DMA_GROUP)
        for r in range(ROW_DMA_GROUP):
            fn(t0 + r)
        return carry

    lax.fori_loop(0, TM // ROW_DMA_GROUP, group, 0)


def _dispatch_kernel(e1_ref, e2_ref, r1_ref, r2_ref, ps_ref, pe_ref, h2_ref, xs_ref, zbuf, sem, zsem):
    i = pl.program_id(0)
    base = i * TM

    @pl.when(i == 0)
    def _():
        zbuf[...] = jnp.zeros_like(zbuf)

        def zero_fill(e):
            off = pl.multiple_of(pe_ref[e] - MOE_BLOCK, MOE_BLOCK)
            return pltpu.make_async_copy(zbuf, xs_ref.at[pl.ds(off, MOE_BLOCK), :], zsem)

        def unused_fill(b):
            off = pl.multiple_of(b * MOE_BLOCK, MOE_BLOCK)
            return pltpu.make_async_copy(zbuf, xs_ref.at[pl.ds(off, MOE_BLOCK), :], zsem)

        n_used = pe_ref[MOE_E - 1] // MOE_BLOCK
        for e in range(MOE_E):
            @pl.when(pe_ref[e] > ps_ref[e])
            def _():
                zero_fill(e).start()
        lax.fori_loop(n_used, N_SLOT_BLOCKS, lambda b, c: (unused_fill(b).start(), c)[1], 0)
        for e in range(MOE_E):
            @pl.when(pe_ref[e] > ps_ref[e])
            def _():
                zero_fill(e).wait()
        lax.fori_loop(n_used, N_SLOT_BLOCKS, lambda b, c: (unused_fill(b).wait(), c)[1], 0)

    def issue(t):
        for e_ref, r_ref in ((e1_ref, r1_ref), (e2_ref, r2_ref)):
            dest = ps_ref[e_ref[base + t]] + r_ref[base + t]
            pltpu.make_async_copy(h2_ref.at[pl.ds(t, 1), :], xs_ref.at[pl.ds(dest, 1), :], sem).start()

    _for_each_row(issue)
    for _ in range(2):
        pltpu.make_async_copy(h2_ref, xs_ref.at[pl.ds(0, TM), :], sem).wait()


def _dispatch_call(e1, e2, r1, r2, pstart, pend, h2):
    grid_spec = pltpu.PrefetchScalarGridSpec(
        num_scalar_prefetch=6,
        grid=(N_TILES,),
        in_specs=[pl.BlockSpec((TM, D), lambda i, *_: (i, 0))],
        out_specs=pl.BlockSpec(memory_space=pl.ANY),
        scratch_shapes=[pltpu.VMEM((MOE_BLOCK, D), F32), pltpu.SemaphoreType.DMA(()), pltpu.SemaphoreType.DMA(())],
    )
    return pl.pallas_call(
        _dispatch_kernel,
        grid_spec=grid_spec,
        out_shape=jax.ShapeDtypeStruct((N_SLOTS, D), F32),
        compiler_params=_cparams(("arbitrary",)),
        name="moe_dispatch",
    )(e1, e2, r1, r2, pstart, pend, h2)


def _moe_kernel(l_ref, be_ref, nb_ref, xs_ref, wg_ref, wu_ref, wd_ref, ys_ref, wg_s, wu_s, wd_s):
    i = pl.program_id(0)
    prev = be_ref[jnp.maximum(i - 1, 0)]

    @pl.when(jnp.logical_or(i == 0, be_ref[i] != prev))
    def _():
        wg_s[...] = wg_ref[...].astype(BF16)
        wu_s[...] = wu_ref[...].astype(BF16)
        wd_s[...] = wd_ref[...].astype(BF16)

    @pl.when(i < nb_ref[0])
    def _():
        xb = xs_ref[...].astype(BF16)
        hid = _silu(_dot(xb, wg_s[...])) * _dot(xb, wu_s[...])
        ys_ref[...] = _dot(hid.astype(BF16), wd_s[...])

    @pl.when(i >= nb_ref[0])
    def _():
        ys_ref[...] = jnp.zeros_like(ys_ref)


def _moe_call(l_arr, block_expert, n_used, xs, w_gate, w_up, w_down):
    wspec = lambda a, b: pl.BlockSpec((None, None, a, b), lambda i, l, be, nb: (l[0], be[i], 0, 0))
    grid_spec = pltpu.PrefetchScalarGridSpec(
        num_scalar_prefetch=3,
        grid=(N_SLOT_BLOCKS,),
        in_specs=[pl.BlockSpec((MOE_BLOCK, D), lambda i, l, be, nb: (jnp.minimum(i, nb[0] - 1), 0)),
                  wspec(D, MOE_HIDDEN), wspec(D, MOE_HIDDEN), wspec(MOE_HIDDEN, D)],
        out_specs=pl.BlockSpec((MOE_BLOCK, D), lambda i, l, be, nb: (i, 0)),
        scratch_shapes=[pltpu.VMEM((D, MOE_HIDDEN), BF16), pltpu.VMEM((D, MOE_HIDDEN), BF16),
                        pltpu.VMEM((MOE_HIDDEN, D), BF16)],
    )
    return pl.pallas_call(
        _moe_kernel,
        grid_spec=grid_spec,
        out_shape=jax.ShapeDtypeStruct((N_SLOTS, D), F32),
        compiler_params=_cparams(("arbitrary",)),
        name="moe_experts",
    )(l_arr, block_expert, n_used, xs, w_gate, w_up, w_down)


def _moe_combine_kernel(l_ref, e1_ref, e2_ref, r1_ref, r2_ref, ps_ref, x1_ref, route_ref, mod_ref, fg_ref, ys_ref,
                        op_ref, ol_ref, buf, sem, *, final):
    i = pl.program_id(0)
    slot = i % 2

    def gather_tile(tile, dst_slot):
        base = tile * TM

        def issue(t):
            for which, (e_ref, r_ref) in enumerate(((e1_ref, r1_ref), (e2_ref, r2_ref))):
                src = ps_ref[e_ref[base + t]] + r_ref[base + t]
                pltpu.make_async_copy(ys_ref.at[pl.ds(src, 1), :], buf.at[dst_slot, which, pl.ds(t, 1), :],
                                      sem.at[dst_slot]).start()

        _for_each_row(issue)

    @pl.when(i == 0)
    def _():
        gather_tile(0, 0)

    @pl.when(i + 1 < N_TILES)
    def _():
        gather_tile(i + 1, 1 - slot)

    for which in range(2):
        pltpu.make_async_copy(ys_ref.at[pl.ds(0, TM), :], buf.at[slot, which], sem.at[slot]).wait()
    w1 = route_ref[:, 2:3]
    w2 = route_ref[:, 3:4]
    y = buf[slot, 0] * w1 + buf[slot, 1] * w2
    x2 = x1_ref[...] + mod_ref[5:6, :] * y
    if final:
        x2 = x2 * lax.rsqrt(jnp.mean(x2 * x2, axis=-1, keepdims=True) + EPS) * fg_ref[...]

    @pl.when(i < N_PROMPT_TILES)
    def _():
        op_ref[...] = x2

    @pl.when(i >= N_PROMPT_TILES)
    def _():
        ol_ref[...] = x2


def _moe_combine_call(l_arr, e1, e2, r1, r2, pstart, x1, route, mod6, final_g, ys, final):
    grid_spec = pltpu.PrefetchScalarGridSpec(
        num_scalar_prefetch=6,
        grid=(N_TILES,),
        in_specs=[
            pl.BlockSpec((TM, D), lambda i, *_: (i, 0)),
            pl.BlockSpec((TM, LANES), lambda i, *_: (i, 0)),
            pl.BlockSpec((None, None, 6, D), lambda i, l, *_: (l[0], _mod_row(i), 0, 0)),
            pl.BlockSpec((1, D), lambda i, *_: (0, 0)),
            pl.BlockSpec(memory_space=pl.ANY),
        ],
        out_specs=_pair_specs(D),
        scratch_shapes=[pltpu.VMEM((2, 2, TM, D), F32), pltpu.SemaphoreType.DMA((2,))],
    )
    return pl.pallas_call(
        functools.partial(_moe_combine_kernel, final=final),
        grid_spec=grid_spec,
        out_shape=[jax.ShapeDtypeStruct((T_P, D), F32), jax.ShapeDtypeStruct((T_L, D), F32)],
        compiler_params=_cparams(("arbitrary",)),
        name="moe_combine_final" if final else "moe_combine",
    )(l_arr, e1, e2, r1, r2, pstart, x1, route, mod6, final_g, ys)


def _rope_tables():
    n = LATENT_LEN
    rows = n // GRID_W
    row = jnp.repeat(jnp.arange(rows), GRID_W).astype(F32)
    col = jnp.tile(jnp.arange(GRID_W), rows).astype(F32)
    axis_dim = DA_HEAD_DIM // 2
    inv_freq = 1.0 / (ROPE_BASE ** (jnp.arange(0, axis_dim, 2, dtype=F32) / axis_dim))
    ar, ac = row[:, None] * inv_freq, col[:, None] * inv_freq
    cos64 = jnp.concatenate([jnp.cos(ar), jnp.cos(ar), jnp.cos(ac), jnp.cos(ac)], axis=1)
    sin64 = jnp.concatenate([-jnp.sin(ar), jnp.sin(ar), -jnp.sin(ac), jnp.sin(ac)], axis=1)
    reps = DA_QK // DA_HEAD_DIM
    cos = jnp.concatenate([jnp.tile(cos64, (1, reps)), jnp.ones((TM, DA_QK), F32)], axis=0)
    sin = jnp.concatenate([jnp.tile(sin64, (1, reps)), jnp.zeros((TM, DA_QK), F32)], axis=0)
    return cos, sin


def _pad_lanes(a, n=LANES):
    return jnp.pad(a, [(0, 0)] * (a.ndim - 1) + [(0, n - a.shape[-1])])


def kernel(x_prompt, x_sample, cache_k, cache_v, state_ssd, c, c_ctx, w_ada, b_ada, norm1_g, norm2_g, w_in, b_in,
           ssd_conv_w, ssd_conv_b, ssd_dt_bias, ssd_a_log, ssd_d, ssd_norm_g, w_br_ssd, cv_dw_w, cv_dw_b, cv_ln_g,
           cv_ln_b, w_br_conv, da_lambda, da_subln_g, w_br_attn, w_out, moe_w_group, moe_b_group, moe_w_expert,
           moe_b_expert, moe_w_gate, moe_w_up, moe_w_down, final_g):
    L = DEPTH
    x_p, x_l = x_prompt.reshape(T_P, D), x_sample.reshape(T_L, D)

    cvec = jnp.concatenate([c, c_ctx[None, :], jnp.zeros((16 - N_LATENT_SEQ - 1, D), F32)], axis=0)
    mod6 = _ada_call(cvec, w_ada, b_ada).reshape(L, 16, 6, D)

    o_z, o_xbc, o_dt = 0, D, D + SSD_XBC
    o_glu = o_dt + 2 * SSD_HEADS
    o_q = o_glu + 2 * CONV_CH
    o_k, o_v, o_g = o_q + DA_QK, o_q + 2 * DA_QK, o_q + 2 * DA_QK + DA_V

    def regroup(w):
        return jnp.concatenate([w[..., o_z:o_dt], _pad_lanes(w[..., o_dt:o_glu]), w[..., o_glu:o_g]], axis=-1)

    w_proj = regroup(w_in).astype(BF16)
    b_proj = regroup(b_in).reshape(L, 1, N_PROJ)
    w_gates = w_in[..., o_g:].astype(BF16)
    b_gates = b_in[..., o_g:].reshape(L, 1, 3 * D)
    cos_tab, sin_tab = _rope_tables()

    tabs_f, tabs_b = _ssd_tables(True), _ssd_tables(False)
    dt_bias = _pad_lanes(ssd_dt_bias.reshape(L, 1, 2 * SSD_HEADS))
    a_log = _pad_lanes(ssd_a_log.reshape(L, 1, 2 * SSD_HEADS))
    dskip = jnp.repeat(ssd_d, SSD_HEAD_DIM, axis=-1).reshape(L, 1, D)
    hp = np.arange(D) // SSD_HEAD_DIM
    e_f, e_b = (jnp.asarray(np.tile(np.arange(LANES)[:, None] == ho + hp[None, :], (3, 1)).astype(np.float32))
                .astype(BF16) for ho in (0, SSD_HEADS))
    h0t = jnp.transpose(state_ssd, (1, 0, 2, 5, 3, 4)).reshape(L, N_LATENT_SEQ, 2, SSD_STATE, D)
    h0t = jnp.concatenate([h0t, jnp.zeros((L, 1, 2, SSD_STATE, D), F32)], axis=1)

    w_router = _pad_lanes(jnp.concatenate([moe_w_group, moe_w_expert], axis=-1))
    wr_hi = w_router.astype(BF16)
    wr_mid = (w_router - wr_hi.astype(F32)).astype(BF16)
    w_router = jnp.concatenate([wr_hi, wr_hi, wr_mid], axis=1)
    b_router = _pad_lanes(jnp.concatenate([moe_b_group, moe_b_expert], axis=-1)).reshape(L, 1, LANES)
    w_br_ssd_b, w_br_conv_b = w_br_ssd.astype(BF16), w_br_conv.astype(BF16)
    w_br_attn_b, w_out_b = w_br_attn.astype(BF16), w_out.astype(BF16)
    r3 = lambda a: a.reshape(L, 1, a.shape[-1])

    ks_new, vs_new, ss_new = [], [], []
    for layer in range(L):
        l_arr = jnp.full((1,), layer, jnp.int32)
        h, z, xbc, dt, u, q, k, v, k_cache, v_cache = _inproj_call(l_arr, x_p, x_l, mod6, r3(norm1_g), w_proj, b_proj,
                                                                   cos_tab, sin_tab)
        y_f, st_f, xc = _ssd_call(l_arr, tabs_f, xbc, dt, ssd_conv_w, r3(ssd_conv_b), dt_bias, a_log, dskip,
                                  e_f, h0t, None, True)
        y_b, st_b = _ssd_call(l_arr, tabs_b, None, dt, None, None, dt_bias, a_log, None, e_b, h0t, xc, False)
        oa_p = _attn_call(l_arr, q, k, v, None, None, da_lambda, r3(da_subln_g), latent=False)
        oa_l = _attn_call(l_arr, q, k, v, cache_k, cache_v, da_lambda, r3(da_subln_g), latent=True)
        x1, h2, route, counts, e1, e2, r1, r2 = _combine_call(
            l_arr, x_p, x_l, h, y_f, y_b, z, u, oa_p, oa_l, mod6, cv_dw_w, r3(cv_dw_b), r3(cv_ln_g), r3(cv_ln_b),
            w_gates, b_gates, r3(ssd_norm_g), w_br_ssd_b, w_br_conv_b, w_br_attn_b, w_out_b, r3(norm2_g),
            w_router, b_router)
        e1, e2, r1, r2 = (a.reshape(T) for a in (e1, e2, r1, r2))
        cnt = counts[0, ROUTER_LANE0:ROUTER_LANE0 + MOE_E].astype(jnp.int32)
        pcnt = (cnt + MOE_BLOCK - 1) // MOE_BLOCK * MOE_BLOCK
        pend = jnp.cumsum(pcnt)
        pstart = pend - pcnt
        blk0 = jnp.arange(N_SLOT_BLOCKS, dtype=jnp.int32) * MOE_BLOCK
        block_expert = jnp.minimum(jnp.sum((pend[None, :] <= blk0[:, None]).astype(jnp.int32), axis=1), MOE_E - 1)
        n_used = pend[-1:] // MOE_BLOCK

        xs = _dispatch_call(e1, e2, r1, r2, pstart, pend, h2)
        ys = _moe_call(l_arr, block_expert, n_used, xs, moe_w_gate, moe_w_up, moe_w_down)
        x_p, x_l = _moe_combine_call(l_arr, e1, e2, r1, r2, pstart, x1, route, mod6, final_g.reshape(1, D), ys,
                                     layer == L - 1)

        ks_new.append(k_cache)
        vs_new.append(v_cache)
        s = jnp.stack([st_f[:N_PROMPT_SEQ], st_b[:N_PROMPT_SEQ]], axis=1)
        s = s.reshape(N_PROMPT_SEQ, 2, SSD_STATE, SSD_HEADS, SSD_HEAD_DIM)
        ss_new.append(jnp.transpose(s, (0, 1, 3, 4, 2)))

    y_prompt = x_p.reshape(N_PROMPT_SEQ, PROMPT_LEN, D)
    y_sample = x_l.reshape(N_LATENT_SEQ, LATENT_LEN, D)
    return (y_prompt, y_sample, jnp.stack(ks_new, axis=1), jnp.stack(vs_new, axis=1), jnp.stack(ss_new, axis=1))
```

```python
import functools
import math

import jax
import jax.numpy as jnp
import numpy as np
from jax import lax
from jax.experimental import pallas as pl
from jax.experimental.pallas import tpu as pltpu

D = 1024
DEPTH = 4
N_PROMPT_SEQ, PROMPT_LEN = 16, 256
N_LATENT_SEQ, LATENT_LEN = 8, 2048
PAST_LEN = 512
T_P = N_PROMPT_SEQ * PROMPT_LEN
T_L = N_LATENT_SEQ * LATENT_LEN
T = T_P + T_L
GRID_W = 64
EPS = 1e-6
SSD_HEADS, SSD_HEAD_DIM, SSD_STATE, SSD_GROUPS = 16, 64, 64, 2
SSD_CONV_W = 5
SSD_CHUNK = 128
SSD_XBC = D + 2 * SSD_GROUPS * SSD_STATE
CONV_CH, CONV_W = 512, 31
DA_HEADS, DA_HEAD_DIM, DA_V_DIM = 4, 64, 128
DA_QK = DA_HEADS * 2 * DA_HEAD_DIM
DA_V = DA_HEADS * DA_V_DIM
ROPE_BASE = 10000.0
MOE_GROUPS, MOE_EPG, MOE_E, MOE_HIDDEN = 4, 8, 32, 512
ROUTER_LANE0 = MOE_GROUPS

LANES = 128
SUBLANES = 8
TM = 256
N_TILES = T // TM
N_PROMPT_TILES = T_P // TM
TILES_PER_LATENT_SEQ = LATENT_LEN // TM
MOE_BLOCK = 512
N_SLOT_BLOCKS = (2 * T) // MOE_BLOCK + MOE_E
N_SLOTS = N_SLOT_BLOCKS * MOE_BLOCK
VMEM_LIMIT = 56 * 1024 * 1024

F32 = jnp.float32
BF16 = jnp.bfloat16
HI = lax.Precision.HIGHEST
NEG = -1e30


def _cparams(sem, vmem=VMEM_LIMIT):
    return pltpu.CompilerParams(dimension_semantics=sem, vmem_limit_bytes=vmem)


def _mod_row(i):
    return jnp.where(i < N_PROMPT_TILES, N_LATENT_SEQ, (i - N_PROMPT_TILES) // TILES_PER_LATENT_SEQ)


def _pair_specs(n, tile_arg=0):
    def prompt(*a):
        return (jnp.minimum(a[tile_arg], N_PROMPT_TILES - 1), 0)

    def latent(*a):
        return (jnp.maximum(a[tile_arg] - N_PROMPT_TILES, 0), 0)

    return [pl.BlockSpec((TM, n), prompt), pl.BlockSpec((TM, n), latent)]


def _pair_read(i, p_ref, l_ref):
    return jnp.where(i < N_PROMPT_TILES, p_ref[...], l_ref[...])


def _silu(x):
    return x * (1.0 / (1.0 + jnp.exp(-x)))


def _sigmoid(x):
    return 1.0 / (1.0 + jnp.exp(-x))


def _softplus(x):
    return jnp.maximum(x, 0.0) + jnp.log(1.0 + jnp.exp(-jnp.abs(x)))


def _dot(a, b, **kw):
    return jnp.dot(a, b, preferred_element_type=F32, **kw)


def _split3(x):
    hi = x.astype(BF16)
    r1 = x - hi.astype(F32)
    mid = r1.astype(BF16)
    lo = (r1 - mid.astype(F32)).astype(BF16)
    return jnp.concatenate([hi, mid, lo], axis=1)


def _dot_nt(a, b):
    return lax.dot_general(a, b, (((1,), (1,)), ((), ())), preferred_element_type=F32)


def _ada_kernel(c_ref, w_ref, b_ref, o_ref):
    cs = _silu(c_ref[...])
    o_ref[...] = _dot(cs, w_ref[...], precision=HI) + b_ref[...]


def _ada_call(cvec, w_ada, b_ada):
    nj = 6
    return pl.pallas_call(
        _ada_kernel,
        out_shape=jax.ShapeDtypeStruct((DEPTH, 16, 6 * D), F32),
        grid=(DEPTH, nj),
        in_specs=[
            pl.BlockSpec((16, D), lambda l, j: (0, 0)),
            pl.BlockSpec((None, D, D), lambda l, j: (l, 0, j)),
            pl.BlockSpec((None, 1, D), lambda l, j: (l, 0, j)),
        ],
        out_specs=pl.BlockSpec((None, 16, D), lambda l, j: (l, 0, j)),
        compiler_params=_cparams(("arbitrary", "arbitrary")),
        name="ada_mod",
    )(cvec, w_ada, b_ada.reshape(DEPTH, 1, 6 * D))


_C_Z = (0, D)
_C_XBC = (_C_Z[1], _C_Z[1] + SSD_XBC)
_C_DT = (_C_XBC[1], _C_XBC[1] + LANES)
_C_GLU = (_C_DT[1], _C_DT[1] + 2 * CONV_CH)
_C_Q = (_C_GLU[1], _C_GLU[1] + DA_QK)
_C_K = (_C_Q[1], _C_Q[1] + DA_QK)
_C_V = (_C_K[1], _C_K[1] + DA_V)
N_PROJ = _C_V[1]


def _swap16(x):
    cols = []
    for c in range(x.shape[1] // LANES):
        xc = x[:, c * LANES:(c + 1) * LANES]
        lane = lax.broadcasted_iota(jnp.int32, xc.shape, 1)
        from_right = pltpu.roll(xc, LANES - 16, axis=1)
        from_left = pltpu.roll(xc, 16, axis=1)
        cols.append(jnp.where((lane >> 4) % 2 == 0, from_right, from_left))
    return jnp.concatenate(cols, axis=1)


def _inproj_kernel(l_ref, xp_ref, xl_ref, mod_ref, g_ref, w_ref, b_ref, cos_ref, sin_ref,
                   h_ref, z_ref, xbc_ref, dt_ref, u_ref, q_ref, k_ref, v_ref, kc_ref, vc_ref):
    i = pl.program_id(0)
    x = _pair_read(i, xp_ref, xl_ref)
    xn = x * lax.rsqrt(jnp.mean(x * x, axis=-1, keepdims=True) + EPS)
    h = xn * g_ref[...] * (1.0 + mod_ref[1:2, :]) + mod_ref[0:1, :]
    hb = h.astype(BF16)
    h_ref[...] = hb

    def proj(c):
        return _dot(hb, w_ref[:, c[0]:c[1]]) + b_ref[:, c[0]:c[1]]

    z_ref[...] = proj(_C_Z).astype(BF16)
    xbc_ref[...] = proj(_C_XBC)
    dt_ref[...] = proj(_C_DT)
    glu = proj(_C_GLU)
    u_ref[...] = glu[:, :CONV_CH] * _sigmoid(glu[:, CONV_CH:])
    cos = cos_ref[...]
    sin = sin_ref[...]
    q = proj(_C_Q)
    q = q * cos + _swap16(q) * sin
    q_ref[...] = (q * (DA_HEAD_DIM ** -0.5 * math.log2(math.e))).astype(BF16)
    k = proj(_C_K)
    k = k * cos + _swap16(k) * sin
    k_ref[...] = k
    v = proj(_C_V)
    v_ref[...] = v

    @pl.when(i < N_PROMPT_TILES)
    def _():
        kc_ref[...] = k.reshape(TM, DA_HEADS, LANES)
        vc_ref[...] = v.reshape(TM, DA_HEADS, LANES)


def _inproj_call(l_arr, x_p, x_l, mod6, norm1_g, w_proj, b_proj, cos_tab, sin_tab):
    tok = lambda n: pl.BlockSpec((TM, n), lambda i, l: (i, 0))
    rope_blk = lambda i, l: (jnp.where(i < N_PROMPT_TILES, TILES_PER_LATENT_SEQ,
                                       (i - N_PROMPT_TILES) % TILES_PER_LATENT_SEQ), 0)
    lay = lambda *shape: pl.BlockSpec((None,) + shape, lambda i, l: (l[0],) + (0,) * len(shape))
    grid_spec = pltpu.PrefetchScalarGridSpec(
        num_scalar_prefetch=1,
        grid=(N_TILES,),
        in_specs=_pair_specs(D) + [
            pl.BlockSpec((None, None, 6, D), lambda i, l: (l[0], _mod_row(i), 0, 0)),
            lay(1, D), lay(D, N_PROJ), lay(1, N_PROJ),
            pl.BlockSpec((TM, DA_QK), rope_blk), pl.BlockSpec((TM, DA_QK), rope_blk),
        ],
        out_specs=[tok(D), tok(D), tok(SSD_XBC), tok(LANES), tok(CONV_CH), tok(DA_QK), tok(DA_QK), tok(DA_V)]
        + [pl.BlockSpec((None, PROMPT_LEN, DA_HEADS, LANES),
                        lambda i, l: (jnp.minimum(i, N_PROMPT_SEQ - 1), 0, 0, 0))] * 2,
    )
    sds = lambda n, dt: jax.ShapeDtypeStruct((T, n), dt)
    cache = jax.ShapeDtypeStruct((N_PROMPT_SEQ, PROMPT_LEN, DA_HEADS, LANES), F32)
    return pl.pallas_call(
        _inproj_kernel,
        grid_spec=grid_spec,
        out_shape=[sds(D, BF16), sds(D, BF16), sds(SSD_XBC, F32), sds(LANES, F32), sds(CONV_CH, F32),
                   sds(DA_QK, BF16), sds(DA_QK, F32), sds(DA_V, F32), cache, cache],
        compiler_params=_cparams(("arbitrary",)),
        name="inproj",
    )(l_arr, x_p, x_l, mod6, norm1_g, w_proj, b_proj, cos_tab, sin_tab)


SSD_STEP = TM
CHUNKS_PER_STEP = SSD_STEP // SSD_CHUNK
N_SSD_STEPS = T // SSD_STEP
N_PROMPT_STEPS = T_P // SSD_STEP
STEPS_PER_PROMPT = PROMPT_LEN // SSD_STEP
STEPS_PER_LATENT = LATENT_LEN // SSD_STEP
HALO = SUBLANES
STATE_DUMP = N_PROMPT_SEQ


def _ssd_tables(fwd):
    cidx, flags, h0, so = (np.zeros((N_SSD_STEPS,), np.int32) for _ in range(4))
    for j in range(N_SSD_STEPS):
        c = j if fwd else N_SSD_STEPS - 1 - j
        if c < N_PROMPT_STEPS:
            seq, pos, n = c // STEPS_PER_PROMPT, c % STEPS_PER_PROMPT, STEPS_PER_PROMPT
            h0i, soi = N_LATENT_SEQ, seq
        else:
            cc = c - N_PROMPT_STEPS
            seq, pos, n = cc // STEPS_PER_LATENT, cc % STEPS_PER_LATENT, STEPS_PER_LATENT
            h0i, soi = seq, STATE_DUMP
        first = pos == 0 if fwd else pos == n - 1
        cidx[j] = c
        flags[j] = int(first) | (int(pos > 0) << 1) | (int(pos < n - 1) << 2)
        h0[j] = h0i
        so[j] = soi
    return [jnp.asarray(a) for a in (cidx, flags, h0, so)]


def _ssd_conv(flags, xc_ref, xp_ref, xn_ref, cw_ref, cb_ref, xpad):
    q = SSD_STEP
    xpad[0:HALO, :] = jnp.where(((flags >> 1) & 1) == 1, xp_ref[...], 0.0)
    xpad[HALO:HALO + q, :] = xc_ref[...]
    xpad[HALO + q:HALO + q + HALO, :] = jnp.where(((flags >> 2) & 1) == 1, xn_ref[...], 0.0)
    acc = jnp.zeros((q, SSD_XBC), F32) + cb_ref[...]
    pad = SSD_CONV_W // 2
    xp = xpad[...]
    rows = q + 2 * HALO
    for k in range(SSD_CONV_W):
        shifted = xp if k == pad else pltpu.roll(xp, (pad - k) % rows, axis=0)
        acc = acc + cw_ref[k:k + 1, :] * shifted[HALO:HALO + q, :]
    return _silu(acc)


def _ssd_scan_step(fwd, flags, xc, dt_ref, dtb_ref, alog_ref, dskip_ref, e_ref, h0_ref, y_ref, so_ref, state, lat_s):
    @pl.when((flags & 1) == 1)
    def _():
        state[...] = h0_ref[...]

    dt = _softplus(dt_ref[...] + dtb_ref[...])
    da = dt * (-jnp.exp(alog_ref[...]))
    st = state[...]
    for k in (range(CHUNKS_PER_STEP) if fwd else reversed(range(CHUNKS_PER_STEP))):
        rows = slice(k * SSD_CHUNK, (k + 1) * SSD_CHUNK)
        st = _ssd_scan_chunk(fwd, xc[rows], dt[rows], da[rows], st, dskip_ref, e_ref, y_ref, rows, lat_s.at[k])
    state[...] = st
    so_ref[...] = st


def _ssd_scan_chunk(fwd, xc, dt, da, st_all, dskip_ref, e_ref, y_ref, rows, lat_s):
    q = SSD_CHUNK
    ho = 0 if fwd else SSD_HEADS
    xs = xc[:, :D]
    bm = xc[:, D:D + LANES]
    cm = xc[:, D + LANES:D + 2 * LANES]
    row = lax.broadcasted_iota(jnp.int32, (q, q), 0)
    col = lax.broadcasted_iota(jnp.int32, (q, q), 1)
    tri = (col <= row) if fwd else (col >= row)
    p = _dot(jnp.where(tri, 1.0, 0.0).astype(BF16), _split3(da))
    la = ((p[:, :LANES] + p[:, LANES:2 * LANES]) + p[:, 2 * LANES:]) * math.log2(math.e)
    lat_s[...] = la.T
    e3 = e_ref[...]
    la_exp = _dot(_split3(la), e3)
    dt_exp = _dot(_split3(dt), e3)
    la_end = la_exp[q - 1:q, :] if fwd else la_exp[0:1, :]
    decay_end = jnp.exp2(la_end - la_exp)
    chunk_decay = jnp.exp2(la_end)
    decay_in = jnp.exp2(la_exp)
    xdt = xs * dt_exp
    xdt_b = xdt.astype(BF16)
    xdtw_b = (xdt * decay_end).astype(BF16)
    bmt = bm.T

    half = LANES // 2
    lane = lax.broadcasted_iota(jnp.int32, (q, LANES), 1)
    hpg = SSD_HEADS // SSD_GROUPS
    st_new = []
    for g in range(SSD_GROUPS):
        c_g = cm[:, g * half:(g + 1) * half].astype(BF16)
        b_g = bm[:, g * half:(g + 1) * half].astype(BF16)
        bt_g = bmt[g * half:(g + 1) * half, :].astype(BF16)
        cb = _dot_nt(c_g, b_g)
        for pp in range(hpg // 2):
            h_a = g * hpg + 2 * pp
            sl = slice(h_a * SSD_HEAD_DIM, (h_a + 2) * SSD_HEAD_DIM)
            xdt_p = xdt_b[:, sl]
            yd = []
            for h in (ho + h_a, ho + h_a + 1):
                seg = la[:, h:h + 1] - lat_s[h:h + 1, :]
                s_h = (cb * jnp.exp2(jnp.where(tri, seg, NEG))).astype(BF16)
                yd.append(_dot(s_h, xdt_p))
            y_diag = jnp.where(lane < half, yd[0], yd[1])
            st_in = st_all[:, sl]
            y = y_diag + _dot(c_g, st_in.astype(BF16)) * decay_in[:, sl]
            st_new.append(st_in * chunk_decay[:, sl] + _dot(bt_g, xdtw_b[:, sl]))
            if fwd:
                y = y + xs[:, sl] * dskip_ref[:, sl]
            y_ref[rows, sl] = y.astype(BF16)
    return jnp.concatenate(st_new, axis=1)


def _ssd_fwd_kernel(l_ref, cidx_ref, flags_ref, h0i_ref, soi_ref,
                    xc_ref, xp_ref, xn_ref, cw_ref, cb_ref, dt_ref, dtb_ref, alog_ref, dskip_ref, e_ref, h0_ref,
                    y_ref, so_ref, xco_ref, state, lat_s, xpad):
    flags = flags_ref[pl.program_id(0)]
    xc = _ssd_conv(flags, xc_ref, xp_ref, xn_ref, cw_ref, cb_ref, xpad)
    xco_ref[...] = xc
    _ssd_scan_step(True, flags, xc, dt_ref, dtb_ref, alog_ref, dskip_ref, e_ref, h0_ref, y_ref, so_ref, state, lat_s)


def _ssd_bwd_kernel(l_ref, cidx_ref, flags_ref, h0i_ref, soi_ref,
                    xc_ref, dt_ref, dtb_ref, alog_ref, e_ref, h0_ref, y_ref, so_ref, state, lat_s):
    flags = flags_ref[pl.program_id(0)]
    _ssd_scan_step(False, flags, xc_ref[...], dt_ref, dtb_ref, alog_ref, None, e_ref, h0_ref, y_ref, so_ref,
                   state, lat_s)


def _ssd_call(l_arr, tabs, xbc, dt, conv_w, conv_b, dt_bias, a_log, dskip, e_mat, h0t, xc_in, fwd):
    nb8 = T // HALO
    per = SSD_STEP // HALO
    cur = lambda j, l, ci, fl, h0, so: (ci[j], 0)
    prev = lambda j, l, ci, fl, h0, so: (jnp.maximum(ci[j] * per - 1, 0), 0)
    nxt = lambda j, l, ci, fl, h0, so: (jnp.minimum((ci[j] + 1) * per, nb8 - 1), 0)
    lay = lambda *shape: pl.BlockSpec((None,) + shape, lambda j, l, *_: (l[0],) + (0,) * len(shape))
    chunk = lambda n: pl.BlockSpec((SSD_STEP, n), cur)
    common_in = [chunk(LANES), lay(1, LANES), lay(1, LANES)]
    tail_in = [pl.BlockSpec((3 * LANES, D), lambda j, *_: (0, 0)),
               pl.BlockSpec((None, None, None, SSD_STATE, D),
                            lambda j, l, ci, fl, h0, so: (l[0], h0[j], 0 if fwd else 1, 0, 0))]
    out_specs = [chunk(D), pl.BlockSpec((None, SSD_STATE, D), lambda j, l, ci, fl, h0, so: (so[j], 0, 0))]
    out_shape = [jax.ShapeDtypeStruct((T, D), BF16), jax.ShapeDtypeStruct((N_PROMPT_SEQ + 1, SSD_STATE, D), F32)]
    scratch = [pltpu.VMEM((SSD_STATE, D), F32), pltpu.VMEM((CHUNKS_PER_STEP, LANES, SSD_CHUNK), F32)]
    if fwd:
        in_specs = ([chunk(SSD_XBC), pl.BlockSpec((HALO, SSD_XBC), prev), pl.BlockSpec((HALO, SSD_XBC), nxt),
                     lay(SSD_CONV_W, SSD_XBC), lay(1, SSD_XBC)] + common_in + [lay(1, D)] + tail_in)
        args = (xbc, xbc, xbc, conv_w, conv_b, dt, dt_bias, a_log, dskip, e_mat, h0t)
        out_specs.append(chunk(SSD_XBC))
        out_shape.append(jax.ShapeDtypeStruct((T, SSD_XBC), F32))
        scratch.append(pltpu.VMEM((SSD_STEP + 2 * HALO, SSD_XBC), F32))
    else:
        in_specs = [chunk(SSD_XBC)] + common_in + tail_in
        args = (xc_in, dt, dt_bias, a_log, e_mat, h0t)
    grid_spec = pltpu.PrefetchScalarGridSpec(num_scalar_prefetch=5, grid=(N_SSD_STEPS,), in_specs=in_specs,
                                             out_specs=out_specs, scratch_shapes=scratch)
    return pl.pallas_call(
        _ssd_fwd_kernel if fwd else _ssd_bwd_kernel,
        grid_spec=grid_spec,
        out_shape=out_shape,
        compiler_params=_cparams(("arbitrary",)),
        name="ssd_scan_fwd" if fwd else "ssd_scan_bwd",
    )(l_arr, *tabs, *args)


CV_HALO = 16


def _conformer_conv(i, uc_ref, up_ref, un_ref, w_ref, b_ref, g_ref, beta_ref, upad):
    pos = (i - N_PROMPT_TILES) % TILES_PER_LATENT_SEQ
    is_prompt = i < N_PROMPT_TILES
    no_l = jnp.logical_or(is_prompt, pos == 0)
    no_r = jnp.logical_or(is_prompt, pos == TILES_PER_LATENT_SEQ - 1)
    upad[0:CV_HALO, :] = jnp.where(no_l, 0.0, up_ref[...])
    upad[CV_HALO:CV_HALO + TM, :] = uc_ref[...]
    upad[CV_HALO + TM:CV_HALO + TM + CV_HALO, :] = jnp.where(no_r, 0.0, un_ref[...])
    acc = jnp.zeros((TM, CONV_CH), F32) + b_ref[...]
    pad = CONV_W // 2
    up = upad[...]
    rows = TM + 2 * CV_HALO
    for rot in range(SUBLANES):
        taps = [k for k in range(CONV_W) if (CV_HALO - pad + k) % SUBLANES == rot]
        rolled = up if rot == 0 else pltpu.roll(up, rows - rot, axis=0)
        for k in taps:
            base = CV_HALO - pad + k - rot
            acc = acc + w_ref[k:k + 1, :] * rolled[base:base + TM, :]
    mu = jnp.mean(acc, axis=-1, keepdims=True)
    xc = acc - mu
    var = jnp.mean(xc * xc, axis=-1, keepdims=True)
    y = xc * lax.rsqrt(var + EPS) * g_ref[...] + beta_ref[...]
    return _silu(y).astype(BF16)


def _conformer_conv_specs():
    per = TM // CV_HALO
    nb = T // CV_HALO
    return [pl.BlockSpec((TM, CONV_CH), lambda i, l: (i, 0)),
            pl.BlockSpec((CV_HALO, CONV_CH), lambda i, l: (jnp.maximum(i * per - 1, 0), 0)),
            pl.BlockSpec((CV_HALO, CONV_CH), lambda i, l: (jnp.minimum((i + 1) * per, nb - 1), 0))]


def _lambda_terms(l_ref, lam_ref):
    lf = jnp.full((1, 1), l_ref[0], jnp.int32).astype(F32)
    lam_init = 0.8 - 0.6 * jnp.exp(-0.3 * lf)
    p = lam_ref[...]
    s1 = jnp.sum(p[0:1, :] * p[1:2, :], axis=-1, keepdims=True)
    s2 = jnp.sum(p[2:3, :] * p[3:4, :], axis=-1, keepdims=True)
    lam = jnp.exp(s1) - jnp.exp(s2) + lam_init
    return lam, 1.0 - lam_init


def _attn_body(l_ref, q_ref, k_ref, v_ref, ck_ref, cv_ref, lam_ref, g_ref, o_ref, k_s, v_s, n_ctx):
    @pl.when(pl.program_id(1) == 0)
    def _():
        k_s[n_ctx:, :] = k_ref[...].astype(BF16)
        for h in range(DA_HEADS):
            sl = slice(h * LANES, (h + 1) * LANES)
            if n_ctx:
                k_s[0:n_ctx, sl] = ck_ref[:, h, :].astype(BF16)
                v_s[0:n_ctx, 2 * h * LANES:(2 * h + 1) * LANES] = cv_ref[:, h, :].astype(BF16)
            v_s[n_ctx:, 2 * h * LANES:(2 * h + 1) * LANES] = v_ref[:, sl].astype(BF16)
            v_s[:, (2 * h + 1) * LANES:(2 * h + 2) * LANES] = jnp.ones((v_s.shape[0], LANES), BF16)

    lam, out_scale = _lambda_terms(l_ref, lam_ref)
    tq = q_ref.shape[0]
    lane = lax.broadcasted_iota(jnp.int32, (tq, LANES), 1)
    zero = jnp.zeros((tq, LANES), BF16)
    for h in range(DA_HEADS):
        sl = slice(h * LANES, (h + 1) * LANES)
        qh = q_ref[:, sl]
        kh = k_s[:, sl]
        vh = v_s[:, 2 * h * LANES:(2 * h + 2) * LANES]
        outs = []
        for c in range(2):
            in_c = (lane < DA_HEAD_DIM) if c == 0 else (lane >= DA_HEAD_DIM)
            s = _dot_nt(jnp.where(in_c, qh, zero), kh)
            m = jnp.max(s, axis=-1, keepdims=True)
            pv = _dot(jnp.exp2(s - m).astype(BF16), vh)
            outs.append(pv[:, :LANES] / pv[:, LANES:])
        o = outs[0] - lam * outs[1]
        o = o * lax.rsqrt(jnp.mean(o * o, axis=-1, keepdims=True) + EPS)
        o_ref[:, sl] = (o * g_ref[...] * out_scale).astype(BF16)


def _attn_prompt_kernel(l_ref, q_ref, k_ref, v_ref, lam_ref, g_ref, o_ref, k_s, v_s):
    _attn_body(l_ref, q_ref, k_ref, v_ref, None, None, lam_ref, g_ref, o_ref, k_s, v_s, 0)


def _attn_latent_kernel(l_ref, q_ref, k_ref, v_ref, ck_ref, cv_ref, lam_ref, g_ref, o_ref, k_s, v_s):
    _attn_body(l_ref, q_ref, k_ref, v_ref, ck_ref, cv_ref, lam_ref, g_ref, o_ref, k_s, v_s, PAST_LEN)


def _attn_call(l_arr, q, k, v, cache_k, cache_v, da_lambda, subln_g, latent):
    if latent:
        nseq, seqlen, n_ctx = N_LATENT_SEQ, LATENT_LEN, PAST_LEN
        tile0, seq0 = N_PROMPT_TILES, T_P // LATENT_LEN
    else:
        nseq, seqlen, n_ctx = N_PROMPT_SEQ, PROMPT_LEN, 0
        tile0, seq0 = 0, 0
    nq = seqlen // TM
    lay = lambda *shape: pl.BlockSpec((None,) + shape, lambda b, i, l: (l[0],) + (0,) * len(shape))
    in_specs = [
        pl.BlockSpec((TM, DA_QK), lambda b, i, l: (tile0 + b * nq + i, 0)),
        pl.BlockSpec((seqlen, DA_QK), lambda b, i, l: (seq0 + b, 0)),
        pl.BlockSpec((seqlen, DA_V), lambda b, i, l: (seq0 + b, 0)),
    ]
    args = [q, k, v]
    if latent:
        ctx = pl.BlockSpec((None, None, PAST_LEN, DA_HEADS, LANES), lambda b, i, l: (b, l[0], 0, 0, 0))
        in_specs += [ctx, ctx]
        args += [cache_k, cache_v]
    in_specs += [lay(4, DA_HEAD_DIM), lay(1, DA_V_DIM)]
    args += [da_lambda, subln_g]
    grid_spec = pltpu.PrefetchScalarGridSpec(
        num_scalar_prefetch=1,
        grid=(nseq, nq),
        in_specs=in_specs,
        out_specs=pl.BlockSpec((TM, DA_V), lambda b, i, l: (b * nq + i, 0)),
        scratch_shapes=[pltpu.VMEM((n_ctx + seqlen, DA_QK), BF16), pltpu.VMEM((n_ctx + seqlen, 2 * DA_V), BF16)],
    )
    return pl.pallas_call(
        _attn_latent_kernel if latent else _attn_prompt_kernel,
        grid_spec=grid_spec,
        out_shape=jax.ShapeDtypeStruct((nseq * seqlen, DA_V), BF16),
        compiler_params=_cparams(("arbitrary", "arbitrary")),
        name="diff_attn_latent" if latent else "diff_attn_prompt",
    )(l_arr, *args)


def _combine_kernel(l_ref, xp_ref, xl_ref, h_ref, yf_ref, yb_ref, z_ref, u_ref, up_ref, un_ref, oap_ref, oal_ref,
                    mod_ref, cvw_ref, cvb_ref, cvg_ref, cvbeta_ref,
                    wg_ref, bg_ref, sg_ref, wa_ref, wb_ref, wc_ref, wo_ref, g2_ref, wr_ref, br_ref,
                    x1_ref, h2_ref, route_ref, cnt_ref, e1_ref, e2_ref, r1_ref, r2_ref, carry, upad):
    i = pl.program_id(0)

    @pl.when(i == 0)
    def _():
        carry[...] = jnp.zeros_like(carry)

    uc = _conformer_conv(i, u_ref, up_ref, un_ref, cvw_ref, cvb_ref, cvg_ref, cvbeta_ref, upad)

    y = (yf_ref[...].astype(F32) + yb_ref[...].astype(F32)) * _silu(z_ref[...].astype(F32))
    y = y * lax.rsqrt(jnp.mean(y * y, axis=-1, keepdims=True) + EPS) * sg_ref[...]
    br_a = _dot(y.astype(BF16), wa_ref[...])
    br_b = _dot(uc, wb_ref[...])
    br_c = _dot(_pair_read(i, oap_ref, oal_ref), wc_ref[...])
    hb = h_ref[...]

    def gate(n):
        return _sigmoid(_dot(hb, wg_ref[:, n * D:(n + 1) * D]) + bg_ref[:, n * D:(n + 1) * D])

    mix = gate(0) * br_a + gate(1) * br_b + gate(2) * br_c
    mixed = _dot(mix.astype(BF16), wo_ref[...])
    x1 = _pair_read(i, xp_ref, xl_ref) + mod_ref[2:3, :] * mixed
    x1_ref[...] = x1
    xn = x1 * lax.rsqrt(jnp.mean(x1 * x1, axis=-1, keepdims=True) + EPS)
    h2 = xn * g2_ref[...] * (1.0 + mod_ref[4:5, :]) + mod_ref[3:4, :]
    h2_ref[...] = h2

    h2_hi = h2.astype(BF16)
    h2_mid = (h2 - h2_hi.astype(F32)).astype(BF16)
    logits = _dot(jnp.concatenate([h2_hi, h2_mid, h2_hi], axis=1), wr_ref[...]) + br_ref[...]
    lane = lax.broadcasted_iota(jnp.int32, (TM, LANES), 1).astype(F32)

    def first_argmax(vals, vmax):
        return jnp.min(jnp.where(vals == vmax, lane, float(LANES)), axis=-1, keepdims=True)

    glog = jnp.where(lane < MOE_GROUPS, logits, NEG)
    gmax = jnp.max(glog, axis=-1, keepdims=True)
    gsel = first_argmax(glog, gmax)
    p_g = 1.0 / jnp.sum(jnp.exp(glog - gmax), axis=-1, keepdims=True)
    lo = ROUTER_LANE0 + MOE_EPG * gsel
    elog = jnp.where(jnp.logical_and(lane >= lo, lane < lo + MOE_EPG), logits, NEG)
    v1 = jnp.max(elog, axis=-1, keepdims=True)
    i1 = first_argmax(elog, v1)
    elog2 = jnp.where(lane == i1, NEG, elog)
    v2 = jnp.max(elog2, axis=-1, keepdims=True)
    i2 = first_argmax(elog2, v2)
    t2 = jnp.exp(v2 - v1)
    w1 = p_g / (1.0 + t2)
    w2 = p_g * t2 / (1.0 + t2)

    oh1 = lane == i1
    oh2 = lane == i2
    both = jnp.where(jnp.logical_or(oh1, oh2), 1.0, 0.0)
    r = lax.broadcasted_iota(jnp.int32, (TM, TM), 0)
    c = lax.broadcasted_iota(jnp.int32, (TM, TM), 1)
    strict_lower = jnp.where(c < r, 1.0, 0.0).astype(BF16)
    before = _dot(strict_lower, both.astype(BF16)) + carry[0:1, :]
    r1 = jnp.sum(jnp.where(oh1, before, 0.0), axis=-1, keepdims=True)
    r2 = jnp.sum(jnp.where(oh2, before, 0.0), axis=-1, keepdims=True)
    carry[...] = carry[...] + jnp.sum(both, axis=0, keepdims=True)
    cnt_ref[...] = carry[...]

    e1 = i1 - ROUTER_LANE0
    e2 = i2 - ROUTER_LANE0
    route = jnp.zeros((TM, LANES), F32)
    for n, val in enumerate((e1, e2, w1, w2, r1, r2)):
        route = jnp.where(lane == float(n), val, route)
    route_ref[...] = route
    route_t = route.T
    for n, ref in ((0, e1_ref), (1, e2_ref), (4, r1_ref), (5, r2_ref)):
        ref[...] = route_t[n:n + 1, :].astype(jnp.int32)


def _combine_call(l_arr, x_p, x_l, h, y_f, y_b, z, u, oa_p, oa_l, mod6, cv_w, cv_b, cv_g, cv_beta, w_gates, b_gates,
                  ssd_norm_g, w_br_ssd, w_br_conv, w_br_attn, w_out, norm2_g, w_router, b_router):
    tok = lambda n: pl.BlockSpec((TM, n), lambda i, l: (i, 0))
    lay = lambda *shape: pl.BlockSpec((None,) + shape, lambda i, l: (l[0],) + (0,) * len(shape))
    grid_spec = pltpu.PrefetchScalarGridSpec(
        num_scalar_prefetch=1,
        grid=(N_TILES,),
        in_specs=_pair_specs(D) + [tok(D), tok(D), tok(D), tok(D)] + _conformer_conv_specs() + _pair_specs(DA_V) + [
            pl.BlockSpec((None, None, 6, D), lambda i, l: (l[0], _mod_row(i), 0, 0)),
            lay(CONV_W, CONV_CH), lay(1, CONV_CH), lay(1, CONV_CH), lay(1, CONV_CH),
            lay(D, 3 * D), lay(1, 3 * D), lay(1, D), lay(D, D), lay(CONV_CH, D), lay(DA_V, D), lay(D, D),
            lay(1, D), lay(3 * D, LANES), lay(1, LANES),
        ],
        out_specs=[tok(D), tok(D), tok(LANES), pl.BlockSpec((SUBLANES, LANES), lambda i, l: (0, 0))]
        + [pl.BlockSpec((None, 1, TM), lambda i, l: (i, 0, 0))] * 4,
        scratch_shapes=[pltpu.VMEM((SUBLANES, LANES), F32), pltpu.VMEM((TM + 2 * CV_HALO, CONV_CH), F32)],
    )
    return pl.pallas_call(
        _combine_kernel,
        grid_spec=grid_spec,
        out_shape=[jax.ShapeDtypeStruct((T, D), F32), jax.ShapeDtypeStruct((T, D), F32),
                   jax.ShapeDtypeStruct((T, LANES), F32), jax.ShapeDtypeStruct((SUBLANES, LANES), F32)]
        + [jax.ShapeDtypeStruct((N_TILES, 1, TM), jnp.int32)] * 4,
        compiler_params=_cparams(("arbitrary",)),
        name="branch_combine_router",
    )(l_arr, x_p, x_l, h, y_f, y_b, z, u, u, u, oa_p, oa_l, mod6, cv_w, cv_b, cv_g, cv_beta, w_gates, b_gates,
      ssd_norm_g, w_br_ssd, w_br_conv, w_br_attn, w_out, norm2_g, w_router, b_router)


ROW_DMA_GROUP = 64


def _for_each_row(fn):
    def body(t, carry):
        fn(t)
        return carry

    lax.fori_loop(0, TM, body, 0, unroll=ROW_DMA_GROUP)


DISPATCH_SLOTS = 3


def _dispatch_kernel(e1_ref, e2_ref, r1_ref, r2_ref, ps_ref, pe_ref, h2_ref, xs_ref, zbuf, hbuf, in_sem, sem, zsem):
    i = pl.program_id(0)
    base = i * TM
    slot = i % DISPATCH_SLOTS

    def fetch(tile, s):
        return pltpu.make_async_copy(h2_ref.at[pl.ds(pl.multiple_of(tile * TM, TM), TM), :], hbuf.at[s], in_sem.at[s])

    def wait_rows(s):
        for _ in range(2):
            pltpu.make_async_copy(hbuf.at[s], xs_ref.at[pl.ds(0, TM), :], sem.at[s]).wait()

    @pl.when(i == 0)
    def _():
        fetch(0, 0).start()

    @pl.when(i == 0)
    def _():
        zbuf[...] = jnp.zeros_like(zbuf)

        def zero_fill(e):
            off = pl.multiple_of(pe_ref[e] - MOE_BLOCK, MOE_BLOCK)
            return pltpu.make_async_copy(zbuf, xs_ref.at[pl.ds(off, MOE_BLOCK), :], zsem)

        def unused_fill(b):
            off = pl.multiple_of(b * MOE_BLOCK, MOE_BLOCK)
            return pltpu.make_async_copy(zbuf, xs_ref.at[pl.ds(off, MOE_BLOCK), :], zsem)

        n_used = pe_ref[MOE_E - 1] // MOE_BLOCK
        for e in range(MOE_E):
            @pl.when(pe_ref[e] > ps_ref[e])
            def _():
                zero_fill(e).start()
        lax.fori_loop(n_used, N_SLOT_BLOCKS, lambda b, c: (unused_fill(b).start(), c)[1], 0)
        for e in range(MOE_E):
            @pl.when(pe_ref[e] > ps_ref[e])
            def _():
                zero_fill(e).wait()
        lax.fori_loop(n_used, N_SLOT_BLOCKS, lambda b, c: (unused_fill(b).wait(), c)[1], 0)

    @pl.when(i >= DISPATCH_SLOTS - 1)
    def _():
        wait_rows((i + 1) % DISPATCH_SLOTS)

    @pl.when(i + 1 < N_TILES)
    def _():
        fetch(i + 1, (i + 1) % DISPATCH_SLOTS).start()

    fetch(i, slot).wait()

    def issue(t):
        for e_ref, r_ref in ((e1_ref, r1_ref), (e2_ref, r2_ref)):
            dest = ps_ref[e_ref[base + t]] + r_ref[base + t]
            pltpu.make_async_copy(hbuf.at[slot, pl.ds(t, 1), :], xs_ref.at[pl.ds(dest, 1), :], sem.at[slot]).start()

    _for_each_row(issue)

    @pl.when(i == N_TILES - 1)
    def _():
        for back in range(DISPATCH_SLOTS - 1):
            wait_rows((i - back) % DISPATCH_SLOTS)


def _dispatch_call(e1, e2, r1, r2, pstart, pend, h2):
    grid_spec = pltpu.PrefetchScalarGridSpec(
        num_scalar_prefetch=6,
        grid=(N_TILES,),
        in_specs=[pl.BlockSpec(memory_space=pl.ANY)],
        out_specs=pl.BlockSpec(memory_space=pl.ANY),
        scratch_shapes=[pltpu.VMEM((MOE_BLOCK, D), F32), pltpu.VMEM((DISPATCH_SLOTS, TM, D), F32),
                        pltpu.SemaphoreType.DMA((DISPATCH_SLOTS,)), pltpu.SemaphoreType.DMA((DISPATCH_SLOTS,)),
                        pltpu.SemaphoreType.DMA(())],
    )
    return pl.pallas_call(
        _dispatch_kernel,
        grid_spec=grid_spec,
        out_shape=jax.ShapeDtypeStruct((N_SLOTS, D), F32),
        compiler_params=_cparams(("arbitrary",)),
        name="moe_dispatch",
    )(e1, e2, r1, r2, pstart, pend, h2)


def _moe_kernel(l_ref, be_ref, nb_ref, xs_ref, wg_ref, wu_ref, wd_ref, ys_ref, wg_s, wu_s, wd_s):
    i = pl.program_id(0)
    prev = be_ref[jnp.maximum(i - 1, 0)]

    @pl.when(jnp.logical_or(i == 0, be_ref[i] != prev))
    def _():
        wg_s[...] = wg_ref[...].astype(BF16)
        wu_s[...] = wu_ref[...].astype(BF16)
        wd_s[...] = wd_ref[...].astype(BF16)

    @pl.when(i < nb_ref[0])
    def _():
        xb = xs_ref[...].astype(BF16)
        hid = _silu(_dot(xb, wg_s[...])) * _dot(xb, wu_s[...])
        ys_ref[...] = _dot(hid.astype(BF16), wd_s[...])

    @pl.when(i >= nb_ref[0])
    def _():
        ys_ref[...] = jnp.zeros_like(ys_ref)


def _moe_call(l_arr, block_expert, n_used, xs, w_gate, w_up, w_down):
    wspec = lambda a, b: pl.BlockSpec((None, None, a, b), lambda i, l, be, nb: (l[0], be[i], 0, 0))
    grid_spec = pltpu.PrefetchScalarGridSpec(
        num_scalar_prefetch=3,
        grid=(N_SLOT_BLOCKS,),
        in_specs=[pl.BlockSpec((MOE_BLOCK, D), lambda i, l, be, nb: (jnp.minimum(i, nb[0] - 1), 0)),
                  wspec(D, MOE_HIDDEN), wspec(D, MOE_HIDDEN), wspec(MOE_HIDDEN, D)],
        out_specs=pl.BlockSpec((MOE_BLOCK, D), lambda i, l, be, nb: (i, 0)),
        scratch_shapes=[pltpu.VMEM((D, MOE_HIDDEN), BF16), pltpu.VMEM((D, MOE_HIDDEN), BF16),
                        pltpu.VMEM((MOE_HIDDEN, D), BF16)],
    )
    return pl.pallas_call(
        _moe_kernel,
        grid_spec=grid_spec,
        out_shape=jax.ShapeDtypeStruct((N_SLOTS, D), F32),
        compiler_params=_cparams(("arbitrary",)),
        name="moe_experts",
    )(l_arr, block_expert, n_used, xs, w_gate, w_up, w_down)


def _moe_combine_kernel(l_ref, e1_ref, e2_ref, r1_ref, r2_ref, ps_ref, x1_ref, route_ref, mod_ref, fg_ref, ys_ref,
                        op_ref, ol_ref, buf, sem, *, final):
    i = pl.program_id(0)
    slot = i % 2

    def gather_tile(tile, dst_slot):
        base = tile * TM

        def issue(t):
            for which, (e_ref, r_ref) in enumerate(((e1_ref, r1_ref), (e2_ref, r2_ref))):
                src = ps_ref[e_ref[base + t]] + r_ref[base + t]
                pltpu.make_async_copy(ys_ref.at[pl.ds(src, 1), :], buf.at[dst_slot, which, pl.ds(t, 1), :],
                                      sem.at[dst_slot]).start()

        _for_each_row(issue)

    @pl.when(i == 0)
    def _():
        gather_tile(0, 0)

    @pl.when(i + 1 < N_TILES)
    def _():
        gather_tile(i + 1, 1 - slot)

    for which in range(2):
        pltpu.make_async_copy(ys_ref.at[pl.ds(0, TM), :], buf.at[slot, which], sem.at[slot]).wait()
    w1 = route_ref[:, 2:3]
    w2 = route_ref[:, 3:4]
    y = buf[slot, 0] * w1 + buf[slot, 1] * w2
    x2 = x1_ref[...] + mod_ref[5:6, :] * y
    if final:
        x2 = x2 * lax.rsqrt(jnp.mean(x2 * x2, axis=-1, keepdims=True) + EPS) * fg_ref[...]

    @pl.when(i < N_PROMPT_TILES)
    def _():
        op_ref[...] = x2

    @pl.when(i >= N_PROMPT_TILES)
    def _():
        ol_ref[...] = x2


def _moe_combine_call(l_arr, e1, e2, r1, r2, pstart, x1, route, mod6, final_g, ys, final):
    grid_spec = pltpu.PrefetchScalarGridSpec(
        num_scalar_prefetch=6,
        grid=(N_TILES,),
        in_specs=[
            pl.BlockSpec((TM, D), lambda i, *_: (i, 0)),
            pl.BlockSpec((TM, LANES), lambda i, *_: (i, 0)),
            pl.BlockSpec((None, None, 6, D), lambda i, l, *_: (l[0], _mod_row(i), 0, 0)),
            pl.BlockSpec((1, D), lambda i, *_: (0, 0)),
            pl.BlockSpec(memory_space=pl.ANY),
        ],
        out_specs=_pair_specs(D),
        scratch_shapes=[pltpu.VMEM((2, 2, TM, D), F32), pltpu.SemaphoreType.DMA((2,))],
    )
    return pl.pallas_call(
        functools.partial(_moe_combine_kernel, final=final),
        grid_spec=grid_spec,
        out_shape=[jax.ShapeDtypeStruct((T_P, D), F32), jax.ShapeDtypeStruct((T_L, D), F32)],
        compiler_params=_cparams(("arbitrary",)),
        name="moe_combine_final" if final else "moe_combine",
    )(l_arr, e1, e2, r1, r2, pstart, x1, route, mod6, final_g, ys)


def _rope_tables():
    n = LATENT_LEN
    rows = n // GRID_W
    row = jnp.repeat(jnp.arange(rows), GRID_W).astype(F32)
    col = jnp.tile(jnp.arange(GRID_W), rows).astype(F32)
    axis_dim = DA_HEAD_DIM // 2
    inv_freq = 1.0 / (ROPE_BASE ** (jnp.arange(0, axis_dim, 2, dtype=F32) / axis_dim))
    ar, ac = row[:, None] * inv_freq, col[:, None] * inv_freq
    cos64 = jnp.concatenate([jnp.cos(ar), jnp.cos(ar), jnp.cos(ac), jnp.cos(ac)], axis=1)
    sin64 = jnp.concatenate([-jnp.sin(ar), jnp.sin(ar), -jnp.sin(ac), jnp.sin(ac)], axis=1)
    reps = DA_QK // DA_HEAD_DIM
    cos = jnp.concatenate([jnp.tile(cos64, (1, reps)), jnp.ones((TM, DA_QK), F32)], axis=0)
    sin = jnp.concatenate([jnp.tile(sin64, (1, reps)), jnp.zeros((TM, DA_QK), F32)], axis=0)
    return cos, sin


def _pad_lanes(a, n=LANES):
    return jnp.pad(a, [(0, 0)] * (a.ndim - 1) + [(0, n - a.shape[-1])])


def kernel(x_prompt, x_sample, cache_k, cache_v, state_ssd, c, c_ctx, w_ada, b_ada, norm1_g, norm2_g, w_in, b_in,
           ssd_conv_w, ssd_conv_b, ssd_dt_bias, ssd_a_log, ssd_d, ssd_norm_g, w_br_ssd, cv_dw_w, cv_dw_b, cv_ln_g,
           cv_ln_b, w_br_conv, da_lambda, da_subln_g, w_br_attn, w_out, moe_w_group, moe_b_group, moe_w_expert,
           moe_b_expert, moe_w_gate, moe_w_up, moe_w_down, final_g):
    L = DEPTH
    x_p, x_l = x_prompt.reshape(T_P, D), x_sample.reshape(T_L, D)

    cvec = jnp.concatenate([c, c_ctx[None, :], jnp.zeros((16 - N_LATENT_SEQ - 1, D), F32)], axis=0)
    mod6 = _ada_call(cvec, w_ada, b_ada).reshape(L, 16, 6, D)

    o_z, o_xbc, o_dt = 0, D, D + SSD_XBC
    o_glu = o_dt + 2 * SSD_HEADS
    o_q = o_glu + 2 * CONV_CH
    o_k, o_v, o_g = o_q + DA_QK, o_q + 2 * DA_QK, o_q + 2 * DA_QK + DA_V

    def regroup(w):
        return jnp.concatenate([w[..., o_z:o_dt], _pad_lanes(w[..., o_dt:o_glu]), w[..., o_glu:o_g]], axis=-1)

    w_proj = regroup(w_in).astype(BF16)
    b_proj = regroup(b_in).reshape(L, 1, N_PROJ)
    w_gates = w_in[..., o_g:].astype(BF16)
    b_gates = b_in[..., o_g:].reshape(L, 1, 3 * D)
    cos_tab, sin_tab = _rope_tables()

    tabs_f, tabs_b = _ssd_tables(True), _ssd_tables(False)
    dt_bias = _pad_lanes(ssd_dt_bias.reshape(L, 1, 2 * SSD_HEADS))
    a_log = _pad_lanes(ssd_a_log.reshape(L, 1, 2 * SSD_HEADS))
    dskip = jnp.repeat(ssd_d, SSD_HEAD_DIM, axis=-1).reshape(L, 1, D)
    hp = np.arange(D) // SSD_HEAD_DIM
    e_f, e_b = (jnp.asarray(np.tile(np.arange(LANES)[:, None] == ho + hp[None, :], (3, 1)).astype(np.float32))
                .astype(BF16) for ho in (0, SSD_HEADS))
    h0t = jnp.transpose(state_ssd, (1, 0, 2, 5, 3, 4)).reshape(L, N_LATENT_SEQ, 2, SSD_STATE, D)
    h0t = jnp.concatenate([h0t, jnp.zeros((L, 1, 2, SSD_STATE, D), F32)], axis=1)

    w_router = _pad_lanes(jnp.concatenate([moe_w_group, moe_w_expert], axis=-1))
    wr_hi = w_router.astype(BF16)
    wr_mid = (w_router - wr_hi.astype(F32)).astype(BF16)
    w_router = jnp.concatenate([wr_hi, wr_hi, wr_mid], axis=1)
    b_router = _pad_lanes(jnp.concatenate([moe_b_group, moe_b_expert], axis=-1)).reshape(L, 1, LANES)
    w_br_ssd_b, w_br_conv_b = w_br_ssd.astype(BF16), w_br_conv.astype(BF16)
    w_br_attn_b, w_out_b = w_br_attn.astype(BF16), w_out.astype(BF16)
    r3 = lambda a: a.reshape(L, 1, a.shape[-1])

    ks_new, vs_new, ss_new = [], [], []
    for layer in range(L):
        l_arr = jnp.full((1,), layer, jnp.int32)
        h, z, xbc, dt, u, q, k, v, k_cache, v_cache = _inproj_call(l_arr, x_p, x_l, mod6, r3(norm1_g), w_proj, b_proj,
                                                                   cos_tab, sin_tab)
        y_f, st_f, xc = _ssd_call(l_arr, tabs_f, xbc, dt, ssd_conv_w, r3(ssd_conv_b), dt_bias, a_log, dskip,
                                  e_f, h0t, None, True)
        y_b, st_b = _ssd_call(l_arr, tabs_b, None, dt, None, None, dt_bias, a_log, None, e_b, h0t, xc, False)
        oa_p = _attn_call(l_arr, q, k, v, None, None, da_lambda, r3(da_subln_g), latent=False)
        oa_l = _attn_call(l_arr, q, k, v, cache_k, cache_v, da_lambda, r3(da_subln_g), latent=True)
        x1, h2, route, counts, e1, e2, r1, r2 = _combine_call(
            l_arr, x_p, x_l, h, y_f, y_b, z, u, oa_p, oa_l, mod6, cv_dw_w, r3(cv_dw_b), r3(cv_ln_g), r3(cv_ln_b),
            w_gates, b_gates, r3(ssd_norm_g), w_br_ssd_b, w_br_conv_b, w_br_attn_b, w_out_b, r3(norm2_g),
            w_router, b_router)
        e1, e2, r1, r2 = (a.reshape(T) for a in (e1, e2, r1, r2))
        cnt = counts[0, ROUTER_LANE0:ROUTER_LANE0 + MOE_E].astype(jnp.int32)
        pcnt = (cnt + MOE_BLOCK - 1) // MOE_BLOCK * MOE_BLOCK
        pend = jnp.cumsum(pcnt)
        pstart = pend - pcnt
        blk0 = jnp.arange(N_SLOT_BLOCKS, dtype=jnp.int32) * MOE_BLOCK
        block_expert = jnp.minimum(jnp.sum((pend[None, :] <= blk0[:, None]).astype(jnp.int32), axis=1), MOE_E - 1)
        n_used = pend[-1:] // MOE_BLOCK

        xs = _dispatch_call(e1, e2, r1, r2, pstart, pend, h2)
        ys = _moe_call(l_arr, block_expert, n_used, xs, moe_w_gate, moe_w_up, moe_w_down)
        x_p, x_l = _moe_combine_call(l_arr, e1, e2, r1, r2, pstart, x1, route, mod6, final_g.reshape(1, D), ys,
                                     layer == L - 1)

        ks_new.append(k_cache)
        vs_new.append(v_cache)
        s = jnp.stack([st_f[:N_PROMPT_SEQ], st_b[:N_PROMPT_SEQ]], axis=1)
        s = s.reshape(N_PROMPT_SEQ, 2, SSD_STATE, SSD_HEADS, SSD_HEAD_DIM)
        ss_new.append(jnp.transpose(s, (0, 1, 3, 4, 2)))

    y_prompt = x_p.reshape(N_PROMPT_SEQ, PROMPT_LEN, D)
    y_sample = x_l.reshape(N_LATENT_SEQ, LATENT_LEN, D)
    return (y_prompt, y_sample, jnp.stack(ks_new, axis=1), jnp.stack(vs_new, axis=1), jnp.stack(ss_new, axis=1))
```

```python
import functools
import math

import jax
import jax.numpy as jnp
import numpy as np
from jax import lax
from jax.experimental import pallas as pl
from jax.experimental.pallas import tpu as pltpu

D = 1024
DEPTH = 4
N_PROMPT_SEQ, PROMPT_LEN = 16, 256
N_LATENT_SEQ, LATENT_LEN = 8, 2048
PAST_LEN = 512
T_P = N_PROMPT_SEQ * PROMPT_LEN
T_L = N_LATENT_SEQ * LATENT_LEN
T = T_P + T_L
GRID_W = 64
EPS = 1e-6
SSD_HEADS, SSD_HEAD_DIM, SSD_STATE, SSD_GROUPS = 16, 64, 64, 2
SSD_CONV_W = 5
SSD_CHUNK = 128
SSD_XBC = D + 2 * SSD_GROUPS * SSD_STATE
CONV_CH, CONV_W = 512, 31
DA_HEADS, DA_HEAD_DIM, DA_V_DIM = 4, 64, 128
DA_QK = DA_HEADS * 2 * DA_HEAD_DIM
DA_V = DA_HEADS * DA_V_DIM
ROPE_BASE = 10000.0
MOE_GROUPS, MOE_EPG, MOE_E, MOE_HIDDEN = 4, 8, 32, 512
ROUTER_LANE0 = MOE_GROUPS

LANES = 128
SUBLANES = 8
TM = 256
N_TILES = T // TM
N_PROMPT_TILES = T_P // TM
TILES_PER_LATENT_SEQ = LATENT_LEN // TM
MOE_BLOCK = 512
N_SLOT_BLOCKS = (2 * T) // MOE_BLOCK + MOE_E
N_SLOTS = N_SLOT_BLOCKS * MOE_BLOCK
VMEM_LIMIT = 56 * 1024 * 1024

F32 = jnp.float32
BF16 = jnp.bfloat16
HI = lax.Precision.HIGHEST
NEG = -1e30


def _cparams(sem, vmem=VMEM_LIMIT):
    return pltpu.CompilerParams(dimension_semantics=sem, vmem_limit_bytes=vmem)


def _mod_row(i):
    return jnp.where(i < N_PROMPT_TILES, N_LATENT_SEQ, (i - N_PROMPT_TILES) // TILES_PER_LATENT_SEQ)


def _pair_specs(n, tile_arg=0):
    def prompt(*a):
        return (jnp.minimum(a[tile_arg], N_PROMPT_TILES - 1), 0)

    def latent(*a):
        return (jnp.maximum(a[tile_arg] - N_PROMPT_TILES, 0), 0)

    return [pl.BlockSpec((TM, n), prompt), pl.BlockSpec((TM, n), latent)]


def _pair_read(i, p_ref, l_ref):
    return jnp.where(i < N_PROMPT_TILES, p_ref[...], l_ref[...])


def _silu(x):
    return x * (1.0 / (1.0 + jnp.exp(-x)))


def _sigmoid(x):
    return 1.0 / (1.0 + jnp.exp(-x))


def _softplus(x):
    return jnp.maximum(x, 0.0) + jnp.log(1.0 + jnp.exp(-jnp.abs(x)))


def _dot(a, b, **kw):
    return jnp.dot(a, b, preferred_element_type=F32, **kw)


def _split3(x):
    hi = x.astype(BF16)
    r1 = x - hi.astype(F32)
    mid = r1.astype(BF16)
    lo = (r1 - mid.astype(F32)).astype(BF16)
    return jnp.concatenate([hi, mid, lo], axis=1)


def _dot_nt(a, b):
    return lax.dot_general(a, b, (((1,), (1,)), ((), ())), preferred_element_type=F32)


def _ada_kernel(c_ref, w_ref, b_ref, o_ref):
    cs = _silu(c_ref[...])
    o_ref[...] = _dot(cs, w_ref[...], precision=HI) + b_ref[...]


def _ada_call(cvec, w_ada, b_ada):
    nj = 6
    return pl.pallas_call(
        _ada_kernel,
        out_shape=jax.ShapeDtypeStruct((DEPTH, 16, 6 * D), F32),
        grid=(DEPTH, nj),
        in_specs=[
            pl.BlockSpec((16, D), lambda l, j: (0, 0)),
            pl.BlockSpec((None, D, D), lambda l, j: (l, 0, j)),
            pl.BlockSpec((None, 1, D), lambda l, j: (l, 0, j)),
        ],
        out_specs=pl.BlockSpec((None, 16, D), lambda l, j: (l, 0, j)),
        compiler_params=_cparams(("arbitrary", "arbitrary")),
        name="ada_mod",
    )(cvec, w_ada, b_ada.reshape(DEPTH, 1, 6 * D))


_C_Z = (0, D)
_C_XBC = (_C_Z[1], _C_Z[1] + SSD_XBC)
_C_DT = (_C_XBC[1], _C_XBC[1] + LANES)
_C_GLU = (_C_DT[1], _C_DT[1] + 2 * CONV_CH)
_C_Q = (_C_GLU[1], _C_GLU[1] + DA_QK)
_C_K = (_C_Q[1], _C_Q[1] + DA_QK)
_C_V = (_C_K[1], _C_K[1] + DA_V)
N_PROJ = _C_V[1]


def _swap16(x):
    cols = []
    for c in range(x.shape[1] // LANES):
        xc = x[:, c * LANES:(c + 1) * LANES]
        lane = lax.broadcasted_iota(jnp.int32, xc.shape, 1)
        from_right = pltpu.roll(xc, LANES - 16, axis=1)
        from_left = pltpu.roll(xc, 16, axis=1)
        cols.append(jnp.where((lane >> 4) % 2 == 0, from_right, from_left))
    return jnp.concatenate(cols, axis=1)


def _inproj_kernel(l_ref, xp_ref, xl_ref, mod_ref, g_ref, w_ref, b_ref, cos_ref, sin_ref,
                   h_ref, z_ref, xbc_ref, dt_ref, u_ref, q_ref, k_ref, v_ref, kc_ref, vc_ref):
    i = pl.program_id(0)
    x = _pair_read(i, xp_ref, xl_ref)
    xn = x * lax.rsqrt(jnp.mean(x * x, axis=-1, keepdims=True) + EPS)
    h = xn * g_ref[...] * (1.0 + mod_ref[1:2, :]) + mod_ref[0:1, :]
    hb = h.astype(BF16)
    h_ref[...] = hb

    def proj(c):
        return _dot(hb, w_ref[:, c[0]:c[1]]) + b_ref[:, c[0]:c[1]]

    z_ref[...] = proj(_C_Z).astype(BF16)
    xbc_ref[...] = proj(_C_XBC)
    dt_ref[...] = proj(_C_DT)
    glu = proj(_C_GLU)
    u_ref[...] = glu[:, :CONV_CH] * _sigmoid(glu[:, CONV_CH:])
    cos = cos_ref[...]
    sin = sin_ref[...]
    q = proj(_C_Q)
    q = q * cos + _swap16(q) * sin
    q_ref[...] = (q * (DA_HEAD_DIM ** -0.5 * math.log2(math.e))).astype(BF16)
    k = proj(_C_K)
    k = k * cos + _swap16(k) * sin
    k_ref[...] = k.astype(BF16)
    v = proj(_C_V)
    v_ref[...] = v.astype(BF16)

    @pl.when(i < N_PROMPT_TILES)
    def _():
        kc_ref[...] = k.reshape(TM, DA_HEADS, LANES)
        vc_ref[...] = v.reshape(TM, DA_HEADS, LANES)


def _inproj_call(l_arr, x_p, x_l, mod6, norm1_g, w_proj, b_proj, cos_tab, sin_tab):
    tok = lambda n: pl.BlockSpec((TM, n), lambda i, l: (i, 0))
    rope_blk = lambda i, l: (jnp.where(i < N_PROMPT_TILES, TILES_PER_LATENT_SEQ,
                                       (i - N_PROMPT_TILES) % TILES_PER_LATENT_SEQ), 0)
    lay = lambda *shape: pl.BlockSpec((None,) + shape, lambda i, l: (l[0],) + (0,) * len(shape))
    grid_spec = pltpu.PrefetchScalarGridSpec(
        num_scalar_prefetch=1,
        grid=(N_TILES,),
        in_specs=_pair_specs(D) + [
            pl.BlockSpec((None, None, 6, D), lambda i, l: (l[0], _mod_row(i), 0, 0)),
            lay(1, D), lay(D, N_PROJ), lay(1, N_PROJ),
            pl.BlockSpec((TM, DA_QK), rope_blk), pl.BlockSpec((TM, DA_QK), rope_blk),
        ],
        out_specs=[tok(D), tok(D), tok(SSD_XBC), tok(LANES), tok(CONV_CH), tok(DA_QK), tok(DA_QK), tok(DA_V)]
        + [pl.BlockSpec((None, PROMPT_LEN, DA_HEADS, LANES),
                        lambda i, l: (jnp.minimum(i, N_PROMPT_SEQ - 1), 0, 0, 0))] * 2,
    )
    sds = lambda n, dt: jax.ShapeDtypeStruct((T, n), dt)
    cache = jax.ShapeDtypeStruct((N_PROMPT_SEQ, PROMPT_LEN, DA_HEADS, LANES), F32)
    return pl.pallas_call(
        _inproj_kernel,
        grid_spec=grid_spec,
        out_shape=[sds(D, BF16), sds(D, BF16), sds(SSD_XBC, F32), sds(LANES, F32), sds(CONV_CH, F32),
                   sds(DA_QK, BF16), sds(DA_QK, BF16), sds(DA_V, BF16), cache, cache],
        compiler_params=_cparams(("arbitrary",)),
        name="inproj",
    )(l_arr, x_p, x_l, mod6, norm1_g, w_proj, b_proj, cos_tab, sin_tab)


SSD_STEP = TM
CHUNKS_PER_STEP = SSD_STEP // SSD_CHUNK
N_SSD_STEPS = T // SSD_STEP
N_PROMPT_STEPS = T_P // SSD_STEP
STEPS_PER_PROMPT = PROMPT_LEN // SSD_STEP
STEPS_PER_LATENT = LATENT_LEN // SSD_STEP
HALO = SUBLANES
STATE_DUMP = N_PROMPT_SEQ


def _ssd_tables(fwd):
    cidx, flags, h0, so = (np.zeros((N_SSD_STEPS,), np.int32) for _ in range(4))
    for j in range(N_SSD_STEPS):
        c = j if fwd else N_SSD_STEPS - 1 - j
        if c < N_PROMPT_STEPS:
            seq, pos, n = c // STEPS_PER_PROMPT, c % STEPS_PER_PROMPT, STEPS_PER_PROMPT
            h0i, soi = N_LATENT_SEQ, seq
        else:
            cc = c - N_PROMPT_STEPS
            seq, pos, n = cc // STEPS_PER_LATENT, cc % STEPS_PER_LATENT, STEPS_PER_LATENT
            h0i, soi = seq, STATE_DUMP
        first = pos == 0 if fwd else pos == n - 1
        cidx[j] = c
        flags[j] = int(first) | (int(pos > 0) << 1) | (int(pos < n - 1) << 2)
        h0[j] = h0i
        so[j] = soi
    return [jnp.asarray(a) for a in (cidx, flags, h0, so)]


def _ssd_conv(flags, xc_ref, xp_ref, xn_ref, cw_ref, cb_ref, xpad):
    q = SSD_STEP
    xpad[0:HALO, :] = jnp.where(((flags >> 1) & 1) == 1, xp_ref[...], 0.0)
    xpad[HALO:HALO + q, :] = xc_ref[...]
    xpad[HALO + q:HALO + q + HALO, :] = jnp.where(((flags >> 2) & 1) == 1, xn_ref[...], 0.0)
    acc = jnp.zeros((q, SSD_XBC), F32) + cb_ref[...]
    pad = SSD_CONV_W // 2
    xp = xpad[...]
    rows = q + 2 * HALO
    for k in range(SSD_CONV_W):
        shifted = xp if k == pad else pltpu.roll(xp, (pad - k) % rows, axis=0)
        acc = acc + cw_ref[k:k + 1, :] * shifted[HALO:HALO + q, :]
    return _silu(acc)


def _ssd_scan_step(fwd, flags, xc, dt_ref, dtb_ref, alog_ref, dskip_ref, e_ref, h0_ref, y_ref, so_ref, state, lat_s):
    @pl.when((flags & 1) == 1)
    def _():
        state[...] = h0_ref[...]

    dt = _softplus(dt_ref[...] + dtb_ref[...])
    da = dt * (-jnp.exp(alog_ref[...]))
    st = state[...]
    for k in (range(CHUNKS_PER_STEP) if fwd else reversed(range(CHUNKS_PER_STEP))):
        rows = slice(k * SSD_CHUNK, (k + 1) * SSD_CHUNK)
        st = _ssd_scan_chunk(fwd, xc[rows], dt[rows], da[rows], st, dskip_ref, e_ref, y_ref, rows, lat_s.at[k])
    state[...] = st
    so_ref[...] = st


def _ssd_scan_chunk(fwd, xc, dt, da, st_all, dskip_ref, e_ref, y_ref, rows, lat_s):
    q = SSD_CHUNK
    ho = 0 if fwd else SSD_HEADS
    xs = xc[:, :D]
    bm = xc[:, D:D + LANES]
    cm = xc[:, D + LANES:D + 2 * LANES]
    row = lax.broadcasted_iota(jnp.int32, (q, q), 0)
    col = lax.broadcasted_iota(jnp.int32, (q, q), 1)
    tri = (col <= row) if fwd else (col >= row)
    p = _dot(jnp.where(tri, 1.0, 0.0).astype(BF16), _split3(da))
    la = ((p[:, :LANES] + p[:, LANES:2 * LANES]) + p[:, 2 * LANES:]) * math.log2(math.e)
    lat_s[...] = la.T
    e3 = e_ref[...]
    la_exp = _dot(_split3(la), e3)
    dt_exp = _dot(_split3(dt), e3)
    la_end = la_exp[q - 1:q, :] if fwd else la_exp[0:1, :]
    decay_end = jnp.exp2(la_end - la_exp)
    chunk_decay = jnp.exp2(la_end)
    decay_in = jnp.exp2(la_exp)
    xdt = xs * dt_exp
    xdt_b = xdt.astype(BF16)
    xdtw_b = (xdt * decay_end).astype(BF16)
    bmt = bm.T

    half = LANES // 2
    lane = lax.broadcasted_iota(jnp.int32, (q, LANES), 1)
    hpg = SSD_HEADS // SSD_GROUPS
    st_new = []
    for g in range(SSD_GROUPS):
        c_g = cm[:, g * half:(g + 1) * half].astype(BF16)
        b_g = bm[:, g * half:(g + 1) * half].astype(BF16)
        bt_g = bmt[g * half:(g + 1) * half, :].astype(BF16)
        cb = _dot_nt(c_g, b_g)
        for pp in range(hpg // 2):
            h_a = g * hpg + 2 * pp
            sl = slice(h_a * SSD_HEAD_DIM, (h_a + 2) * SSD_HEAD_DIM)
            xdt_p = xdt_b[:, sl]
            yd = []
            for h in (ho + h_a, ho + h_a + 1):
                seg = la[:, h:h + 1] - lat_s[h:h + 1, :]
                s_h = (cb * jnp.exp2(jnp.where(tri, seg, NEG))).astype(BF16)
                yd.append(_dot(s_h, xdt_p))
            y_diag = jnp.where(lane < half, yd[0], yd[1])
            st_in = st_all[:, sl]
            y = y_diag + _dot(c_g, st_in.astype(BF16)) * decay_in[:, sl]
            st_new.append(st_in * chunk_decay[:, sl] + _dot(bt_g, xdtw_b[:, sl]))
            if fwd:
                y = y + xs[:, sl] * dskip_ref[:, sl]
            y_ref[rows, sl] = y.astype(BF16)
    return jnp.concatenate(st_new, axis=1)


def _ssd_fwd_kernel(l_ref, cidx_ref, flags_ref, h0i_ref, soi_ref,
                    xc_ref, xp_ref, xn_ref, cw_ref, cb_ref, dt_ref, dtb_ref, alog_ref, dskip_ref, e_ref, h0_ref,
                    y_ref, so_ref, xco_ref, state, lat_s, xpad):
    flags = flags_ref[pl.program_id(0)]
    xc = _ssd_conv(flags, xc_ref, xp_ref, xn_ref, cw_ref, cb_ref, xpad)
    xco_ref[...] = xc
    _ssd_scan_step(True, flags, xc, dt_ref, dtb_ref, alog_ref, dskip_ref, e_ref, h0_ref, y_ref, so_ref, state, lat_s)


def _ssd_bwd_kernel(l_ref, cidx_ref, flags_ref, h0i_ref, soi_ref,
                    xc_ref, dt_ref, dtb_ref, alog_ref, e_ref, h0_ref, y_ref, so_ref, state, lat_s):
    flags = flags_ref[pl.program_id(0)]
    _ssd_scan_step(False, flags, xc_ref[...], dt_ref, dtb_ref, alog_ref, None, e_ref, h0_ref, y_ref, so_ref,
                   state, lat_s)


def _ssd_call(l_arr, tabs, xbc, dt, conv_w, conv_b, dt_bias, a_log, dskip, e_mat, h0t, xc_in, fwd):
    nb8 = T // HALO
    per = SSD_STEP // HALO
    cur = lambda j, l, ci, fl, h0, so: (ci[j], 0)
    prev = lambda j, l, ci, fl, h0, so: (jnp.maximum(ci[j] * per - 1, 0), 0)
    nxt = lambda j, l, ci, fl, h0, so: (jnp.minimum((ci[j] + 1) * per, nb8 - 1), 0)
    lay = lambda *shape: pl.BlockSpec((None,) + shape, lambda j, l, *_: (l[0],) + (0,) * len(shape))
    chunk = lambda n: pl.BlockSpec((SSD_STEP, n), cur)
    common_in = [chunk(LANES), lay(1, LANES), lay(1, LANES)]
    tail_in = [pl.BlockSpec((3 * LANES, D), lambda j, *_: (0, 0)),
               pl.BlockSpec((None, None, None, SSD_STATE, D),
                            lambda j, l, ci, fl, h0, so: (l[0], h0[j], 0 if fwd else 1, 0, 0))]
    out_specs = [chunk(D), pl.BlockSpec((None, SSD_STATE, D), lambda j, l, ci, fl, h0, so: (so[j], 0, 0))]
    out_shape = [jax.ShapeDtypeStruct((T, D), BF16), jax.ShapeDtypeStruct((N_PROMPT_SEQ + 1, SSD_STATE, D), F32)]
    scratch = [pltpu.VMEM((SSD_STATE, D), F32), pltpu.VMEM((CHUNKS_PER_STEP, LANES, SSD_CHUNK), F32)]
    if fwd:
        in_specs = ([chunk(SSD_XBC), pl.BlockSpec((HALO, SSD_XBC), prev), pl.BlockSpec((HALO, SSD_XBC), nxt),
                     lay(SSD_CONV_W, SSD_XBC), lay(1, SSD_XBC)] + common_in + [lay(1, D)] + tail_in)
        args = (xbc, xbc, xbc, conv_w, conv_b, dt, dt_bias, a_log, dskip, e_mat, h0t)
        out_specs.append(chunk(SSD_XBC))
        out_shape.append(jax.ShapeDtypeStruct((T, SSD_XBC), F32))
        scratch.append(pltpu.VMEM((SSD_STEP + 2 * HALO, SSD_XBC), F32))
    else:
        in_specs = [chunk(SSD_XBC)] + common_in + tail_in
        args = (xc_in, dt, dt_bias, a_log, e_mat, h0t)
    grid_spec = pltpu.PrefetchScalarGridSpec(num_scalar_prefetch=5, grid=(N_SSD_STEPS,), in_specs=in_specs,
                                             out_specs=out_specs, scratch_shapes=scratch)
    return pl.pallas_call(
        _ssd_fwd_kernel if fwd else _ssd_bwd_kernel,
        grid_spec=grid_spec,
        out_shape=out_shape,
        compiler_params=_cparams(("arbitrary",)),
        name="ssd_scan_fwd" if fwd else "ssd_scan_bwd",
    )(l_arr, *tabs, *args)


CV_HALO = 16


def _conformer_conv(i, uc_ref, up_ref, un_ref, w_ref, b_ref, g_ref, beta_ref, upad):
    pos = (i - N_PROMPT_TILES) % TILES_PER_LATENT_SEQ
    is_prompt = i < N_PROMPT_TILES
    no_l = jnp.logical_or(is_prompt, pos == 0)
    no_r = jnp.logical_or(is_prompt, pos == TILES_PER_LATENT_SEQ - 1)
    upad[0:CV_HALO, :] = jnp.where(no_l, 0.0, up_ref[...])
    upad[CV_HALO:CV_HALO + TM, :] = uc_ref[...]
    upad[CV_HALO + TM:CV_HALO + TM + CV_HALO, :] = jnp.where(no_r, 0.0, un_ref[...])
    acc = jnp.zeros((TM, CONV_CH), F32) + b_ref[...]
    pad = CONV_W // 2
    up = upad[...]
    rows = TM + 2 * CV_HALO
    for rot in range(SUBLANES):
        taps = [k for k in range(CONV_W) if (CV_HALO - pad + k) % SUBLANES == rot]
        rolled = up if rot == 0 else pltpu.roll(up, rows - rot, axis=0)
        for k in taps:
            base = CV_HALO - pad + k - rot
            acc = acc + w_ref[k:k + 1, :] * rolled[base:base + TM, :]
    mu = jnp.mean(acc, axis=-1, keepdims=True)
    xc = acc - mu
    var = jnp.mean(xc * xc, axis=-1, keepdims=True)
    y = xc * lax.rsqrt(var + EPS) * g_ref[...] + beta_ref[...]
    return _silu(y).astype(BF16)


def _conformer_conv_specs():
    per = TM // CV_HALO
    nb = T // CV_HALO
    return [pl.BlockSpec((TM, CONV_CH), lambda i, l: (i, 0)),
            pl.BlockSpec((CV_HALO, CONV_CH), lambda i, l: (jnp.maximum(i * per - 1, 0), 0)),
            pl.BlockSpec((CV_HALO, CONV_CH), lambda i, l: (jnp.minimum((i + 1) * per, nb - 1), 0))]


def _lambda_terms(l_ref, lam_ref):
    lf = jnp.full((1, 1), l_ref[0], jnp.int32).astype(F32)
    lam_init = 0.8 - 0.6 * jnp.exp(-0.3 * lf)
    p = lam_ref[...]
    s1 = jnp.sum(p[0:1, :] * p[1:2, :], axis=-1, keepdims=True)
    s2 = jnp.sum(p[2:3, :] * p[3:4, :], axis=-1, keepdims=True)
    lam = jnp.exp(s1) - jnp.exp(s2) + lam_init
    return lam, 1.0 - lam_init


def _attn_body(l_ref, q_ref, k_ref, v_ref, ck_ref, cv_ref, lam_ref, g_ref, o_ref, k_s, v_s, n_ctx):
    @pl.when(pl.program_id(1) == 0)
    def _():
        k_s[n_ctx:, :] = k_ref[...]
        for h in range(DA_HEADS):
            sl = slice(h * LANES, (h + 1) * LANES)
            if n_ctx:
                k_s[0:n_ctx, sl] = ck_ref[:, h, :].astype(BF16)
                v_s[0:n_ctx, 2 * h * LANES:(2 * h + 1) * LANES] = cv_ref[:, h, :].astype(BF16)
            v_s[n_ctx:, 2 * h * LANES:(2 * h + 1) * LANES] = v_ref[:, sl]
            v_s[:, (2 * h + 1) * LANES:(2 * h + 2) * LANES] = jnp.ones((v_s.shape[0], LANES), BF16)

    lam, out_scale = _lambda_terms(l_ref, lam_ref)
    tq = q_ref.shape[0]
    lane = lax.broadcasted_iota(jnp.int32, (tq, LANES), 1)
    zero = jnp.zeros((tq, LANES), BF16)
    for h in range(DA_HEADS):
        sl = slice(h * LANES, (h + 1) * LANES)
        qh = q_ref[:, sl]
        kh = k_s[:, sl]
        vh = v_s[:, 2 * h * LANES:(2 * h + 2) * LANES]
        outs = []
        for c in range(2):
            in_c = (lane < DA_HEAD_DIM) if c == 0 else (lane >= DA_HEAD_DIM)
            s = _dot_nt(jnp.where(in_c, qh, zero), kh)
            m = jnp.max(s, axis=-1, keepdims=True)
            pv = _dot(jnp.exp2(s - m).astype(BF16), vh)
            outs.append(pv[:, :LANES] / pv[:, LANES:])
        o = outs[0] - lam * outs[1]
        o = o * lax.rsqrt(jnp.mean(o * o, axis=-1, keepdims=True) + EPS)
        o_ref[:, sl] = (o * g_ref[...] * out_scale).astype(BF16)


def _attn_prompt_kernel(l_ref, q_ref, k_ref, v_ref, lam_ref, g_ref, o_ref, k_s, v_s):
    _attn_body(l_ref, q_ref, k_ref, v_ref, None, None, lam_ref, g_ref, o_ref, k_s, v_s, 0)


def _attn_latent_kernel(l_ref, q_ref, k_ref, v_ref, ck_ref, cv_ref, lam_ref, g_ref, o_ref, k_s, v_s):
    _attn_body(l_ref, q_ref, k_ref, v_ref, ck_ref, cv_ref, lam_ref, g_ref, o_ref, k_s, v_s, PAST_LEN)


def _attn_call(l_arr, q, k, v, cache_k, cache_v, da_lambda, subln_g, latent):
    if latent:
        nseq, seqlen, n_ctx = N_LATENT_SEQ, LATENT_LEN, PAST_LEN
        tile0, seq0 = N_PROMPT_TILES, T_P // LATENT_LEN
    else:
        nseq, seqlen, n_ctx = N_PROMPT_SEQ, PROMPT_LEN, 0
        tile0, seq0 = 0, 0
    nq = seqlen // TM
    lay = lambda *shape: pl.BlockSpec((None,) + shape, lambda b, i, l: (l[0],) + (0,) * len(shape))
    in_specs = [
        pl.BlockSpec((TM, DA_QK), lambda b, i, l: (tile0 + b * nq + i, 0)),
        pl.BlockSpec((seqlen, DA_QK), lambda b, i, l: (seq0 + b, 0)),
        pl.BlockSpec((seqlen, DA_V), lambda b, i, l: (seq0 + b, 0)),
    ]
    args = [q, k, v]
    if latent:
        ctx = pl.BlockSpec((None, None, PAST_LEN, DA_HEADS, LANES), lambda b, i, l: (b, l[0], 0, 0, 0))
        in_specs += [ctx, ctx]
        args += [cache_k, cache_v]
    in_specs += [lay(4, DA_HEAD_DIM), lay(1, DA_V_DIM)]
    args += [da_lambda, subln_g]
    grid_spec = pltpu.PrefetchScalarGridSpec(
        num_scalar_prefetch=1,
        grid=(nseq, nq),
        in_specs=in_specs,
        out_specs=pl.BlockSpec((TM, DA_V), lambda b, i, l: (b * nq + i, 0)),
        scratch_shapes=[pltpu.VMEM((n_ctx + seqlen, DA_QK), BF16), pltpu.VMEM((n_ctx + seqlen, 2 * DA_V), BF16)],
    )
    return pl.pallas_call(
        _attn_latent_kernel if latent else _attn_prompt_kernel,
        grid_spec=grid_spec,
        out_shape=jax.ShapeDtypeStruct((nseq * seqlen, DA_V), BF16),
        compiler_params=_cparams(("arbitrary", "arbitrary")),
        name="diff_attn_latent" if latent else "diff_attn_prompt",
    )(l_arr, *args)


def _combine_kernel(l_ref, xp_ref, xl_ref, h_ref, yf_ref, yb_ref, z_ref, u_ref, up_ref, un_ref, oap_ref, oal_ref,
                    mod_ref, cvw_ref, cvb_ref, cvg_ref, cvbeta_ref,
                    wg_ref, bg_ref, sg_ref, wa_ref, wb_ref, wc_ref, wo_ref, g2_ref, wr_ref, br_ref,
                    x1_ref, h2_ref, route_ref, cnt_ref, e1_ref, e2_ref, r1_ref, r2_ref, carry, upad):
    i = pl.program_id(0)

    @pl.when(i == 0)
    def _():
        carry[...] = jnp.zeros_like(carry)

    uc = _conformer_conv(i, u_ref, up_ref, un_ref, cvw_ref, cvb_ref, cvg_ref, cvbeta_ref, upad)

    y = (yf_ref[...].astype(F32) + yb_ref[...].astype(F32)) * _silu(z_ref[...].astype(F32))
    y = y * lax.rsqrt(jnp.mean(y * y, axis=-1, keepdims=True) + EPS) * sg_ref[...]
    br_a = _dot(y.astype(BF16), wa_ref[...])
    br_b = _dot(uc, wb_ref[...])
    br_c = _dot(_pair_read(i, oap_ref, oal_ref), wc_ref[...])
    hb = h_ref[...]

    def gate(n):
        return _sigmoid(_dot(hb, wg_ref[:, n * D:(n + 1) * D]) + bg_ref[:, n * D:(n + 1) * D])

    mix = gate(0) * br_a + gate(1) * br_b + gate(2) * br_c
    mixed = _dot(mix.astype(BF16), wo_ref[...])
    x1 = _pair_read(i, xp_ref, xl_ref) + mod_ref[2:3, :] * mixed
    x1_ref[...] = x1
    xn = x1 * lax.rsqrt(jnp.mean(x1 * x1, axis=-1, keepdims=True) + EPS)
    h2 = xn * g2_ref[...] * (1.0 + mod_ref[4:5, :]) + mod_ref[3:4, :]
    h2_ref[...] = h2

    h2_hi = h2.astype(BF16)
    h2_mid = (h2 - h2_hi.astype(F32)).astype(BF16)
    logits = _dot(jnp.concatenate([h2_hi, h2_mid, h2_hi], axis=1), wr_ref[...]) + br_ref[...]
    lane = lax.broadcasted_iota(jnp.int32, (TM, LANES), 1).astype(F32)

    def first_argmax(vals, vmax):
        return jnp.min(jnp.where(vals == vmax, lane, float(LANES)), axis=-1, keepdims=True)

    glog = jnp.where(lane < MOE_GROUPS, logits, NEG)
    gmax = jnp.max(glog, axis=-1, keepdims=True)
    gsel = first_argmax(glog, gmax)
    p_g = 1.0 / jnp.sum(jnp.exp(glog - gmax), axis=-1, keepdims=True)
    lo = ROUTER_LANE0 + MOE_EPG * gsel
    elog = jnp.where(jnp.logical_and(lane >= lo, lane < lo + MOE_EPG), logits, NEG)
    v1 = jnp.max(elog, axis=-1, keepdims=True)
    i1 = first_argmax(elog, v1)
    elog2 = jnp.where(lane == i1, NEG, elog)
    v2 = jnp.max(elog2, axis=-1, keepdims=True)
    i2 = first_argmax(elog2, v2)
    t2 = jnp.exp(v2 - v1)
    w1 = p_g / (1.0 + t2)
    w2 = p_g * t2 / (1.0 + t2)

    oh1 = lane == i1
    oh2 = lane == i2
    both = jnp.where(jnp.logical_or(oh1, oh2), 1.0, 0.0)
    r = lax.broadcasted_iota(jnp.int32, (TM, TM), 0)
    c = lax.broadcasted_iota(jnp.int32, (TM, TM), 1)
    strict_lower = jnp.where(c < r, 1.0, 0.0).astype(BF16)
    before = _dot(strict_lower, both.astype(BF16)) + carry[0:1, :]
    r1 = jnp.sum(jnp.where(oh1, before, 0.0), axis=-1, keepdims=True)
    r2 = jnp.sum(jnp.where(oh2, before, 0.0), axis=-1, keepdims=True)
    carry[...] = carry[...] + jnp.sum(both, axis=0, keepdims=True)
    cnt_ref[...] = carry[...]

    e1 = i1 - ROUTER_LANE0
    e2 = i2 - ROUTER_LANE0
    route = jnp.zeros((TM, LANES), F32)
    for n, val in enumerate((e1, e2, w1, w2, r1, r2)):
        route = jnp.where(lane == float(n), val, route)
    route_ref[...] = route
    route_t = route.T
    for n, ref in ((0, e1_ref), (1, e2_ref), (4, r1_ref), (5, r2_ref)):
        ref[...] = route_t[n:n + 1, :].astype(jnp.int32)


def _combine_call(l_arr, x_p, x_l, h, y_f, y_b, z, u, oa_p, oa_l, mod6, cv_w, cv_b, cv_g, cv_beta, w_gates, b_gates,
                  ssd_norm_g, w_br_ssd, w_br_conv, w_br_attn, w_out, norm2_g, w_router, b_router):
    tok = lambda n: pl.BlockSpec((TM, n), lambda i, l: (i, 0))
    lay = lambda *shape: pl.BlockSpec((None,) + shape, lambda i, l: (l[0],) + (0,) * len(shape))
    grid_spec = pltpu.PrefetchScalarGridSpec(
        num_scalar_prefetch=1,
        grid=(N_TILES,),
        in_specs=_pair_specs(D) + [tok(D), tok(D), tok(D), tok(D)] + _conformer_conv_specs() + _pair_specs(DA_V) + [
            pl.BlockSpec((None, None, 6, D), lambda i, l: (l[0], _mod_row(i), 0, 0)),
            lay(CONV_W, CONV_CH), lay(1, CONV_CH), lay(1, CONV_CH), lay(1, CONV_CH),
            lay(D, 3 * D), lay(1, 3 * D), lay(1, D), lay(D, D), lay(CONV_CH, D), lay(DA_V, D), lay(D, D),
            lay(1, D), lay(3 * D, LANES), lay(1, LANES),
        ],
        out_specs=[tok(D), tok(D), tok(LANES), pl.BlockSpec((SUBLANES, LANES), lambda i, l: (0, 0))]
        + [pl.BlockSpec((None, 1, TM), lambda i, l: (i, 0, 0))] * 4,
        scratch_shapes=[pltpu.VMEM((SUBLANES, LANES), F32), pltpu.VMEM((TM + 2 * CV_HALO, CONV_CH), F32)],
    )
    return pl.pallas_call(
        _combine_kernel,
        grid_spec=grid_spec,
        out_shape=[jax.ShapeDtypeStruct((T, D), F32), jax.ShapeDtypeStruct((T, D), F32),
                   jax.ShapeDtypeStruct((T, LANES), F32), jax.ShapeDtypeStruct((SUBLANES, LANES), F32)]
        + [jax.ShapeDtypeStruct((N_TILES, 1, TM), jnp.int32)] * 4,
        compiler_params=_cparams(("arbitrary",)),
        name="branch_combine_router",
    )(l_arr, x_p, x_l, h, y_f, y_b, z, u, u, u, oa_p, oa_l, mod6, cv_w, cv_b, cv_g, cv_beta, w_gates, b_gates,
      ssd_norm_g, w_br_ssd, w_br_conv, w_br_attn, w_out, norm2_g, w_router, b_router)


ROW_DMA_GROUP = 64


def _for_each_row(fn):
    def body(t, carry):
        fn(t)
        return carry

    lax.fori_loop(0, TM, body, 0, unroll=ROW_DMA_GROUP)


DISPATCH_SLOTS = 3


def _dispatch_kernel(e1_ref, e2_ref, r1_ref, r2_ref, ps_ref, pe_ref, h2_ref, xs_ref, zbuf, hbuf, in_sem, sem, zsem):
    i = pl.program_id(0)
    base = i * TM
    slot = i % DISPATCH_SLOTS

    def fetch(tile, s):
        return pltpu.make_async_copy(h2_ref.at[pl.ds(pl.multiple_of(tile * TM, TM), TM), :], hbuf.at[s], in_sem.at[s])

    def wait_rows(s):
        for _ in range(2):
            pltpu.make_async_copy(hbuf.at[s], xs_ref.at[pl.ds(0, TM), :], sem.at[s]).wait()

    @pl.when(i == 0)
    def _():
        fetch(0, 0).start()

    @pl.when(i == 0)
    def _():
        zbuf[...] = jnp.zeros_like(zbuf)

        def zero_fill(e):
            off = pl.multiple_of(pe_ref[e] - MOE_BLOCK, MOE_BLOCK)
            return pltpu.make_async_copy(zbuf, xs_ref.at[pl.ds(off, MOE_BLOCK), :], zsem)

        def unused_fill(b):
            off = pl.multiple_of(b * MOE_BLOCK, MOE_BLOCK)
            return pltpu.make_async_copy(zbuf, xs_ref.at[pl.ds(off, MOE_BLOCK), :], zsem)

        n_used = pe_ref[MOE_E - 1] // MOE_BLOCK
        for e in range(MOE_E):
            @pl.when(pe_ref[e] > ps_ref[e])
            def _():
                zero_fill(e).start()
        lax.fori_loop(n_used, N_SLOT_BLOCKS, lambda b, c: (unused_fill(b).start(), c)[1], 0)
        for e in range(MOE_E):
            @pl.when(pe_ref[e] > ps_ref[e])
            def _():
                zero_fill(e).wait()
        lax.fori_loop(n_used, N_SLOT_BLOCKS, lambda b, c: (unused_fill(b).wait(), c)[1], 0)

    @pl.when(i >= DISPATCH_SLOTS - 1)
    def _():
        wait_rows((i + 1) % DISPATCH_SLOTS)

    @pl.when(i + 1 < N_TILES)
    def _():
        fetch(i + 1, (i + 1) % DISPATCH_SLOTS).start()

    fetch(i, slot).wait()

    def issue(t):
        for e_ref, r_ref in ((e1_ref, r1_ref), (e2_ref, r2_ref)):
            dest = ps_ref[e_ref[base + t]] + r_ref[base + t]
            pltpu.make_async_copy(hbuf.at[slot, pl.ds(t, 1), :], xs_ref.at[pl.ds(dest, 1), :], sem.at[slot]).start()

    _for_each_row(issue)

    @pl.when(i == N_TILES - 1)
    def _():
        for back in range(DISPATCH_SLOTS - 1):
            wait_rows((i - back) % DISPATCH_SLOTS)


def _dispatch_call(e1, e2, r1, r2, pstart, pend, h2):
    grid_spec = pltpu.PrefetchScalarGridSpec(
        num_scalar_prefetch=6,
        grid=(N_TILES,),
        in_specs=[pl.BlockSpec(memory_space=pl.ANY)],
        out_specs=pl.BlockSpec(memory_space=pl.ANY),
        scratch_shapes=[pltpu.VMEM((MOE_BLOCK, D), F32), pltpu.VMEM((DISPATCH_SLOTS, TM, D), F32),
                        pltpu.SemaphoreType.DMA((DISPATCH_SLOTS,)), pltpu.SemaphoreType.DMA((DISPATCH_SLOTS,)),
                        pltpu.SemaphoreType.DMA(())],
    )
    return pl.pallas_call(
        _dispatch_kernel,
        grid_spec=grid_spec,
        out_shape=jax.ShapeDtypeStruct((N_SLOTS, D), F32),
        compiler_params=_cparams(("arbitrary",)),
        name="moe_dispatch",
    )(e1, e2, r1, r2, pstart, pend, h2)


def _moe_kernel(l_ref, be_ref, nb_ref, xs_ref, wg_ref, wu_ref, wd_ref, ys_ref, wg_s, wu_s, wd_s):
    i = pl.program_id(0)
    prev = be_ref[jnp.maximum(i - 1, 0)]

    @pl.when(jnp.logical_or(i == 0, be_ref[i] != prev))
    def _():
        wg_s[...] = wg_ref[...].astype(BF16)
        wu_s[...] = wu_ref[...].astype(BF16)
        wd_s[...] = wd_ref[...].astype(BF16)

    @pl.when(i < nb_ref[0])
    def _():
        xb = xs_ref[...].astype(BF16)
        hid = _silu(_dot(xb, wg_s[...])) * _dot(xb, wu_s[...])
        ys_ref[...] = _dot(hid.astype(BF16), wd_s[...])

    @pl.when(i >= nb_ref[0])
    def _():
        ys_ref[...] = jnp.zeros_like(ys_ref)


def _moe_call(l_arr, block_expert, n_used, xs, w_gate, w_up, w_down):
    wspec = lambda a, b: pl.BlockSpec((None, None, a, b), lambda i, l, be, nb: (l[0], be[i], 0, 0))
    grid_spec = pltpu.PrefetchScalarGridSpec(
        num_scalar_prefetch=3,
        grid=(N_SLOT_BLOCKS,),
        in_specs=[pl.BlockSpec((MOE_BLOCK, D), lambda i, l, be, nb: (jnp.minimum(i, nb[0] - 1), 0)),
                  wspec(D, MOE_HIDDEN), wspec(D, MOE_HIDDEN), wspec(MOE_HIDDEN, D)],
        out_specs=pl.BlockSpec((MOE_BLOCK, D), lambda i, l, be, nb: (i, 0)),
        scratch_shapes=[pltpu.VMEM((D, MOE_HIDDEN), BF16), pltpu.VMEM((D, MOE_HIDDEN), BF16),
                        pltpu.VMEM((MOE_HIDDEN, D), BF16)],
    )
    return pl.pallas_call(
        _moe_kernel,
        grid_spec=grid_spec,
        out_shape=jax.ShapeDtypeStruct((N_SLOTS, D), F32),
        compiler_params=_cparams(("arbitrary",)),
        name="moe_experts",
    )(l_arr, block_expert, n_used, xs, w_gate, w_up, w_down)


def _moe_combine_kernel(l_ref, e1_ref, e2_ref, r1_ref, r2_ref, ps_ref, x1_ref, route_ref, mod_ref, fg_ref, ys_ref,
                        op_ref, ol_ref, buf_a, buf_b, obuf, sem, *, final):
    i = pl.program_id(0)

    def request_row(tile, dst, dst_sem, t):
        for which, (e_ref, r_ref) in enumerate(((e1_ref, r1_ref), (e2_ref, r2_ref))):
            src = ps_ref[e_ref[tile * TM + t]] + r_ref[tile * TM + t]
            pltpu.make_async_copy(ys_ref.at[pl.ds(src, 1), :], dst.at[which, pl.ds(t, 1), :], dst_sem).start()

    def wait_tile(b, b_sem):
        for which in range(2):
            pltpu.make_async_copy(ys_ref.at[pl.ds(0, TM), :], b.at[which], b_sem).wait()

    @pl.when(i == 0)
    def _():
        _for_each_row(lambda t: request_row(0, buf_a, sem.at[0], t))

    nxt = jnp.minimum(i + 1, N_TILES - 1)

    def step(cur, cur_sem, other, other_sem):
        wait_tile(cur, cur_sem)

        def group(g, carry):
            r0 = pl.multiple_of(g * ROW_DMA_GROUP, ROW_DMA_GROUP)
            for r in range(ROW_DMA_GROUP):
                request_row(nxt, other, other_sem, r0 + r)
            rows = pl.ds(r0, ROW_DMA_GROUP)
            y = cur[0, rows, :] * route_ref[rows, 2:3] + cur[1, rows, :] * route_ref[rows, 3:4]
            x2 = x1_ref[rows, :] + mod_ref[5:6, :] * y
            if final:
                x2 = x2 * lax.rsqrt(jnp.mean(x2 * x2, axis=-1, keepdims=True) + EPS) * fg_ref[...]
            obuf[rows, :] = x2
            return carry

        lax.fori_loop(0, TM // ROW_DMA_GROUP, group, 0)

        @pl.when(i == N_TILES - 1)
        def _():
            wait_tile(other, other_sem)

    @pl.when(i % 2 == 0)
    def _():
        step(buf_a, sem.at[0], buf_b, sem.at[1])

    @pl.when(i % 2 == 1)
    def _():
        step(buf_b, sem.at[1], buf_a, sem.at[0])

    @pl.when(i < N_PROMPT_TILES)
    def _():
        op_ref[...] = obuf[...]

    @pl.when(i >= N_PROMPT_TILES)
    def _():
        ol_ref[...] = obuf[...]


def _moe_combine_call(l_arr, e1, e2, r1, r2, pstart, x1, route, mod6, final_g, ys, final):
    grid_spec = pltpu.PrefetchScalarGridSpec(
        num_scalar_prefetch=6,
        grid=(N_TILES,),
        in_specs=[
            pl.BlockSpec((TM, D), lambda i, *_: (i, 0)),
            pl.BlockSpec((TM, LANES), lambda i, *_: (i, 0)),
            pl.BlockSpec((None, None, 6, D), lambda i, l, *_: (l[0], _mod_row(i), 0, 0)),
            pl.BlockSpec((1, D), lambda i, *_: (0, 0)),
            pl.BlockSpec(memory_space=pl.ANY),
        ],
        out_specs=_pair_specs(D),
        scratch_shapes=[pltpu.VMEM((2, TM, D), F32), pltpu.VMEM((2, TM, D), F32), pltpu.VMEM((TM, D), F32),
                        pltpu.SemaphoreType.DMA((2,))],
    )
    return pl.pallas_call(
        functools.partial(_moe_combine_kernel, final=final),
        grid_spec=grid_spec,
        out_shape=[jax.ShapeDtypeStruct((T_P, D), F32), jax.ShapeDtypeStruct((T_L, D), F32)],
        compiler_params=_cparams(("arbitrary",)),
        name="moe_combine_final" if final else "moe_combine",
    )(l_arr, e1, e2, r1, r2, pstart, x1, route, mod6, final_g, ys)


def _rope_tables():
    n = LATENT_LEN
    rows = n // GRID_W
    row = jnp.repeat(jnp.arange(rows), GRID_W).astype(F32)
    col = jnp.tile(jnp.arange(GRID_W), rows).astype(F32)
    axis_dim = DA_HEAD_DIM // 2
    inv_freq = 1.0 / (ROPE_BASE ** (jnp.arange(0, axis_dim, 2, dtype=F32) / axis_dim))
    ar, ac = row[:, None] * inv_freq, col[:, None] * inv_freq
    cos64 = jnp.concatenate([jnp.cos(ar), jnp.cos(ar), jnp.cos(ac), jnp.cos(ac)], axis=1)
    sin64 = jnp.concatenate([-jnp.sin(ar), jnp.sin(ar), -jnp.sin(ac), jnp.sin(ac)], axis=1)
    reps = DA_QK // DA_HEAD_DIM
    cos = jnp.concatenate([jnp.tile(cos64, (1, reps)), jnp.ones((TM, DA_QK), F32)], axis=0)
    sin = jnp.concatenate([jnp.tile(sin64, (1, reps)), jnp.zeros((TM, DA_QK), F32)], axis=0)
    return cos, sin


def _pad_lanes(a, n=LANES):
    return jnp.pad(a, [(0, 0)] * (a.ndim - 1) + [(0, n - a.shape[-1])])


def kernel(x_prompt, x_sample, cache_k, cache_v, state_ssd, c, c_ctx, w_ada, b_ada, norm1_g, norm2_g, w_in, b_in,
           ssd_conv_w, ssd_conv_b, ssd_dt_bias, ssd_a_log, ssd_d, ssd_norm_g, w_br_ssd, cv_dw_w, cv_dw_b, cv_ln_g,
           cv_ln_b, w_br_conv, da_lambda, da_subln_g, w_br_attn, w_out, moe_w_group, moe_b_group, moe_w_expert,
           moe_b_expert, moe_w_gate, moe_w_up, moe_w_down, final_g):
    L = DEPTH
    x_p, x_l = x_prompt.reshape(T_P, D), x_sample.reshape(T_L, D)

    cvec = jnp.concatenate([c, c_ctx[None, :], jnp.zeros((16 - N_LATENT_SEQ - 1, D), F32)], axis=0)
    mod6 = _ada_call(cvec, w_ada, b_ada).reshape(L, 16, 6, D)

    o_z, o_xbc, o_dt = 0, D, D + SSD_XBC
    o_glu = o_dt + 2 * SSD_HEADS
    o_q = o_glu + 2 * CONV_CH
    o_k, o_v, o_g = o_q + DA_QK, o_q + 2 * DA_QK, o_q + 2 * DA_QK + DA_V

    def regroup(w):
        return jnp.concatenate([w[..., o_z:o_dt], _pad_lanes(w[..., o_dt:o_glu]), w[..., o_glu:o_g]], axis=-1)

    w_proj = regroup(w_in).astype(BF16)
    b_proj = regroup(b_in).reshape(L, 1, N_PROJ)
    w_gates = w_in[..., o_g:].astype(BF16)
    b_gates = b_in[..., o_g:].reshape(L, 1, 3 * D)
    cos_tab, sin_tab = _rope_tables()

    tabs_f, tabs_b = _ssd_tables(True), _ssd_tables(False)
    dt_bias = _pad_lanes(ssd_dt_bias.reshape(L, 1, 2 * SSD_HEADS))
    a_log = _pad_lanes(ssd_a_log.reshape(L, 1, 2 * SSD_HEADS))
    dskip = jnp.repeat(ssd_d, SSD_HEAD_DIM, axis=-1).reshape(L, 1, D)
    hp = np.arange(D) // SSD_HEAD_DIM
    e_f, e_b = (jnp.asarray(np.tile(np.arange(LANES)[:, None] == ho + hp[None, :], (3, 1)).astype(np.float32))
                .astype(BF16) for ho in (0, SSD_HEADS))
    h0t = jnp.transpose(state_ssd, (1, 0, 2, 5, 3, 4)).reshape(L, N_LATENT_SEQ, 2, SSD_STATE, D)
    h0t = jnp.concatenate([h0t, jnp.zeros((L, 1, 2, SSD_STATE, D), F32)], axis=1)

    w_router = _pad_lanes(jnp.concatenate([moe_w_group, moe_w_expert], axis=-1))
    wr_hi = w_router.astype(BF16)
    wr_mid = (w_router - wr_hi.astype(F32)).astype(BF16)
    w_router = jnp.concatenate([wr_hi, wr_hi, wr_mid], axis=1)
    b_router = _pad_lanes(jnp.concatenate([moe_b_group, moe_b_expert], axis=-1)).reshape(L, 1, LANES)
    w_br_ssd_b, w_br_conv_b = w_br_ssd.astype(BF16), w_br_conv.astype(BF16)
    w_br_attn_b, w_out_b = w_br_attn.astype(BF16), w_out.astype(BF16)
    r3 = lambda a: a.reshape(L, 1, a.shape[-1])

    ks_new, vs_new, ss_new = [], [], []
    for layer in range(L):
        l_arr = jnp.full((1,), layer, jnp.int32)
        h, z, xbc, dt, u, q, k, v, k_cache, v_cache = _inproj_call(l_arr, x_p, x_l, mod6, r3(norm1_g), w_proj, b_proj,
                                                                   cos_tab, sin_tab)
        y_f, st_f, xc = _ssd_call(l_arr, tabs_f, xbc, dt, ssd_conv_w, r3(ssd_conv_b), dt_bias, a_log, dskip,
                                  e_f, h0t, None, True)
        y_b, st_b = _ssd_call(l_arr, tabs_b, None, dt, None, None, dt_bias, a_log, None, e_b, h0t, xc, False)
        oa_p = _attn_call(l_arr, q, k, v, None, None, da_lambda, r3(da_subln_g), latent=False)
        oa_l = _attn_call(l_arr, q, k, v, cache_k, cache_v, da_lambda, r3(da_subln_g), latent=True)
        x1, h2, route, counts, e1, e2, r1, r2 = _combine_call(
            l_arr, x_p, x_l, h, y_f, y_b, z, u, oa_p, oa_l, mod6, cv_dw_w, r3(cv_dw_b), r3(cv_ln_g), r3(cv_ln_b),
            w_gates, b_gates, r3(ssd_norm_g), w_br_ssd_b, w_br_conv_b, w_br_attn_b, w_out_b, r3(norm2_g),
            w_router, b_router)
        e1, e2, r1, r2 = (a.reshape(T) for a in (e1, e2, r1, r2))
        cnt = counts[0, ROUTER_LANE0:ROUTER_LANE0 + MOE_E].astype(jnp.int32)
        pcnt = (cnt + MOE_BLOCK - 1) // MOE_BLOCK * MOE_BLOCK
        pend = jnp.cumsum(pcnt)
        pstart = pend - pcnt
        blk0 = jnp.arange(N_SLOT_BLOCKS, dtype=jnp.int32) * MOE_BLOCK
        block_expert = jnp.minimum(jnp.sum((pend[None, :] <= blk0[:, None]).astype(jnp.int32), axis=1), MOE_E - 1)
        n_used = pend[-1:] // MOE_BLOCK

        xs = _dispatch_call(e1, e2, r1, r2, pstart, pend, h2)
        ys = _moe_call(l_arr, block_expert, n_used, xs, moe_w_gate, moe_w_up, moe_w_down)
        x_p, x_l = _moe_combine_call(l_arr, e1, e2, r1, r2, pstart, x1, route, mod6, final_g.reshape(1, D), ys,
                                     layer == L - 1)

        ks_new.append(k_cache)
        vs_new.append(v_cache)
        s = jnp.stack([st_f[:N_PROMPT_SEQ], st_b[:N_PROMPT_SEQ]], axis=1)
        s = s.reshape(N_PROMPT_SEQ, 2, SSD_STATE, SSD_HEADS, SSD_HEAD_DIM)
        ss_new.append(jnp.transpose(s, (0, 1, 3, 4, 2)))

    y_prompt = x_p.reshape(N_PROMPT_SEQ, PROMPT_LEN, D)
    y_sample = x_l.reshape(N_LATENT_SEQ, LATENT_LEN, D)
    return (y_prompt, y_sample, jnp.stack(ks_new, axis=1), jnp.stack(vs_new, axis=1), jnp.stack(ss_new, axis=1))
```

```python
import functools
import math

import jax
import jax.numpy as jnp
import numpy as np
from jax import lax
from jax.experimental import pallas as pl
from jax.experimental.pallas import tpu as pltpu

D = 1024
DEPTH = 4
N_PROMPT_SEQ, PROMPT_LEN = 16, 256
N_LATENT_SEQ, LATENT_LEN = 8, 2048
PAST_LEN = 512
T_P = N_PROMPT_SEQ * PROMPT_LEN
T_L = N_LATENT_SEQ * LATENT_LEN
T = T_P + T_L
GRID_W = 64
EPS = 1e-6
SSD_HEADS, SSD_HEAD_DIM, SSD_STATE, SSD_GROUPS = 16, 64, 64, 2
SSD_CONV_W = 5
SSD_CHUNK = 128
SSD_XBC = D + 2 * SSD_GROUPS * SSD_STATE
CONV_CH, CONV_W = 512, 31
DA_HEADS, DA_HEAD_DIM, DA_V_DIM = 4, 64, 128
DA_QK = DA_HEADS * 2 * DA_HEAD_DIM
DA_V = DA_HEADS * DA_V_DIM
ROPE_BASE = 10000.0
MOE_GROUPS, MOE_EPG, MOE_E, MOE_HIDDEN = 4, 8, 32, 512
ROUTER_LANE0 = MOE_GROUPS

LANES = 128
SUBLANES = 8
TM = 256
N_TILES = T // TM
N_PROMPT_TILES = T_P // TM
TILES_PER_LATENT_SEQ = LATENT_LEN // TM
MOE_BLOCK = 512
N_SLOT_BLOCKS = (2 * T) // MOE_BLOCK + MOE_E
N_SLOTS = N_SLOT_BLOCKS * MOE_BLOCK
VMEM_LIMIT = 56 * 1024 * 1024

F32 = jnp.float32
BF16 = jnp.bfloat16
HI = lax.Precision.HIGHEST
NEG = -1e30


def _cparams(sem, vmem=VMEM_LIMIT):
    return pltpu.CompilerParams(dimension_semantics=sem, vmem_limit_bytes=vmem)


def _mod_row(i):
    return jnp.where(i < N_PROMPT_TILES, N_LATENT_SEQ, (i - N_PROMPT_TILES) // TILES_PER_LATENT_SEQ)


def _pair_specs(n, tile_arg=0):
    def prompt(*a):
        return (jnp.minimum(a[tile_arg], N_PROMPT_TILES - 1), 0)

    def latent(*a):
        return (jnp.maximum(a[tile_arg] - N_PROMPT_TILES, 0), 0)

    return [pl.BlockSpec((TM, n), prompt), pl.BlockSpec((TM, n), latent)]


def _pair_read(i, p_ref, l_ref):
    return jnp.where(i < N_PROMPT_TILES, p_ref[...], l_ref[...])


def _silu(x):
    return x * (1.0 / (1.0 + jnp.exp(-x)))


def _sigmoid(x):
    return 1.0 / (1.0 + jnp.exp(-x))


def _softplus(x):
    return jnp.maximum(x, 0.0) + jnp.log(1.0 + jnp.exp(-jnp.abs(x)))


def _dot(a, b, **kw):
    return jnp.dot(a, b, preferred_element_type=F32, **kw)


def _split3(x):
    hi = x.astype(BF16)
    r1 = x - hi.astype(F32)
    mid = r1.astype(BF16)
    lo = (r1 - mid.astype(F32)).astype(BF16)
    return jnp.concatenate([hi, mid, lo], axis=1)


def _dot_nt(a, b):
    return lax.dot_general(a, b, (((1,), (1,)), ((), ())), preferred_element_type=F32)


def _ada_kernel(c_ref, w_ref, b_ref, o_ref):
    cs = _silu(c_ref[...])
    o_ref[...] = _dot(cs, w_ref[...], precision=HI) + b_ref[...]


def _ada_call(cvec, w_ada, b_ada):
    nj = 6
    return pl.pallas_call(
        _ada_kernel,
        out_shape=jax.ShapeDtypeStruct((DEPTH, 16, 6 * D), F32),
        grid=(DEPTH, nj),
        in_specs=[
            pl.BlockSpec((16, D), lambda l, j: (0, 0)),
            pl.BlockSpec((None, D, D), lambda l, j: (l, 0, j)),
            pl.BlockSpec((None, 1, D), lambda l, j: (l, 0, j)),
        ],
        out_specs=pl.BlockSpec((None, 16, D), lambda l, j: (l, 0, j)),
        compiler_params=_cparams(("arbitrary", "arbitrary")),
        name="ada_mod",
    )(cvec, w_ada, b_ada.reshape(DEPTH, 1, 6 * D))


_C_Z = (0, D)
_C_XBC = (_C_Z[1], _C_Z[1] + SSD_XBC)
_C_DT = (_C_XBC[1], _C_XBC[1] + LANES)
_C_GLU = (_C_DT[1], _C_DT[1] + 2 * CONV_CH)
_C_Q = (_C_GLU[1], _C_GLU[1] + DA_QK)
_C_K = (_C_Q[1], _C_Q[1] + DA_QK)
_C_V = (_C_K[1], _C_K[1] + DA_V)
N_PROJ = _C_V[1]


def _swap16(x):
    cols = []
    for c in range(x.shape[1] // LANES):
        xc = x[:, c * LANES:(c + 1) * LANES]
        lane = lax.broadcasted_iota(jnp.int32, xc.shape, 1)
        from_right = pltpu.roll(xc, LANES - 16, axis=1)
        from_left = pltpu.roll(xc, 16, axis=1)
        cols.append(jnp.where((lane >> 4) % 2 == 0, from_right, from_left))
    return jnp.concatenate(cols, axis=1)


def _inproj_kernel(l_ref, xp_ref, xl_ref, mod_ref, g_ref, w_ref, b_ref, cos_ref, sin_ref,
                   h_ref, z_ref, xbc_ref, dt_ref, u_ref, q_ref, k_ref, v_ref, kc_ref, vc_ref):
    i = pl.program_id(0)
    x = _pair_read(i, xp_ref, xl_ref)
    xn = x * lax.rsqrt(jnp.mean(x * x, axis=-1, keepdims=True) + EPS)
    h = xn * g_ref[...] * (1.0 + mod_ref[1:2, :]) + mod_ref[0:1, :]
    hb = h.astype(BF16)
    h_ref[...] = hb

    def proj(c):
        return _dot(hb, w_ref[:, c[0]:c[1]]) + b_ref[:, c[0]:c[1]]

    z_ref[...] = proj(_C_Z).astype(BF16)
    xbc_ref[...] = proj(_C_XBC)
    dt_ref[...] = proj(_C_DT)
    glu = proj(_C_GLU)
    u_ref[...] = glu[:, :CONV_CH] * _sigmoid(glu[:, CONV_CH:])
    cos = cos_ref[...]
    sin = sin_ref[...]
    q = proj(_C_Q)
    q = q * cos + _swap16(q) * sin
    q_ref[...] = (q * (DA_HEAD_DIM ** -0.5 * math.log2(math.e))).astype(BF16)
    k = proj(_C_K)
    k = k * cos + _swap16(k) * sin
    k_ref[...] = k.astype(BF16)
    v = proj(_C_V)
    v_ref[...] = v.astype(BF16)

    @pl.when(i < N_PROMPT_TILES)
    def _():
        kc_ref[...] = k.reshape(TM, DA_HEADS, LANES)
        vc_ref[...] = v.reshape(TM, DA_HEADS, LANES)


def _inproj_call(l_arr, x_p, x_l, mod6, norm1_g, w_proj, b_proj, cos_tab, sin_tab):
    tok = lambda n: pl.BlockSpec((TM, n), lambda i, l: (i, 0))
    rope_blk = lambda i, l: (jnp.where(i < N_PROMPT_TILES, TILES_PER_LATENT_SEQ,
                                       (i - N_PROMPT_TILES) % TILES_PER_LATENT_SEQ), 0)
    lay = lambda *shape: pl.BlockSpec((None,) + shape, lambda i, l: (l[0],) + (0,) * len(shape))
    grid_spec = pltpu.PrefetchScalarGridSpec(
        num_scalar_prefetch=1,
        grid=(N_TILES,),
        in_specs=_pair_specs(D) + [
            pl.BlockSpec((None, None, 6, D), lambda i, l: (l[0], _mod_row(i), 0, 0)),
            lay(1, D), lay(D, N_PROJ), lay(1, N_PROJ),
            pl.BlockSpec((TM, DA_QK), rope_blk), pl.BlockSpec((TM, DA_QK), rope_blk),
        ],
        out_specs=[tok(D), tok(D), tok(SSD_XBC), tok(LANES), tok(CONV_CH), tok(DA_QK), tok(DA_QK), tok(DA_V)]
        + [pl.BlockSpec((None, PROMPT_LEN, DA_HEADS, LANES),
                        lambda i, l: (jnp.minimum(i, N_PROMPT_SEQ - 1), 0, 0, 0))] * 2,
    )
    sds = lambda n, dt: jax.ShapeDtypeStruct((T, n), dt)
    cache = jax.ShapeDtypeStruct((N_PROMPT_SEQ, PROMPT_LEN, DA_HEADS, LANES), F32)
    return pl.pallas_call(
        _inproj_kernel,
        grid_spec=grid_spec,
        out_shape=[sds(D, BF16), sds(D, BF16), sds(SSD_XBC, F32), sds(LANES, F32), sds(CONV_CH, F32),
                   sds(DA_QK, BF16), sds(DA_QK, BF16), sds(DA_V, BF16), cache, cache],
        compiler_params=_cparams(("arbitrary",)),
        name="inproj",
    )(l_arr, x_p, x_l, mod6, norm1_g, w_proj, b_proj, cos_tab, sin_tab)


SSD_STEP = TM
CHUNKS_PER_STEP = SSD_STEP // SSD_CHUNK
N_SSD_STEPS = T // SSD_STEP
N_PROMPT_STEPS = T_P // SSD_STEP
STEPS_PER_PROMPT = PROMPT_LEN // SSD_STEP
STEPS_PER_LATENT = LATENT_LEN // SSD_STEP
HALO = SUBLANES
STATE_DUMP = N_PROMPT_SEQ


def _ssd_tables(fwd):
    cidx, flags, h0, so = (np.zeros((N_SSD_STEPS,), np.int32) for _ in range(4))
    for j in range(N_SSD_STEPS):
        c = j if fwd else N_SSD_STEPS - 1 - j
        if c < N_PROMPT_STEPS:
            seq, pos, n = c // STEPS_PER_PROMPT, c % STEPS_PER_PROMPT, STEPS_PER_PROMPT
            h0i, soi = N_LATENT_SEQ, seq
        else:
            cc = c - N_PROMPT_STEPS
            seq, pos, n = cc // STEPS_PER_LATENT, cc % STEPS_PER_LATENT, STEPS_PER_LATENT
            h0i, soi = seq, STATE_DUMP
        first = pos == 0 if fwd else pos == n - 1
        cidx[j] = c
        flags[j] = int(first) | (int(pos > 0) << 1) | (int(pos < n - 1) << 2)
        h0[j] = h0i
        so[j] = soi
    return [jnp.asarray(a) for a in (cidx, flags, h0, so)]


def _ssd_conv(flags, xc_ref, xp_ref, xn_ref, cw_ref, cb_ref, xpad):
    q = SSD_STEP
    xpad[0:HALO, :] = jnp.where(((flags >> 1) & 1) == 1, xp_ref[...], 0.0)
    xpad[HALO:HALO + q, :] = xc_ref[...]
    xpad[HALO + q:HALO + q + HALO, :] = jnp.where(((flags >> 2) & 1) == 1, xn_ref[...], 0.0)
    acc = jnp.zeros((q, SSD_XBC), F32) + cb_ref[...]
    pad = SSD_CONV_W // 2
    xp = xpad[...]
    rows = q + 2 * HALO
    for k in range(SSD_CONV_W):
        shifted = xp if k == pad else pltpu.roll(xp, (pad - k) % rows, axis=0)
        acc = acc + cw_ref[k:k + 1, :] * shifted[HALO:HALO + q, :]
    return _silu(acc)


def _ssd_scan_step(fwd, flags, xc, dt_ref, dtb_ref, alog_ref, dskip_ref, e_ref, h0_ref, y_ref, so_ref, state, lat_s):
    @pl.when((flags & 1) == 1)
    def _():
        state[...] = h0_ref[...]

    dt = _softplus(dt_ref[...] + dtb_ref[...])
    da = dt * (-jnp.exp(alog_ref[...]))
    st = state[...]
    for k in (range(CHUNKS_PER_STEP) if fwd else reversed(range(CHUNKS_PER_STEP))):
        rows = slice(k * SSD_CHUNK, (k + 1) * SSD_CHUNK)
        st = _ssd_scan_chunk(fwd, xc[rows], dt[rows], da[rows], st, dskip_ref, e_ref, y_ref, rows, lat_s.at[k])
    state[...] = st
    so_ref[...] = st


def _ssd_scan_chunk(fwd, xc, dt, da, st_all, dskip_ref, e_ref, y_ref, rows, lat_s):
    q = SSD_CHUNK
    ho = 0 if fwd else SSD_HEADS
    xs = xc[:, :D]
    bm = xc[:, D:D + LANES]
    cm = xc[:, D + LANES:D + 2 * LANES]
    row = lax.broadcasted_iota(jnp.int32, (q, q), 0)
    col = lax.broadcasted_iota(jnp.int32, (q, q), 1)
    tri = (col <= row) if fwd else (col >= row)
    p = _dot(jnp.where(tri, 1.0, 0.0).astype(BF16), _split3(da))
    la = ((p[:, :LANES] + p[:, LANES:2 * LANES]) + p[:, 2 * LANES:]) * math.log2(math.e)
    lat_s[...] = la.T
    e3 = e_ref[...]
    la_exp = _dot(_split3(la), e3)
    dt_exp = _dot(_split3(dt), e3)
    la_end = la_exp[q - 1:q, :] if fwd else la_exp[0:1, :]
    decay_end = jnp.exp2(la_end - la_exp)
    chunk_decay = jnp.exp2(la_end)
    decay_in = jnp.exp2(la_exp)
    xdt = xs * dt_exp
    xdt_b = xdt.astype(BF16)
    xdtw_b = (xdt * decay_end).astype(BF16)
    bmt = bm.T

    half = LANES // 2
    lane = lax.broadcasted_iota(jnp.int32, (q, LANES), 1)
    hpg = SSD_HEADS // SSD_GROUPS
    st_new = []
    for g in range(SSD_GROUPS):
        c_g = cm[:, g * half:(g + 1) * half].astype(BF16)
        b_g = bm[:, g * half:(g + 1) * half].astype(BF16)
        bt_g = bmt[g * half:(g + 1) * half, :].astype(BF16)
        cb = _dot_nt(c_g, b_g)
        for pp in range(hpg // 2):
            h_a = g * hpg + 2 * pp
            sl = slice(h_a * SSD_HEAD_DIM, (h_a + 2) * SSD_HEAD_DIM)
            xdt_p = xdt_b[:, sl]
            yd = []
            for h in (ho + h_a, ho + h_a + 1):
                seg = la[:, h:h + 1] - lat_s[h:h + 1, :]
                s_h = (cb * jnp.exp2(jnp.where(tri, seg, NEG))).astype(BF16)
                yd.append(_dot(s_h, xdt_p))
            y_diag = jnp.where(lane < half, yd[0], yd[1])
            st_in = st_all[:, sl]
            y = y_diag + _dot(c_g, st_in.astype(BF16)) * decay_in[:, sl]
            st_new.append(st_in * chunk_decay[:, sl] + _dot(bt_g, xdtw_b[:, sl]))
            if fwd:
                y = y + xs[:, sl] * dskip_ref[:, sl]
            y_ref[rows, sl] = y.astype(BF16)
    return jnp.concatenate(st_new, axis=1)


def _ssd_fwd_kernel(l_ref, cidx_ref, flags_ref, h0i_ref, soi_ref,
                    xc_ref, xp_ref, xn_ref, cw_ref, cb_ref, dt_ref, dtb_ref, alog_ref, dskip_ref, e_ref, h0_ref,
                    y_ref, so_ref, xco_ref, state, lat_s, xpad):
    flags = flags_ref[pl.program_id(0)]
    xc = _ssd_conv(flags, xc_ref, xp_ref, xn_ref, cw_ref, cb_ref, xpad)
    xco_ref[...] = xc
    _ssd_scan_step(True, flags, xc, dt_ref, dtb_ref, alog_ref, dskip_ref, e_ref, h0_ref, y_ref, so_ref, state, lat_s)


def _ssd_bwd_kernel(l_ref, cidx_ref, flags_ref, h0i_ref, soi_ref,
                    xc_ref, dt_ref, dtb_ref, alog_ref, e_ref, h0_ref, y_ref, so_ref, state, lat_s):
    flags = flags_ref[pl.program_id(0)]
    _ssd_scan_step(False, flags, xc_ref[...], dt_ref, dtb_ref, alog_ref, None, e_ref, h0_ref, y_ref, so_ref,
                   state, lat_s)


def _ssd_call(l_arr, tabs, xbc, dt, conv_w, conv_b, dt_bias, a_log, dskip, e_mat, h0t, xc_in, fwd):
    nb8 = T // HALO
    per = SSD_STEP // HALO
    cur = lambda j, l, ci, fl, h0, so: (ci[j], 0)
    prev = lambda j, l, ci, fl, h0, so: (jnp.maximum(ci[j] * per - 1, 0), 0)
    nxt = lambda j, l, ci, fl, h0, so: (jnp.minimum((ci[j] + 1) * per, nb8 - 1), 0)
    lay = lambda *shape: pl.BlockSpec((None,) + shape, lambda j, l, *_: (l[0],) + (0,) * len(shape))
    chunk = lambda n: pl.BlockSpec((SSD_STEP, n), cur)
    common_in = [chunk(LANES), lay(1, LANES), lay(1, LANES)]
    tail_in = [pl.BlockSpec((3 * LANES, D), lambda j, *_: (0, 0)),
               pl.BlockSpec((None, None, None, SSD_STATE, D),
                            lambda j, l, ci, fl, h0, so: (l[0], h0[j], 0 if fwd else 1, 0, 0))]
    out_specs = [chunk(D), pl.BlockSpec((None, SSD_STATE, D), lambda j, l, ci, fl, h0, so: (so[j], 0, 0))]
    out_shape = [jax.ShapeDtypeStruct((T, D), BF16), jax.ShapeDtypeStruct((N_PROMPT_SEQ + 1, SSD_STATE, D), F32)]
    scratch = [pltpu.VMEM((SSD_STATE, D), F32), pltpu.VMEM((CHUNKS_PER_STEP, LANES, SSD_CHUNK), F32)]
    if fwd:
        in_specs = ([chunk(SSD_XBC), pl.BlockSpec((HALO, SSD_XBC), prev), pl.BlockSpec((HALO, SSD_XBC), nxt),
                     lay(SSD_CONV_W, SSD_XBC), lay(1, SSD_XBC)] + common_in + [lay(1, D)] + tail_in)
        args = (xbc, xbc, xbc, conv_w, conv_b, dt, dt_bias, a_log, dskip, e_mat, h0t)
        out_specs.append(chunk(SSD_XBC))
        out_shape.append(jax.ShapeDtypeStruct((T, SSD_XBC), F32))
        scratch.append(pltpu.VMEM((SSD_STEP + 2 * HALO, SSD_XBC), F32))
    else:
        in_specs = [chunk(SSD_XBC)] + common_in + tail_in
        args = (xc_in, dt, dt_bias, a_log, e_mat, h0t)
    grid_spec = pltpu.PrefetchScalarGridSpec(num_scalar_prefetch=5, grid=(N_SSD_STEPS,), in_specs=in_specs,
                                             out_specs=out_specs, scratch_shapes=scratch)
    return pl.pallas_call(
        _ssd_fwd_kernel if fwd else _ssd_bwd_kernel,
        grid_spec=grid_spec,
        out_shape=out_shape,
        compiler_params=_cparams(("arbitrary",)),
        name="ssd_scan_fwd" if fwd else "ssd_scan_bwd",
    )(l_arr, *tabs, *args)


CV_HALO = 16


def _conformer_conv(i, uc_ref, up_ref, un_ref, w_ref, b_ref, g_ref, beta_ref, upad):
    pos = (i - N_PROMPT_TILES) % TILES_PER_LATENT_SEQ
    is_prompt = i < N_PROMPT_TILES
    no_l = jnp.logical_or(is_prompt, pos == 0)
    no_r = jnp.logical_or(is_prompt, pos == TILES_PER_LATENT_SEQ - 1)
    upad[0:CV_HALO, :] = jnp.where(no_l, 0.0, up_ref[...])
    upad[CV_HALO:CV_HALO + TM, :] = uc_ref[...]
    upad[CV_HALO + TM:CV_HALO + TM + CV_HALO, :] = jnp.where(no_r, 0.0, un_ref[...])
    acc = jnp.zeros((TM, CONV_CH), F32) + b_ref[...]
    pad = CONV_W // 2
    up = upad[...]
    rows = TM + 2 * CV_HALO
    for rot in range(SUBLANES):
        taps = [k for k in range(CONV_W) if (CV_HALO - pad + k) % SUBLANES == rot]
        rolled = up if rot == 0 else pltpu.roll(up, rows - rot, axis=0)
        for k in taps:
            base = CV_HALO - pad + k - rot
            acc = acc + w_ref[k:k + 1, :] * rolled[base:base + TM, :]
    mu = jnp.mean(acc, axis=-1, keepdims=True)
    xc = acc - mu
    var = jnp.mean(xc * xc, axis=-1, keepdims=True)
    y = xc * lax.rsqrt(var + EPS) * g_ref[...] + beta_ref[...]
    return _silu(y).astype(BF16)


def _conformer_conv_specs():
    per = TM // CV_HALO
    nb = T // CV_HALO
    return [pl.BlockSpec((TM, CONV_CH), lambda i, l: (i, 0)),
            pl.BlockSpec((CV_HALO, CONV_CH), lambda i, l: (jnp.maximum(i * per - 1, 0), 0)),
            pl.BlockSpec((CV_HALO, CONV_CH), lambda i, l: (jnp.minimum((i + 1) * per, nb - 1), 0))]


def _lambda_terms(l_ref, lam_ref):
    lf = jnp.full((1, 1), l_ref[0], jnp.int32).astype(F32)
    lam_init = 0.8 - 0.6 * jnp.exp(-0.3 * lf)
    p = lam_ref[...]
    s1 = jnp.sum(p[0:1, :] * p[1:2, :], axis=-1, keepdims=True)
    s2 = jnp.sum(p[2:3, :] * p[3:4, :], axis=-1, keepdims=True)
    lam = jnp.exp(s1) - jnp.exp(s2) + lam_init
    return lam, 1.0 - lam_init


def _attn_body(l_ref, q_ref, k_ref, v_ref, ck_ref, cv_ref, lam_ref, g_ref, o_ref, k_s, v_s, n_ctx):
    @pl.when(pl.program_id(1) == 0)
    def _():
        k_s[n_ctx:, :] = k_ref[...]
        for h in range(DA_HEADS):
            sl = slice(h * LANES, (h + 1) * LANES)
            if n_ctx:
                k_s[0:n_ctx, sl] = ck_ref[:, h, :].astype(BF16)
                v_s[0:n_ctx, 2 * h * LANES:(2 * h + 1) * LANES] = cv_ref[:, h, :].astype(BF16)
            v_s[n_ctx:, 2 * h * LANES:(2 * h + 1) * LANES] = v_ref[:, sl]
            v_s[:, (2 * h + 1) * LANES:(2 * h + 2) * LANES] = jnp.ones((v_s.shape[0], LANES), BF16)

    lam, out_scale = _lambda_terms(l_ref, lam_ref)
    tq = q_ref.shape[0]
    lane = lax.broadcasted_iota(jnp.int32, (tq, LANES), 1)
    zero = jnp.zeros((tq, LANES), BF16)
    for h in range(DA_HEADS):
        sl = slice(h * LANES, (h + 1) * LANES)
        qh = q_ref[:, sl]
        kh = k_s[:, sl]
        vh = v_s[:, 2 * h * LANES:(2 * h + 2) * LANES]
        outs = []
        for c in range(2):
            in_c = (lane < DA_HEAD_DIM) if c == 0 else (lane >= DA_HEAD_DIM)
            s = _dot_nt(jnp.where(in_c, qh, zero), kh)
            m = jnp.max(s, axis=-1, keepdims=True)
            pv = _dot(jnp.exp2(s - m).astype(BF16), vh)
            outs.append(pv[:, :LANES] / pv[:, LANES:])
        o = outs[0] - lam * outs[1]
        o = o * lax.rsqrt(jnp.mean(o * o, axis=-1, keepdims=True) + EPS)
        o_ref[:, sl] = (o * g_ref[...] * out_scale).astype(BF16)


def _attn_prompt_kernel(l_ref, q_ref, k_ref, v_ref, lam_ref, g_ref, o_ref, k_s, v_s):
    _attn_body(l_ref, q_ref, k_ref, v_ref, None, None, lam_ref, g_ref, o_ref, k_s, v_s, 0)


def _attn_latent_kernel(l_ref, q_ref, k_ref, v_ref, ck_ref, cv_ref, lam_ref, g_ref, o_ref, k_s, v_s):
    _attn_body(l_ref, q_ref, k_ref, v_ref, ck_ref, cv_ref, lam_ref, g_ref, o_ref, k_s, v_s, PAST_LEN)


def _attn_call(l_arr, q, k, v, cache_k, cache_v, da_lambda, subln_g, latent):
    if latent:
        nseq, seqlen, n_ctx = N_LATENT_SEQ, LATENT_LEN, PAST_LEN
        tile0, seq0 = N_PROMPT_TILES, T_P // LATENT_LEN
    else:
        nseq, seqlen, n_ctx = N_PROMPT_SEQ, PROMPT_LEN, 0
        tile0, seq0 = 0, 0
    nq = seqlen // TM
    lay = lambda *shape: pl.BlockSpec((None,) + shape, lambda b, i, l: (l[0],) + (0,) * len(shape))
    in_specs = [
        pl.BlockSpec((TM, DA_QK), lambda b, i, l: (tile0 + b * nq + i, 0)),
        pl.BlockSpec((seqlen, DA_QK), lambda b, i, l: (seq0 + b, 0)),
        pl.BlockSpec((seqlen, DA_V), lambda b, i, l: (seq0 + b, 0)),
    ]
    args = [q, k, v]
    if latent:
        ctx = pl.BlockSpec((None, None, PAST_LEN, DA_HEADS, LANES), lambda b, i, l: (b, l[0], 0, 0, 0))
        in_specs += [ctx, ctx]
        args += [cache_k, cache_v]
    in_specs += [lay(4, DA_HEAD_DIM), lay(1, DA_V_DIM)]
    args += [da_lambda, subln_g]
    grid_spec = pltpu.PrefetchScalarGridSpec(
        num_scalar_prefetch=1,
        grid=(nseq, nq),
        in_specs=in_specs,
        out_specs=pl.BlockSpec((TM, DA_V), lambda b, i, l: (b * nq + i, 0)),
        scratch_shapes=[pltpu.VMEM((n_ctx + seqlen, DA_QK), BF16), pltpu.VMEM((n_ctx + seqlen, 2 * DA_V), BF16)],
    )
    return pl.pallas_call(
        _attn_latent_kernel if latent else _attn_prompt_kernel,
        grid_spec=grid_spec,
        out_shape=jax.ShapeDtypeStruct((nseq * seqlen, DA_V), BF16),
        compiler_params=_cparams(("arbitrary", "arbitrary")),
        name="diff_attn_latent" if latent else "diff_attn_prompt",
    )(l_arr, *args)


def _combine_kernel(l_ref, xp_ref, xl_ref, h_ref, yf_ref, yb_ref, z_ref, u_ref, up_ref, un_ref, oap_ref, oal_ref,
                    mod_ref, cvw_ref, cvb_ref, cvg_ref, cvbeta_ref,
                    wg_ref, bg_ref, sg_ref, wa_ref, wb_ref, wc_ref, wo_ref, g2_ref, wr_ref, br_ref,
                    x1_ref, h2_ref, route_ref, cnt_ref, e1_ref, e2_ref, r1_ref, r2_ref, carry, upad):
    i = pl.program_id(0)

    @pl.when(i == 0)
    def _():
        carry[...] = jnp.zeros_like(carry)

    uc = _conformer_conv(i, u_ref, up_ref, un_ref, cvw_ref, cvb_ref, cvg_ref, cvbeta_ref, upad)

    y = (yf_ref[...].astype(F32) + yb_ref[...].astype(F32)) * _silu(z_ref[...].astype(F32))
    y = y * lax.rsqrt(jnp.mean(y * y, axis=-1, keepdims=True) + EPS) * sg_ref[...]
    br_a = _dot(y.astype(BF16), wa_ref[...])
    br_b = _dot(uc, wb_ref[...])
    br_c = _dot(_pair_read(i, oap_ref, oal_ref), wc_ref[...])
    hb = h_ref[...]

    def gate(n):
        return _sigmoid(_dot(hb, wg_ref[:, n * D:(n + 1) * D]) + bg_ref[:, n * D:(n + 1) * D])

    mix = gate(0) * br_a + gate(1) * br_b + gate(2) * br_c
    mixed = _dot(mix.astype(BF16), wo_ref[...])
    x1 = _pair_read(i, xp_ref, xl_ref) + mod_ref[2:3, :] * mixed
    x1_ref[...] = x1
    xn = x1 * lax.rsqrt(jnp.mean(x1 * x1, axis=-1, keepdims=True) + EPS)
    h2 = xn * g2_ref[...] * (1.0 + mod_ref[4:5, :]) + mod_ref[3:4, :]
    h2_ref[...] = h2

    h2_hi = h2.astype(BF16)
    h2_mid = (h2 - h2_hi.astype(F32)).astype(BF16)
    logits = _dot(jnp.concatenate([h2_hi, h2_mid, h2_hi], axis=1), wr_ref[...]) + br_ref[...]
    lane = lax.broadcasted_iota(jnp.int32, (TM, LANES), 1).astype(F32)

    def first_argmax(vals, vmax):
        return jnp.min(jnp.where(vals == vmax, lane, float(LANES)), axis=-1, keepdims=True)

    glog = jnp.where(lane < MOE_GROUPS, logits, NEG)
    gmax = jnp.max(glog, axis=-1, keepdims=True)
    gsel = first_argmax(glog, gmax)
    p_g = 1.0 / jnp.sum(jnp.exp(glog - gmax), axis=-1, keepdims=True)
    lo = ROUTER_LANE0 + MOE_EPG * gsel
    elog = jnp.where(jnp.logical_and(lane >= lo, lane < lo + MOE_EPG), logits, NEG)
    v1 = jnp.max(elog, axis=-1, keepdims=True)
    i1 = first_argmax(elog, v1)
    elog2 = jnp.where(lane == i1, NEG, elog)
    v2 = jnp.max(elog2, axis=-1, keepdims=True)
    i2 = first_argmax(elog2, v2)
    t2 = jnp.exp(v2 - v1)
    w1 = p_g / (1.0 + t2)
    w2 = p_g * t2 / (1.0 + t2)

    oh1 = lane == i1
    oh2 = lane == i2
    both = jnp.where(jnp.logical_or(oh1, oh2), 1.0, 0.0)
    r = lax.broadcasted_iota(jnp.int32, (TM, TM), 0)
    c = lax.broadcasted_iota(jnp.int32, (TM, TM), 1)
    strict_lower = jnp.where(c < r, 1.0, 0.0).astype(BF16)
    before = _dot(strict_lower, both.astype(BF16)) + carry[0:1, :]
    r1 = jnp.sum(jnp.where(oh1, before, 0.0), axis=-1, keepdims=True)
    r2 = jnp.sum(jnp.where(oh2, before, 0.0), axis=-1, keepdims=True)
    carry[...] = carry[...] + jnp.sum(both, axis=0, keepdims=True)
    cnt_ref[...] = carry[...]

    e1 = i1 - ROUTER_LANE0
    e2 = i2 - ROUTER_LANE0
    route = jnp.zeros((TM, LANES), F32)
    for n, val in enumerate((e1, e2, w1, w2, r1, r2)):
        route = jnp.where(lane == float(n), val, route)
    route_ref[...] = route
    route_t = route.T
    for n, ref in ((0, e1_ref), (1, e2_ref), (4, r1_ref), (5, r2_ref)):
        ref[...] = route_t[n:n + 1, :].astype(jnp.int32)


def _combine_call(l_arr, x_p, x_l, h, y_f, y_b, z, u, oa_p, oa_l, mod6, cv_w, cv_b, cv_g, cv_beta, w_gates, b_gates,
                  ssd_norm_g, w_br_ssd, w_br_conv, w_br_attn, w_out, norm2_g, w_router, b_router):
    tok = lambda n: pl.BlockSpec((TM, n), lambda i, l: (i, 0))
    lay = lambda *shape: pl.BlockSpec((None,) + shape, lambda i, l: (l[0],) + (0,) * len(shape))
    grid_spec = pltpu.PrefetchScalarGridSpec(
        num_scalar_prefetch=1,
        grid=(N_TILES,),
        in_specs=_pair_specs(D) + [tok(D), tok(D), tok(D), tok(D)] + _conformer_conv_specs() + _pair_specs(DA_V) + [
            pl.BlockSpec((None, None, 6, D), lambda i, l: (l[0], _mod_row(i), 0, 0)),
            lay(CONV_W, CONV_CH), lay(1, CONV_CH), lay(1, CONV_CH), lay(1, CONV_CH),
            lay(D, 3 * D), lay(1, 3 * D), lay(1, D), lay(D, D), lay(CONV_CH, D), lay(DA_V, D), lay(D, D),
            lay(1, D), lay(3 * D, LANES), lay(1, LANES),
        ],
        out_specs=[tok(D), tok(D), tok(LANES), pl.BlockSpec((SUBLANES, LANES), lambda i, l: (0, 0))]
        + [pl.BlockSpec((None, 1, TM), lambda i, l: (i, 0, 0))] * 4,
        scratch_shapes=[pltpu.VMEM((SUBLANES, LANES), F32), pltpu.VMEM((TM + 2 * CV_HALO, CONV_CH), F32)],
    )
    return pl.pallas_call(
        _combine_kernel,
        grid_spec=grid_spec,
        out_shape=[jax.ShapeDtypeStruct((T, D), F32), jax.ShapeDtypeStruct((T, D), F32),
                   jax.ShapeDtypeStruct((T, LANES), F32), jax.ShapeDtypeStruct((SUBLANES, LANES), F32)]
        + [jax.ShapeDtypeStruct((N_TILES, 1, TM), jnp.int32)] * 4,
        compiler_params=_cparams(("arbitrary",)),
        name="branch_combine_router",
    )(l_arr, x_p, x_l, h, y_f, y_b, z, u, u, u, oa_p, oa_l, mod6, cv_w, cv_b, cv_g, cv_beta, w_gates, b_gates,
      ssd_norm_g, w_br_ssd, w_br_conv, w_br_attn, w_out, norm2_g, w_router, b_router)


ROW_DMA_GROUP = 64


def _for_each_row(fn):
    def body(t, carry):
        fn(t)
        return carry

    lax.fori_loop(0, TM, body, 0, unroll=ROW_DMA_GROUP)


DISPATCH_SLOTS = 3


def _dispatch_kernel(e1_ref, e2_ref, r1_ref, r2_ref, ps_ref, pe_ref, h2_ref, xs_ref, zbuf, hbuf, in_sem, sem, zsem):
    i = pl.program_id(0)
    base = i * TM
    slot = i % DISPATCH_SLOTS

    def fetch(tile, s):
        return pltpu.make_async_copy(h2_ref.at[pl.ds(pl.multiple_of(tile * TM, TM), TM), :], hbuf.at[s], in_sem.at[s])

    def wait_rows(s):
        for _ in range(2):
            pltpu.make_async_copy(hbuf.at[s], xs_ref.at[pl.ds(0, TM), :], sem.at[s]).wait()

    @pl.when(i == 0)
    def _():
        fetch(0, 0).start()

    @pl.when(i == 0)
    def _():
        zbuf[...] = jnp.zeros_like(zbuf)

        def zero_fill(e):
            off = pl.multiple_of(pe_ref[e] - MOE_BLOCK, MOE_BLOCK)
            return pltpu.make_async_copy(zbuf, xs_ref.at[pl.ds(off, MOE_BLOCK), :], zsem)

        def unused_fill(b):
            off = pl.multiple_of(b * MOE_BLOCK, MOE_BLOCK)
            return pltpu.make_async_copy(zbuf, xs_ref.at[pl.ds(off, MOE_BLOCK), :], zsem)

        n_used = pe_ref[MOE_E - 1] // MOE_BLOCK
        for e in range(MOE_E):
            @pl.when(pe_ref[e] > ps_ref[e])
            def _():
                zero_fill(e).start()
        lax.fori_loop(n_used, N_SLOT_BLOCKS, lambda b, c: (unused_fill(b).start(), c)[1], 0)
        for e in range(MOE_E):
            @pl.when(pe_ref[e] > ps_ref[e])
            def _():
                zero_fill(e).wait()
        lax.fori_loop(n_used, N_SLOT_BLOCKS, lambda b, c: (unused_fill(b).wait(), c)[1], 0)

    @pl.when(i >= DISPATCH_SLOTS - 1)
    def _():
        wait_rows((i + 1) % DISPATCH_SLOTS)

    @pl.when(i + 1 < N_TILES)
    def _():
        fetch(i + 1, (i + 1) % DISPATCH_SLOTS).start()

    fetch(i, slot).wait()

    def issue(t):
        for e_ref, r_ref in ((e1_ref, r1_ref), (e2_ref, r2_ref)):
            dest = ps_ref[e_ref[base + t]] + r_ref[base + t]
            pltpu.make_async_copy(hbuf.at[slot, pl.ds(t, 1), :], xs_ref.at[pl.ds(dest, 1), :], sem.at[slot]).start()

    _for_each_row(issue)

    @pl.when(i == N_TILES - 1)
    def _():
        for back in range(DISPATCH_SLOTS - 1):
            wait_rows((i - back) % DISPATCH_SLOTS)


def _dispatch_call(e1, e2, r1, r2, pstart, pend, h2):
    grid_spec = pltpu.PrefetchScalarGridSpec(
        num_scalar_prefetch=6,
        grid=(N_TILES,),
        in_specs=[pl.BlockSpec(memory_space=pl.ANY)],
        out_specs=pl.BlockSpec(memory_space=pl.ANY),
        scratch_shapes=[pltpu.VMEM((MOE_BLOCK, D), F32), pltpu.VMEM((DISPATCH_SLOTS, TM, D), F32),
                        pltpu.SemaphoreType.DMA((DISPATCH_SLOTS,)), pltpu.SemaphoreType.DMA((DISPATCH_SLOTS,)),
                        pltpu.SemaphoreType.DMA(())],
    )
    return pl.pallas_call(
        _dispatch_kernel,
        grid_spec=grid_spec,
        out_shape=jax.ShapeDtypeStruct((N_SLOTS, D), F32),
        compiler_params=_cparams(("arbitrary",)),
        name="moe_dispatch",
    )(e1, e2, r1, r2, pstart, pend, h2)


def _moe_kernel(l_ref, be_ref, nb_ref, xs_ref, wg_ref, wu_ref, wd_ref, ys_ref, wg_s, wu_s, wd_s):
    i = pl.program_id(0)
    prev = be_ref[jnp.maximum(i - 1, 0)]

    @pl.when(jnp.logical_or(i == 0, be_ref[i] != prev))
    def _():
        wg_s[...] = wg_ref[...].astype(BF16)
        wu_s[...] = wu_ref[...].astype(BF16)
        wd_s[...] = wd_ref[...].astype(BF16)

    @pl.when(i < nb_ref[0])
    def _():
        xb = xs_ref[...].astype(BF16)
        hid = _silu(_dot(xb, wg_s[...])) * _dot(xb, wu_s[...])
        ys_ref[...] = _dot(hid.astype(BF16), wd_s[...])

    @pl.when(i >= nb_ref[0])
    def _():
        ys_ref[...] = jnp.zeros_like(ys_ref)


def _moe_call(l_arr, block_expert, n_used, xs, w_gate, w_up, w_down):
    wspec = lambda a, b: pl.BlockSpec((None, None, a, b), lambda i, l, be, nb: (l[0], be[i], 0, 0))
    grid_spec = pltpu.PrefetchScalarGridSpec(
        num_scalar_prefetch=3,
        grid=(N_SLOT_BLOCKS,),
        in_specs=[pl.BlockSpec((MOE_BLOCK, D), lambda i, l, be, nb: (jnp.minimum(i, nb[0] - 1), 0)),
                  wspec(D, MOE_HIDDEN), wspec(D, MOE_HIDDEN), wspec(MOE_HIDDEN, D)],
        out_specs=pl.BlockSpec((MOE_BLOCK, D), lambda i, l, be, nb: (i, 0)),
        scratch_shapes=[pltpu.VMEM((D, MOE_HIDDEN), BF16), pltpu.VMEM((D, MOE_HIDDEN), BF16),
                        pltpu.VMEM((MOE_HIDDEN, D), BF16)],
    )
    return pl.pallas_call(
        _moe_kernel,
        grid_spec=grid_spec,
        out_shape=jax.ShapeDtypeStruct((N_SLOTS, D), F32),
        compiler_params=_cparams(("arbitrary",)),
        name="moe_experts",
    )(l_arr, block_expert, n_used, xs, w_gate, w_up, w_down)


def _moe_combine_kernel(l_ref, e1_ref, e2_ref, r1_ref, r2_ref, ps_ref, x1_ref, route_ref, mod_ref, fg_ref, ys_ref,
                        op_ref, ol_ref, buf, sem, *, final):
    i = pl.program_id(0)
    slot = i % 2

    def gather_tile(tile, dst_slot):
        base = tile * TM

        def issue(t):
            for which, (e_ref, r_ref) in enumerate(((e1_ref, r1_ref), (e2_ref, r2_ref))):
                src = ps_ref[e_ref[base + t]] + r_ref[base + t]
                pltpu.make_async_copy(ys_ref.at[pl.ds(src, 1), :], buf.at[dst_slot, which, pl.ds(t, 1), :],
                                      sem.at[dst_slot]).start()

        _for_each_row(issue)

    @pl.when(i == 0)
    def _():
        gather_tile(0, 0)

    @pl.when(i + 1 < N_TILES)
    def _():
        gather_tile(i + 1, 1 - slot)

    for which in range(2):
        pltpu.make_async_copy(ys_ref.at[pl.ds(0, TM), :], buf.at[slot, which], sem.at[slot]).wait()
    w1 = route_ref[:, 2:3]
    w2 = route_ref[:, 3:4]
    y = buf[slot, 0] * w1 + buf[slot, 1] * w2
    x2 = x1_ref[...] + mod_ref[5:6, :] * y
    if final:
        x2 = x2 * lax.rsqrt(jnp.mean(x2 * x2, axis=-1, keepdims=True) + EPS) * fg_ref[...]

    @pl.when(i < N_PROMPT_TILES)
    def _():
        op_ref[...] = x2

    @pl.when(i >= N_PROMPT_TILES)
    def _():
        ol_ref[...] = x2


def _moe_combine_call(l_arr, e1, e2, r1, r2, pstart, x1, route, mod6, final_g, ys, final):
    grid_spec = pltpu.PrefetchScalarGridSpec(
        num_scalar_prefetch=6,
        grid=(N_TILES,),
        in_specs=[
            pl.BlockSpec((TM, D), lambda i, *_: (i, 0)),
            pl.BlockSpec((TM, LANES), lambda i, *_: (i, 0)),
            pl.BlockSpec((None, None, 6, D), lambda i, l, *_: (l[0], _mod_row(i), 0, 0)),
            pl.BlockSpec((1, D), lambda i, *_: (0, 0)),
            pl.BlockSpec(memory_space=pl.ANY),
        ],
        out_specs=_pair_specs(D),
        scratch_shapes=[pltpu.VMEM((2, 2, TM, D), F32), pltpu.SemaphoreType.DMA((2,))],
    )
    return pl.pallas_call(
        functools.partial(_moe_combine_kernel, final=final),
        grid_spec=grid_spec,
        out_shape=[jax.ShapeDtypeStruct((T_P, D), F32), jax.ShapeDtypeStruct((T_L, D), F32)],
        compiler_params=_cparams(("arbitrary",)),
        name="moe_combine_final" if final else "moe_combine",
    )(l_arr, e1, e2, r1, r2, pstart, x1, route, mod6, final_g, ys)


def _rope_tables():
    n = LATENT_LEN
    rows = n // GRID_W
    row = jnp.repeat(jnp.arange(rows), GRID_W).astype(F32)
    col = jnp.tile(jnp.arange(GRID_W), rows).astype(F32)
    axis_dim = DA_HEAD_DIM // 2
    inv_freq = 1.0 / (ROPE_BASE ** (jnp.arange(0, axis_dim, 2, dtype=F32) / axis_dim))
    ar, ac = row[:, None] * inv_freq, col[:, None] * inv_freq
    cos64 = jnp.concatenate([jnp.cos(ar), jnp.cos(ar), jnp.cos(ac), jnp.cos(ac)], axis=1)
    sin64 = jnp.concatenate([-jnp.sin(ar), jnp.sin(ar), -jnp.sin(ac), jnp.sin(ac)], axis=1)
    reps = DA_QK // DA_HEAD_DIM
    cos = jnp.concatenate([jnp.tile(cos64, (1, reps)), jnp.ones((TM, DA_QK), F32)], axis=0)
    sin = jnp.concatenate([jnp.tile(sin64, (1, reps)), jnp.zeros((TM, DA_QK), F32)], axis=0)
    return cos, sin


def _pad_lanes(a, n=LANES):
    return jnp.pad(a, [(0, 0)] * (a.ndim - 1) + [(0, n - a.shape[-1])])


def kernel(x_prompt, x_sample, cache_k, cache_v, state_ssd, c, c_ctx, w_ada, b_ada, norm1_g, norm2_g, w_in, b_in,
           ssd_conv_w, ssd_conv_b, ssd_dt_bias, ssd_a_log, ssd_d, ssd_norm_g, w_br_ssd, cv_dw_w, cv_dw_b, cv_ln_g,
           cv_ln_b, w_br_conv, da_lambda, da_subln_g, w_br_attn, w_out, moe_w_group, moe_b_group, moe_w_expert,
           moe_b_expert, moe_w_gate, moe_w_up, moe_w_down, final_g):
    L = DEPTH
    x_p, x_l = x_prompt.reshape(T_P, D), x_sample.reshape(T_L, D)

    cvec = jnp.concatenate([c, c_ctx[None, :], jnp.zeros((16 - N_LATENT_SEQ - 1, D), F32)], axis=0)
    mod6 = _ada_call(cvec, w_ada, b_ada).reshape(L, 16, 6, D)

    o_z, o_xbc, o_dt = 0, D, D + SSD_XBC
    o_glu = o_dt + 2 * SSD_HEADS
    o_q = o_glu + 2 * CONV_CH
    o_k, o_v, o_g = o_q + DA_QK, o_q + 2 * DA_QK, o_q + 2 * DA_QK + DA_V

    def regroup(w):
        return jnp.concatenate([w[..., o_z:o_dt], _pad_lanes(w[..., o_dt:o_glu]), w[..., o_glu:o_g]], axis=-1)

    w_proj = regroup(w_in).astype(BF16)
    b_proj = regroup(b_in).reshape(L, 1, N_PROJ)
    w_gates = w_in[..., o_g:].astype(BF16)
    b_gates = b_in[..., o_g:].reshape(L, 1, 3 * D)
    cos_tab, sin_tab = _rope_tables()

    tabs_f, tabs_b = _ssd_tables(True), _ssd_tables(False)
    dt_bias = _pad_lanes(ssd_dt_bias.reshape(L, 1, 2 * SSD_HEADS))
    a_log = _pad_lanes(ssd_a_log.reshape(L, 1, 2 * SSD_HEADS))
    dskip = jnp.repeat(ssd_d, SSD_HEAD_DIM, axis=-1).reshape(L, 1, D)
    hp = np.arange(D) // SSD_HEAD_DIM
    e_f, e_b = (jnp.asarray(np.tile(np.arange(LANES)[:, None] == ho + hp[None, :], (3, 1)).astype(np.float32))
                .astype(BF16) for ho in (0, SSD_HEADS))
    h0t = jnp.transpose(state_ssd, (1, 0, 2, 5, 3, 4)).reshape(L, N_LATENT_SEQ, 2, SSD_STATE, D)
    h0t = jnp.concatenate([h0t, jnp.zeros((L, 1, 2, SSD_STATE, D), F32)], axis=1)

    w_router = _pad_lanes(jnp.concatenate([moe_w_group, moe_w_expert], axis=-1))
    wr_hi = w_router.astype(BF16)
    wr_mid = (w_router - wr_hi.astype(F32)).astype(BF16)
    w_router = jnp.concatenate([wr_hi, wr_hi, wr_mid], axis=1)
    b_router = _pad_lanes(jnp.concatenate([moe_b_group, moe_b_expert], axis=-1)).reshape(L, 1, LANES)
    w_br_ssd_b, w_br_conv_b = w_br_ssd.astype(BF16), w_br_conv.astype(BF16)
    w_br_attn_b, w_out_b = w_br_attn.astype(BF16), w_out.astype(BF16)
    r3 = lambda a: a.reshape(L, 1, a.shape[-1])

    ks_new, vs_new, ss_new = [], [], []
    for layer in range(L):
        l_arr = jnp.full((1,), layer, jnp.int32)
        h, z, xbc, dt, u, q, k, v, k_cache, v_cache = _inproj_call(l_arr, x_p, x_l, mod6, r3(norm1_g), w_proj, b_proj,
                                                                   cos_tab, sin_tab)
        y_f, st_f, xc = _ssd_call(l_arr, tabs_f, xbc, dt, ssd_conv_w, r3(ssd_conv_b), dt_bias, a_log, dskip,
                                  e_f, h0t, None, True)
        y_b, st_b = _ssd_call(l_arr, tabs_b, None, dt, None, None, dt_bias, a_log, None, e_b, h0t, xc, False)
        oa_p = _attn_call(l_arr, q, k, v, None, None, da_lambda, r3(da_subln_g), latent=False)
        oa_l = _attn_call(l_arr, q, k, v, cache_k, cache_v, da_lambda, r3(da_subln_g), latent=True)
        x1, h2, route, counts, e1, e2, r1, r2 = _combine_call(
            l_arr, x_p, x_l, h, y_f, y_b, z, u, oa_p, oa_l, mod6, cv_dw_w, r3(cv_dw_b), r3(cv_ln_g), r3(cv_ln_b),
            w_gates, b_gates, r3(ssd_norm_g), w_br_ssd_b, w_br_conv_b, w_br_attn_b, w_out_b, r3(norm2_g),
            w_router, b_router)
        e1, e2, r1, r2 = (a.reshape(T) for a in (e1, e2, r1, r2))
        cnt = counts[0, ROUTER_LANE0:ROUTER_LANE0 + MOE_E].astype(jnp.int32)
        pcnt = (cnt + MOE_BLOCK - 1) // MOE_BLOCK * MOE_BLOCK
        pend = jnp.cumsum(pcnt)
        pstart = pend - pcnt
        blk0 = jnp.arange(N_SLOT_BLOCKS, dtype=jnp.int32) * MOE_BLOCK
        block_expert = jnp.minimum(jnp.sum((pend[None, :] <= blk0[:, None]).astype(jnp.int32), axis=1), MOE_E - 1)
        n_used = pend[-1:] // MOE_BLOCK

        xs = _dispatch_call(e1, e2, r1, r2, pstart, pend, h2)
        ys = _moe_call(l_arr, block_expert, n_used, xs, moe_w_gate, moe_w_up, moe_w_down)
        x_p, x_l = _moe_combine_call(l_arr, e1, e2, r1, r2, pstart, x1, route, mod6, final_g.reshape(1, D), ys,
                                     layer == L - 1)

        ks_new.append(k_cache)
        vs_new.append(v_cache)
        s = jnp.stack([st_f[:N_PROMPT_SEQ], st_b[:N_PROMPT_SEQ]], axis=1)
        s = s.reshape(N_PROMPT_SEQ, 2, SSD_STATE, SSD_HEADS, SSD_HEAD_DIM)
        ss_new.append(jnp.transpose(s, (0, 1, 3, 4, 2)))

    y_prompt = x_p.reshape(N_PROMPT_SEQ, PROMPT_LEN, D)
    y_sample = x_l.reshape(N_LATENT_SEQ, LATENT_LEN, D)
    return (y_prompt, y_sample, jnp.stack(ks_new, axis=1), jnp.stack(vs_new, axis=1), jnp.stack(ss_new, axis=1))
```

```python
import functools
import math

import jax
import jax.numpy as jnp
import numpy as np
from jax import lax
from jax.experimental import pallas as pl
from jax.experimental.pallas import tpu as pltpu

D = 1024
DEPTH = 4
N_PROMPT_SEQ, PROMPT_LEN = 16, 256
N_LATENT_SEQ, LATENT_LEN = 8, 2048
PAST_LEN = 512
T_P = N_PROMPT_SEQ * PROMPT_LEN
T_L = N_LATENT_SEQ * LATENT_LEN
T = T_P + T_L
GRID_W = 64
EPS = 1e-6
SSD_HEADS, SSD_HEAD_DIM, SSD_STATE, SSD_GROUPS = 16, 64, 64, 2
SSD_CONV_W = 5
SSD_CHUNK = 128
SSD_XBC = D + 2 * SSD_GROUPS * SSD_STATE
CONV_CH, CONV_W = 512, 31
DA_HEADS, DA_HEAD_DIM, DA_V_DIM = 4, 64, 128
DA_QK = DA_HEADS * 2 * DA_HEAD_DIM
DA_V = DA_HEADS * DA_V_DIM
ROPE_BASE = 10000.0
MOE_GROUPS, MOE_EPG, MOE_E, MOE_HIDDEN = 4, 8, 32, 512
ROUTER_LANE0 = MOE_GROUPS

LANES = 128
SUBLANES = 8
TM = 256
N_TILES = T // TM
N_PROMPT_TILES = T_P // TM
TILES_PER_LATENT_SEQ = LATENT_LEN // TM
MOE_BLOCK = 512
N_SLOT_BLOCKS = (2 * T) // MOE_BLOCK + MOE_E
N_SLOTS = N_SLOT_BLOCKS * MOE_BLOCK
VMEM_LIMIT = 56 * 1024 * 1024

F32 = jnp.float32
BF16 = jnp.bfloat16
HI = lax.Precision.HIGHEST
NEG = -1e30


def _cparams(sem, vmem=VMEM_LIMIT):
    return pltpu.CompilerParams(dimension_semantics=sem, vmem_limit_bytes=vmem)


def _mod_row(i):
    return jnp.where(i < N_PROMPT_TILES, N_LATENT_SEQ, (i - N_PROMPT_TILES) // TILES_PER_LATENT_SEQ)


def _pair_specs(n, tile_arg=0):
    def prompt(*a):
        return (jnp.minimum(a[tile_arg], N_PROMPT_TILES - 1), 0)

    def latent(*a):
        return (jnp.maximum(a[tile_arg] - N_PROMPT_TILES, 0), 0)

    return [pl.BlockSpec((TM, n), prompt), pl.BlockSpec((TM, n), latent)]


def _pair_read(i, p_ref, l_ref):
    return jnp.where(i < N_PROMPT_TILES, p_ref[...], l_ref[...])


def _silu(x):
    return x * (1.0 / (1.0 + jnp.exp(-x)))


def _sigmoid(x):
    return 1.0 / (1.0 + jnp.exp(-x))


def _softplus(x):
    return jnp.maximum(x, 0.0) + jnp.log(1.0 + jnp.exp(-jnp.abs(x)))


def _dot(a, b, **kw):
    return jnp.dot(a, b, preferred_element_type=F32, **kw)


def _split3(x):
    hi = x.astype(BF16)
    r1 = x - hi.astype(F32)
    mid = r1.astype(BF16)
    lo = (r1 - mid.astype(F32)).astype(BF16)
    return jnp.concatenate([hi, mid, lo], axis=1)


def _dot_nt(a, b):
    return lax.dot_general(a, b, (((1,), (1,)), ((), ())), preferred_element_type=F32)


def _ada_kernel(c_ref, w_ref, b_ref, o_ref):
    cs = _silu(c_ref[...])
    o_ref[...] = _dot(cs, w_ref[...], precision=HI) + b_ref[...]


def _ada_call(cvec, w_ada, b_ada):
    nj = 6
    return pl.pallas_call(
        _ada_kernel,
        out_shape=jax.ShapeDtypeStruct((DEPTH, 16, 6 * D), F32),
        grid=(DEPTH, nj),
        in_specs=[
            pl.BlockSpec((16, D), lambda l, j: (0, 0)),
            pl.BlockSpec((None, D, D), lambda l, j: (l, 0, j)),
            pl.BlockSpec((None, 1, D), lambda l, j: (l, 0, j)),
        ],
        out_specs=pl.BlockSpec((None, 16, D), lambda l, j: (l, 0, j)),
        compiler_params=_cparams(("arbitrary", "arbitrary")),
        name="ada_mod",
    )(cvec, w_ada, b_ada.reshape(DEPTH, 1, 6 * D))


_C_Z = (0, D)
_C_XBC = (_C_Z[1], _C_Z[1] + SSD_XBC)
_C_DT = (_C_XBC[1], _C_XBC[1] + LANES)
_C_GLU = (_C_DT[1], _C_DT[1] + 2 * CONV_CH)
_C_Q = (_C_GLU[1], _C_GLU[1] + DA_QK)
_C_K = (_C_Q[1], _C_Q[1] + DA_QK)
_C_V = (_C_K[1], _C_K[1] + DA_V)
N_PROJ = _C_V[1]


def _swap16(x):
    cols = []
    for c in range(x.shape[1] // LANES):
        xc = x[:, c * LANES:(c + 1) * LANES]
        lane = lax.broadcasted_iota(jnp.int32, xc.shape, 1)
        from_right = pltpu.roll(xc, LANES - 16, axis=1)
        from_left = pltpu.roll(xc, 16, axis=1)
        cols.append(jnp.where((lane >> 4) % 2 == 0, from_right, from_left))
    return jnp.concatenate(cols, axis=1)


def _inproj_kernel(l_ref, xp_ref, xl_ref, mod_ref, g_ref, w_ref, b_ref, cos_ref, sin_ref,
                   h_ref, z_ref, xbc_ref, dt_ref, u_ref, q_ref, k_ref, v_ref, kc_ref, vc_ref):
    i = pl.program_id(0)
    x = _pair_read(i, xp_ref, xl_ref)
    xn = x * lax.rsqrt(jnp.mean(x * x, axis=-1, keepdims=True) + EPS)
    h = xn * g_ref[...] * (1.0 + mod_ref[1:2, :]) + mod_ref[0:1, :]
    hb = h.astype(BF16)
    h_ref[...] = hb

    def proj(c):
        return _dot(hb, w_ref[:, c[0]:c[1]]) + b_ref[:, c[0]:c[1]]

    z_ref[...] = proj(_C_Z).astype(BF16)
    xbc_ref[...] = proj(_C_XBC)
    dt_ref[...] = proj(_C_DT)
    glu = proj(_C_GLU)
    u_ref[...] = glu[:, :CONV_CH] * _sigmoid(glu[:, CONV_CH:])
    cos = cos_ref[...]
    sin = sin_ref[...]
    q = proj(_C_Q)
    q = q * cos + _swap16(q) * sin
    q_ref[...] = (q * (DA_HEAD_DIM ** -0.5 * math.log2(math.e))).astype(BF16)
    k = proj(_C_K)
    k = k * cos + _swap16(k) * sin
    k_ref[...] = k.astype(BF16)
    v = proj(_C_V)
    v_ref[...] = v.astype(BF16)

    @pl.when(i < N_PROMPT_TILES)
    def _():
        kc_ref[...] = k.reshape(TM, DA_HEADS, LANES)
        vc_ref[...] = v.reshape(TM, DA_HEADS, LANES)


def _inproj_call(l_arr, x_p, x_l, mod6, norm1_g, w_proj, b_proj, cos_tab, sin_tab):
    tok = lambda n: pl.BlockSpec((TM, n), lambda i, l: (i, 0))
    rope_blk = lambda i, l: (jnp.where(i < N_PROMPT_TILES, TILES_PER_LATENT_SEQ,
                                       (i - N_PROMPT_TILES) % TILES_PER_LATENT_SEQ), 0)
    lay = lambda *shape: pl.BlockSpec((None,) + shape, lambda i, l: (l[0],) + (0,) * len(shape))
    grid_spec = pltpu.PrefetchScalarGridSpec(
        num_scalar_prefetch=1,
        grid=(N_TILES,),
        in_specs=_pair_specs(D) + [
            pl.BlockSpec((None, None, 6, D), lambda i, l: (l[0], _mod_row(i), 0, 0)),
            lay(1, D), lay(D, N_PROJ), lay(1, N_PROJ),
            pl.BlockSpec((TM, DA_QK), rope_blk), pl.BlockSpec((TM, DA_QK), rope_blk),
        ],
        out_specs=[tok(D), tok(D), tok(SSD_XBC), tok(LANES), tok(CONV_CH), tok(DA_QK), tok(DA_QK), tok(DA_V)]
        + [pl.BlockSpec((None, PROMPT_LEN, DA_HEADS, LANES),
                        lambda i, l: (jnp.minimum(i, N_PROMPT_SEQ - 1), 0, 0, 0))] * 2,
    )
    sds = lambda n, dt: jax.ShapeDtypeStruct((T, n), dt)
    cache = jax.ShapeDtypeStruct((N_PROMPT_SEQ, PROMPT_LEN, DA_HEADS, LANES), F32)
    return pl.pallas_call(
        _inproj_kernel,
        grid_spec=grid_spec,
        out_shape=[sds(D, BF16), sds(D, BF16), sds(SSD_XBC, F32), sds(LANES, F32), sds(CONV_CH, F32),
                   sds(DA_QK, BF16), sds(DA_QK, BF16), sds(DA_V, BF16), cache, cache],
        compiler_params=_cparams(("arbitrary",)),
        name="inproj",
    )(l_arr, x_p, x_l, mod6, norm1_g, w_proj, b_proj, cos_tab, sin_tab)


SSD_STEP = TM
CHUNKS_PER_STEP = SSD_STEP // SSD_CHUNK
N_SSD_STEPS = T // SSD_STEP
N_PROMPT_STEPS = T_P // SSD_STEP
STEPS_PER_PROMPT = PROMPT_LEN // SSD_STEP
STEPS_PER_LATENT = LATENT_LEN // SSD_STEP
HALO = SUBLANES
STATE_DUMP = N_PROMPT_SEQ


def _ssd_tables(fwd):
    cidx, flags, h0, so = (np.zeros((N_SSD_STEPS,), np.int32) for _ in range(4))
    for j in range(N_SSD_STEPS):
        c = j if fwd else N_SSD_STEPS - 1 - j
        if c < N_PROMPT_STEPS:
            seq, pos, n = c // STEPS_PER_PROMPT, c % STEPS_PER_PROMPT, STEPS_PER_PROMPT
            h0i, soi = N_LATENT_SEQ, seq
        else:
            cc = c - N_PROMPT_STEPS
            seq, pos, n = cc // STEPS_PER_LATENT, cc % STEPS_PER_LATENT, STEPS_PER_LATENT
            h0i, soi = seq, STATE_DUMP
        first = pos == 0 if fwd else pos == n - 1
        cidx[j] = c
        flags[j] = int(first) | (int(pos > 0) << 1) | (int(pos < n - 1) << 2)
        h0[j] = h0i
        so[j] = soi
    return [jnp.asarray(a) for a in (cidx, flags, h0, so)]


def _ssd_conv(flags, xc_ref, xp_ref, xn_ref, cw_ref, cb_ref, xpad):
    q = SSD_STEP
    xpad[0:HALO, :] = jnp.where(((flags >> 1) & 1) == 1, xp_ref[...], 0.0)
    xpad[HALO:HALO + q, :] = xc_ref[...]
    xpad[HALO + q:HALO + q + HALO, :] = jnp.where(((flags >> 2) & 1) == 1, xn_ref[...], 0.0)
    acc = jnp.zeros((q, SSD_XBC), F32) + cb_ref[...]
    pad = SSD_CONV_W // 2
    xp = xpad[...]
    rows = q + 2 * HALO
    for k in range(SSD_CONV_W):
        shifted = xp if k == pad else pltpu.roll(xp, (pad - k) % rows, axis=0)
        acc = acc + cw_ref[k:k + 1, :] * shifted[HALO:HALO + q, :]
    return _silu(acc)


def _ssd_scan_step(fwd, flags, xc, dt_ref, dtb_ref, alog_ref, dskip_ref, e_ref, h0_ref, y_ref, so_ref, state, lat_s):
    @pl.when((flags & 1) == 1)
    def _():
        state[...] = h0_ref[...]

    dt = _softplus(dt_ref[...] + dtb_ref[...])
    da = dt * (-jnp.exp(alog_ref[...]))
    st = state[...]
    for k in (range(CHUNKS_PER_STEP) if fwd else reversed(range(CHUNKS_PER_STEP))):
        rows = slice(k * SSD_CHUNK, (k + 1) * SSD_CHUNK)
        st = _ssd_scan_chunk(fwd, xc[rows], dt[rows], da[rows], st, dskip_ref, e_ref, y_ref, rows, lat_s.at[k])
    state[...] = st
    so_ref[...] = st


def _ssd_scan_chunk(fwd, xc, dt, da, st_all, dskip_ref, e_ref, y_ref, rows, lat_s):
    q = SSD_CHUNK
    ho = 0 if fwd else SSD_HEADS
    xs = xc[:, :D]
    bm = xc[:, D:D + LANES]
    cm = xc[:, D + LANES:D + 2 * LANES]
    row = lax.broadcasted_iota(jnp.int32, (q, q), 0)
    col = lax.broadcasted_iota(jnp.int32, (q, q), 1)
    tri = (col <= row) if fwd else (col >= row)
    p = _dot(jnp.where(tri, 1.0, 0.0).astype(BF16), _split3(da))
    la = ((p[:, :LANES] + p[:, LANES:2 * LANES]) + p[:, 2 * LANES:]) * math.log2(math.e)
    lat_s[...] = la.T
    e3 = e_ref[...]
    la_exp = _dot(_split3(la), e3)
    dt_exp = _dot(_split3(dt), e3)
    la_end = la_exp[q - 1:q, :] if fwd else la_exp[0:1, :]
    decay_end = jnp.exp2(la_end - la_exp)
    chunk_decay = jnp.exp2(la_end)
    decay_in = jnp.exp2(la_exp)
    xdt = xs * dt_exp
    xdt_b = xdt.astype(BF16)
    xdtw_b = (xdt * decay_end).astype(BF16)
    bmt = bm.T

    half = LANES // 2
    lane = lax.broadcasted_iota(jnp.int32, (q, LANES), 1)
    hpg = SSD_HEADS // SSD_GROUPS
    st_new = []
    for g in range(SSD_GROUPS):
        c_g = cm[:, g * half:(g + 1) * half].astype(BF16)
        b_g = bm[:, g * half:(g + 1) * half].astype(BF16)
        bt_g = bmt[g * half:(g + 1) * half, :].astype(BF16)
        cb = _dot_nt(c_g, b_g)
        for pp in range(hpg // 2):
            h_a = g * hpg + 2 * pp
            sl = slice(h_a * SSD_HEAD_DIM, (h_a + 2) * SSD_HEAD_DIM)
            xdt_p = xdt_b[:, sl]
            yd = []
            for h in (ho + h_a, ho + h_a + 1):
                seg = la[:, h:h + 1] - lat_s[h:h + 1, :]
                s_h = (cb * jnp.exp2(jnp.where(tri, seg, NEG))).astype(BF16)
                yd.append(_dot(s_h, xdt_p))
            y_diag = jnp.where(lane < half, yd[0], yd[1])
            st_in = st_all[:, sl]
            y = y_diag + _dot(c_g, st_in.astype(BF16)) * decay_in[:, sl]
            st_new.append(st_in * chunk_decay[:, sl] + _dot(bt_g, xdtw_b[:, sl]))
            if fwd:
                y = y + xs[:, sl] * dskip_ref[:, sl]
            y_ref[rows, sl] = y.astype(BF16)
    return jnp.concatenate(st_new, axis=1)


def _ssd_fwd_kernel(l_ref, cidx_ref, flags_ref, h0i_ref, soi_ref,
                    xc_ref, xp_ref, xn_ref, cw_ref, cb_ref, dt_ref, dtb_ref, alog_ref, dskip_ref, e_ref, h0_ref,
                    y_ref, so_ref, xco_ref, state, lat_s, xpad):
    flags = flags_ref[pl.program_id(0)]
    xc = _ssd_conv(flags, xc_ref, xp_ref, xn_ref, cw_ref, cb_ref, xpad)
    xco_ref[...] = xc
    _ssd_scan_step(True, flags, xc, dt_ref, dtb_ref, alog_ref, dskip_ref, e_ref, h0_ref, y_ref, so_ref, state, lat_s)


def _ssd_bwd_kernel(l_ref, cidx_ref, flags_ref, h0i_ref, soi_ref,
                    xc_ref, dt_ref, dtb_ref, alog_ref, e_ref, h0_ref, y_ref, so_ref, state, lat_s):
    flags = flags_ref[pl.program_id(0)]
    _ssd_scan_step(False, flags, xc_ref[...], dt_ref, dtb_ref, alog_ref, None, e_ref, h0_ref, y_ref, so_ref,
                   state, lat_s)


def _ssd_call(l_arr, tabs, xbc, dt, conv_w, conv_b, dt_bias, a_log, dskip, e_mat, h0t, xc_in, fwd):
    nb8 = T // HALO
    per = SSD_STEP // HALO
    cur = lambda j, l, ci, fl, h0, so: (ci[j], 0)
    prev = lambda j, l, ci, fl, h0, so: (jnp.maximum(ci[j] * per - 1, 0), 0)
    nxt = lambda j, l, ci, fl, h0, so: (jnp.minimum((ci[j] + 1) * per, nb8 - 1), 0)
    lay = lambda *shape: pl.BlockSpec((None,) + shape, lambda j, l, *_: (l[0],) + (0,) * len(shape))
    chunk = lambda n: pl.BlockSpec((SSD_STEP, n), cur)
    common_in = [chunk(LANES), lay(1, LANES), lay(1, LANES)]
    tail_in = [pl.BlockSpec((3 * LANES, D), lambda j, *_: (0, 0)),
               pl.BlockSpec((None, None, None, SSD_STATE, D),
                            lambda j, l, ci, fl, h0, so: (l[0], h0[j], 0 if fwd else 1, 0, 0))]
    out_specs = [chunk(D), pl.BlockSpec((None, SSD_STATE, D), lambda j, l, ci, fl, h0, so: (so[j], 0, 0))]
    out_shape = [jax.ShapeDtypeStruct((T, D), BF16), jax.ShapeDtypeStruct((N_PROMPT_SEQ + 1, SSD_STATE, D), F32)]
    scratch = [pltpu.VMEM((SSD_STATE, D), F32), pltpu.VMEM((CHUNKS_PER_STEP, LANES, SSD_CHUNK), F32)]
    if fwd:
        in_specs = ([chunk(SSD_XBC), pl.BlockSpec((HALO, SSD_XBC), prev), pl.BlockSpec((HALO, SSD_XBC), nxt),
                     lay(SSD_CONV_W, SSD_XBC), lay(1, SSD_XBC)] + common_in + [lay(1, D)] + tail_in)
        args = (xbc, xbc, xbc, conv_w, conv_b, dt, dt_bias, a_log, dskip, e_mat, h0t)
        out_specs.append(chunk(SSD_XBC))
        out_shape.append(jax.ShapeDtypeStruct((T, SSD_XBC), F32))
        scratch.append(pltpu.VMEM((SSD_STEP + 2 * HALO, SSD_XBC), F32))
    else:
        in_specs = [chunk(SSD_XBC)] + common_in + tail_in
        args = (xc_in, dt, dt_bias, a_log, e_mat, h0t)
    grid_spec = pltpu.PrefetchScalarGridSpec(num_scalar_prefetch=5, grid=(N_SSD_STEPS,), in_specs=in_specs,
                                             out_specs=out_specs, scratch_shapes=scratch)
    return pl.pallas_call(
        _ssd_fwd_kernel if fwd else _ssd_bwd_kernel,
        grid_spec=grid_spec,
        out_shape=out_shape,
        compiler_params=_cparams(("arbitrary",)),
        name="ssd_scan_fwd" if fwd else "ssd_scan_bwd",
    )(l_arr, *tabs, *args)


CV_HALO = 16


def _conformer_conv(i, uc_ref, up_ref, un_ref, w_ref, b_ref, g_ref, beta_ref, upad):
    pos = (i - N_PROMPT_TILES) % TILES_PER_LATENT_SEQ
    is_prompt = i < N_PROMPT_TILES
    no_l = jnp.logical_or(is_prompt, pos == 0)
    no_r = jnp.logical_or(is_prompt, pos == TILES_PER_LATENT_SEQ - 1)
    upad[0:CV_HALO, :] = jnp.where(no_l, 0.0, up_ref[...])
    upad[CV_HALO:CV_HALO + TM, :] = uc_ref[...]
    upad[CV_HALO + TM:CV_HALO + TM + CV_HALO, :] = jnp.where(no_r, 0.0, un_ref[...])
    acc = jnp.zeros((TM, CONV_CH), F32) + b_ref[...]
    pad = CONV_W // 2
    up = upad[...]
    rows = TM + 2 * CV_HALO
    for rot in range(SUBLANES):
        taps = [k for k in range(CONV_W) if (CV_HALO - pad + k) % SUBLANES == rot]
        rolled = up if rot == 0 else pltpu.roll(up, rows - rot, axis=0)
        for k in taps:
            base = CV_HALO - pad + k - rot
            acc = acc + w_ref[k:k + 1, :] * rolled[base:base + TM, :]
    mu = jnp.mean(acc, axis=-1, keepdims=True)
    xc = acc - mu
    var = jnp.mean(xc * xc, axis=-1, keepdims=True)
    y = xc * lax.rsqrt(var + EPS) * g_ref[...] + beta_ref[...]
    return _silu(y).astype(BF16)


def _conformer_conv_specs():
    per = TM // CV_HALO
    nb = T // CV_HALO
    return [pl.BlockSpec((TM, CONV_CH), lambda i, l: (i, 0)),
            pl.BlockSpec((CV_HALO, CONV_CH), lambda i, l: (jnp.maximum(i * per - 1, 0), 0)),
            pl.BlockSpec((CV_HALO, CONV_CH), lambda i, l: (jnp.minimum((i + 1) * per, nb - 1), 0))]


def _lambda_terms(l_ref, lam_ref):
    lf = jnp.full((1, 1), l_ref[0], jnp.int32).astype(F32)
    lam_init = 0.8 - 0.6 * jnp.exp(-0.3 * lf)
    p = lam_ref[...]
    s1 = jnp.sum(p[0:1, :] * p[1:2, :], axis=-1, keepdims=True)
    s2 = jnp.sum(p[2:3, :] * p[3:4, :], axis=-1, keepdims=True)
    lam = jnp.exp(s1) - jnp.exp(s2) + lam_init
    return lam, 1.0 - lam_init


def _attn_body(l_ref, q_ref, k_ref, v_ref, ck_ref, cv_ref, lam_ref, g_ref, o_ref, k_s, v_s, n_ctx):
    @pl.when(pl.program_id(1) == 0)
    def _():
        k_s[n_ctx:, :] = k_ref[...]
        for h in range(DA_HEADS):
            sl = slice(h * LANES, (h + 1) * LANES)
            if n_ctx:
                k_s[0:n_ctx, sl] = ck_ref[:, h, :].astype(BF16)
                v_s[0:n_ctx, 2 * h * LANES:(2 * h + 1) * LANES] = cv_ref[:, h, :].astype(BF16)
            v_s[n_ctx:, 2 * h * LANES:(2 * h + 1) * LANES] = v_ref[:, sl]
            v_s[:, (2 * h + 1) * LANES:(2 * h + 2) * LANES] = jnp.ones((v_s.shape[0], LANES), BF16)

    lam, out_scale = _lambda_terms(l_ref, lam_ref)
    tq = q_ref.shape[0]
    lane = lax.broadcasted_iota(jnp.int32, (tq, LANES), 1)
    zero = jnp.zeros((tq, LANES), BF16)
    for h in range(DA_HEADS):
        sl = slice(h * LANES, (h + 1) * LANES)
        qh = q_ref[:, sl]
        kh = k_s[:, sl]
        vh = v_s[:, 2 * h * LANES:(2 * h + 2) * LANES]
        outs = []
        for c in range(2):
            in_c = (lane < DA_HEAD_DIM) if c == 0 else (lane >= DA_HEAD_DIM)
            s = _dot_nt(jnp.where(in_c, qh, zero), kh)
            m = jnp.max(s, axis=-1, keepdims=True)
            pv = _dot(jnp.exp2(s - m).astype(BF16), vh)
            outs.append(pv[:, :LANES] / pv[:, LANES:])
        o = outs[0] - lam * outs[1]
        o = o * lax.rsqrt(jnp.mean(o * o, axis=-1, keepdims=True) + EPS)
        o_ref[:, sl] = (o * g_ref[...] * out_scale).astype(BF16)


def _attn_prompt_kernel(l_ref, q_ref, k_ref, v_ref, lam_ref, g_ref, o_ref, k_s, v_s):
    _attn_body(l_ref, q_ref, k_ref, v_ref, None, None, lam_ref, g_ref, o_ref, k_s, v_s, 0)


def _attn_latent_kernel(l_ref, q_ref, k_ref, v_ref, ck_ref, cv_ref, lam_ref, g_ref, o_ref, k_s, v_s):
    _attn_body(l_ref, q_ref, k_ref, v_ref, ck_ref, cv_ref, lam_ref, g_ref, o_ref, k_s, v_s, PAST_LEN)


def _attn_call(l_arr, q, k, v, cache_k, cache_v, da_lambda, subln_g, latent):
    if latent:
        nseq, seqlen, n_ctx = N_LATENT_SEQ, LATENT_LEN, PAST_LEN
        tile0, seq0 = N_PROMPT_TILES, T_P // LATENT_LEN
    else:
        nseq, seqlen, n_ctx = N_PROMPT_SEQ, PROMPT_LEN, 0
        tile0, seq0 = 0, 0
    nq = seqlen // TM
    lay = lambda *shape: pl.BlockSpec((None,) + shape, lambda b, i, l: (l[0],) + (0,) * len(shape))
    in_specs = [
        pl.BlockSpec((TM, DA_QK), lambda b, i, l: (tile0 + b * nq + i, 0)),
        pl.BlockSpec((seqlen, DA_QK), lambda b, i, l: (seq0 + b, 0)),
        pl.BlockSpec((seqlen, DA_V), lambda b, i, l: (seq0 + b, 0)),
    ]
    args = [q, k, v]
    if latent:
        ctx = pl.BlockSpec((None, None, PAST_LEN, DA_HEADS, LANES), lambda b, i, l: (b, l[0], 0, 0, 0))
        in_specs += [ctx, ctx]
        args += [cache_k, cache_v]
    in_specs += [lay(4, DA_HEAD_DIM), lay(1, DA_V_DIM)]
    args += [da_lambda, subln_g]
    grid_spec = pltpu.PrefetchScalarGridSpec(
        num_scalar_prefetch=1,
        grid=(nseq, nq),
        in_specs=in_specs,
        out_specs=pl.BlockSpec((TM, DA_V), lambda b, i, l: (b * nq + i, 0)),
        scratch_shapes=[pltpu.VMEM((n_ctx + seqlen, DA_QK), BF16), pltpu.VMEM((n_ctx + seqlen, 2 * DA_V), BF16)],
    )
    return pl.pallas_call(
        _attn_latent_kernel if latent else _attn_prompt_kernel,
        grid_spec=grid_spec,
        out_shape=jax.ShapeDtypeStruct((nseq * seqlen, DA_V), BF16),
        compiler_params=_cparams(("arbitrary", "arbitrary")),
        name="diff_attn_latent" if latent else "diff_attn_prompt",
    )(l_arr, *args)


def _combine_kernel(l_ref, xp_ref, xl_ref, h_ref, yf_ref, yb_ref, z_ref, u_ref, up_ref, un_ref, oap_ref, oal_ref,
                    mod_ref, cvw_ref, cvb_ref, cvg_ref, cvbeta_ref,
                    wg_ref, bg_ref, sg_ref, wa_ref, wb_ref, wc_ref, wo_ref, g2_ref, wr_ref, br_ref,
                    x1_ref, h2_ref, route_ref, cnt_ref, e1_ref, e2_ref, r1_ref, r2_ref, carry, upad):
    i = pl.program_id(0)

    @pl.when(i == 0)
    def _():
        carry[...] = jnp.zeros_like(carry)

    uc = _conformer_conv(i, u_ref, up_ref, un_ref, cvw_ref, cvb_ref, cvg_ref, cvbeta_ref, upad)

    y = (yf_ref[...].astype(F32) + yb_ref[...].astype(F32)) * _silu(z_ref[...].astype(F32))
    y = y * lax.rsqrt(jnp.mean(y * y, axis=-1, keepdims=True) + EPS) * sg_ref[...]
    br_a = _dot(y.astype(BF16), wa_ref[...])
    br_b = _dot(uc, wb_ref[...])
    br_c = _dot(_pair_read(i, oap_ref, oal_ref), wc_ref[...])
    hb = h_ref[...]

    def gate(n):
        return _sigmoid(_dot(hb, wg_ref[:, n * D:(n + 1) * D]) + bg_ref[:, n * D:(n + 1) * D])

    mix = gate(0) * br_a + gate(1) * br_b + gate(2) * br_c
    mixed = _dot(mix.astype(BF16), wo_ref[...])
    x1 = _pair_read(i, xp_ref, xl_ref) + mod_ref[2:3, :] * mixed
    x1_ref[...] = x1
    xn = x1 * lax.rsqrt(jnp.mean(x1 * x1, axis=-1, keepdims=True) + EPS)
    h2 = xn * g2_ref[...] * (1.0 + mod_ref[4:5, :]) + mod_ref[3:4, :]
    h2_ref[...] = h2

    h2_hi = h2.astype(BF16)
    h2_mid = (h2 - h2_hi.astype(F32)).astype(BF16)
    logits = _dot(jnp.concatenate([h2_hi, h2_mid, h2_hi], axis=1), wr_ref[...]) + br_ref[...]
    lane = lax.broadcasted_iota(jnp.int32, (TM, LANES), 1).astype(F32)

    def first_argmax(vals, vmax):
        return jnp.min(jnp.where(vals == vmax, lane, float(LANES)), axis=-1, keepdims=True)

    glog = jnp.where(lane < MOE_GROUPS, logits, NEG)
    gmax = jnp.max(glog, axis=-1, keepdims=True)
    gsel = first_argmax(glog, gmax)
    p_g = 1.0 / jnp.sum(jnp.exp(glog - gmax), axis=-1, keepdims=True)
    lo = ROUTER_LANE0 + MOE_EPG * gsel
    elog = jnp.where(jnp.logical_and(lane >= lo, lane < lo + MOE_EPG), logits, NEG)
    v1 = jnp.max(elog, axis=-1, keepdims=True)
    i1 = first_argmax(elog, v1)
    elog2 = jnp.where(lane == i1, NEG, elog)
    v2 = jnp.max(elog2, axis=-1, keepdims=True)
    i2 = first_argmax(elog2, v2)
    t2 = jnp.exp(v2 - v1)
    w1 = p_g / (1.0 + t2)
    w2 = p_g * t2 / (1.0 + t2)

    oh1 = lane == i1
    oh2 = lane == i2
    both = jnp.where(jnp.logical_or(oh1, oh2), 1.0, 0.0)
    r = lax.broadcasted_iota(jnp.int32, (TM, TM), 0)
    c = lax.broadcasted_iota(jnp.int32, (TM, TM), 1)
    strict_lower = jnp.where(c < r, 1.0, 0.0).astype(BF16)
    before = _dot(strict_lower, both.astype(BF16)) + carry[0:1, :]
    r1 = jnp.sum(jnp.where(oh1, before, 0.0), axis=-1, keepdims=True)
    r2 = jnp.sum(jnp.where(oh2, before, 0.0), axis=-1, keepdims=True)
    carry[...] = carry[...] + jnp.sum(both, axis=0, keepdims=True)
    cnt_ref[...] = carry[...]

    e1 = i1 - ROUTER_LANE0
    e2 = i2 - ROUTER_LANE0
    route = jnp.zeros((TM, LANES), F32)
    for n, val in enumerate((e1, e2, w1, w2, r1, r2)):
        route = jnp.where(lane == float(n), val, route)
    route_ref[...] = route
    route_t = route.T
    for n, ref in ((0, e1_ref), (1, e2_ref), (4, r1_ref), (5, r2_ref)):
        ref[...] = route_t[n:n + 1, :].astype(jnp.int32)


def _combine_call(l_arr, x_p, x_l, h, y_f, y_b, z, u, oa_p, oa_l, mod6, cv_w, cv_b, cv_g, cv_beta, w_gates, b_gates,
                  ssd_norm_g, w_br_ssd, w_br_conv, w_br_attn, w_out, norm2_g, w_router, b_router):
    tok = lambda n: pl.BlockSpec((TM, n), lambda i, l: (i, 0))
    lay = lambda *shape: pl.BlockSpec((None,) + shape, lambda i, l: (l[0],) + (0,) * len(shape))
    grid_spec = pltpu.PrefetchScalarGridSpec(
        num_scalar_prefetch=1,
        grid=(N_TILES,),
        in_specs=_pair_specs(D) + [tok(D), tok(D), tok(D), tok(D)] + _conformer_conv_specs() + _pair_specs(DA_V) + [
            pl.BlockSpec((None, None, 6, D), lambda i, l: (l[0], _mod_row(i), 0, 0)),
            lay(CONV_W, CONV_CH), lay(1, CONV_CH), lay(1, CONV_CH), lay(1, CONV_CH),
            lay(D, 3 * D), lay(1, 3 * D), lay(1, D), lay(D, D), lay(CONV_CH, D), lay(DA_V, D), lay(D, D),
            lay(1, D), lay(3 * D, LANES), lay(1, LANES),
        ],
        out_specs=[tok(D), tok(D), tok(LANES), pl.BlockSpec((SUBLANES, LANES), lambda i, l: (0, 0))]
        + [pl.BlockSpec((None, 1, TM), lambda i, l: (i, 0, 0))] * 4,
        scratch_shapes=[pltpu.VMEM((SUBLANES, LANES), F32), pltpu.VMEM((TM + 2 * CV_HALO, CONV_CH), F32)],
    )
    return pl.pallas_call(
        _combine_kernel,
        grid_spec=grid_spec,
        out_shape=[jax.ShapeDtypeStruct((T, D), F32), jax.ShapeDtypeStruct((T, D), F32),
                   jax.ShapeDtypeStruct((T, LANES), F32), jax.ShapeDtypeStruct((SUBLANES, LANES), F32)]
        + [jax.ShapeDtypeStruct((N_TILES, 1, TM), jnp.int32)] * 4,
        compiler_params=_cparams(("arbitrary",)),
        name="branch_combine_router",
    )(l_arr, x_p, x_l, h, y_f, y_b, z, u, u, u, oa_p, oa_l, mod6, cv_w, cv_b, cv_g, cv_beta, w_gates, b_gates,
      ssd_norm_g, w_br_ssd, w_br_conv, w_br_attn, w_out, norm2_g, w_router, b_router)


ROW_DMA_GROUP = 64


def _for_each_row(fn):
    def body(t, carry):
        fn(t)
        return carry

    lax.fori_loop(0, TM, body, 0, unroll=ROW_DMA_GROUP)


DISPATCH_SLOTS = 3


def _dispatch_kernel(e1_ref, e2_ref, r1_ref, r2_ref, ps_ref, pe_ref, h2_ref, xs_ref, zbuf, hbuf, in_sem, sem, zsem):
    i = pl.program_id(0)
    base = i * TM
    slot = i % DISPATCH_SLOTS

    def fetch(tile, s):
        return pltpu.make_async_copy(h2_ref.at[pl.ds(pl.multiple_of(tile * TM, TM), TM), :], hbuf.at[s], in_sem.at[s])

    def wait_rows(s):
        for _ in range(2):
            pltpu.make_async_copy(hbuf.at[s], xs_ref.at[pl.ds(0, TM), :], sem.at[s]).wait()

    @pl.when(i == 0)
    def _():
        fetch(0, 0).start()

    @pl.when(i == 0)
    def _():
        zbuf[...] = jnp.zeros_like(zbuf)

        def zero_fill(e):
            off = pl.multiple_of(pe_ref[e] - MOE_BLOCK, MOE_BLOCK)
            return pltpu.make_async_copy(zbuf, xs_ref.at[pl.ds(off, MOE_BLOCK), :], zsem)

        def unused_fill(b):
            off = pl.multiple_of(b * MOE_BLOCK, MOE_BLOCK)
            return pltpu.make_async_copy(zbuf, xs_ref.at[pl.ds(off, MOE_BLOCK), :], zsem)

        n_used = pe_ref[MOE_E - 1] // MOE_BLOCK
        for e in range(MOE_E):
            @pl.when(pe_ref[e] > ps_ref[e])
            def _():
                zero_fill(e).start()
        lax.fori_loop(n_used, N_SLOT_BLOCKS, lambda b, c: (unused_fill(b).start(), c)[1], 0)
        for e in range(MOE_E):
            @pl.when(pe_ref[e] > ps_ref[e])
            def _():
                zero_fill(e).wait()
        lax.fori_loop(n_used, N_SLOT_BLOCKS, lambda b, c: (unused_fill(b).wait(), c)[1], 0)

    @pl.when(i >= DISPATCH_SLOTS - 1)
    def _():
        wait_rows((i + 1) % DISPATCH_SLOTS)

    @pl.when(i + 1 < N_TILES)
    def _():
        fetch(i + 1, (i + 1) % DISPATCH_SLOTS).start()

    fetch(i, slot).wait()

    def issue(t):
        for prio, (e_ref, r_ref) in enumerate(((e1_ref, r1_ref), (e2_ref, r2_ref))):
            dest = ps_ref[e_ref[base + t]] + r_ref[base + t]
            pltpu.make_async_copy(hbuf.at[slot, pl.ds(t, 1), :], xs_ref.at[pl.ds(dest, 1), :],
                                  sem.at[slot]).start(priority=prio)

    _for_each_row(issue)

    @pl.when(i == N_TILES - 1)
    def _():
        for back in range(DISPATCH_SLOTS - 1):
            wait_rows((i - back) % DISPATCH_SLOTS)


def _dispatch_call(e1, e2, r1, r2, pstart, pend, h2):
    grid_spec = pltpu.PrefetchScalarGridSpec(
        num_scalar_prefetch=6,
        grid=(N_TILES,),
        in_specs=[pl.BlockSpec(memory_space=pl.ANY)],
        out_specs=pl.BlockSpec(memory_space=pl.ANY),
        scratch_shapes=[pltpu.VMEM((MOE_BLOCK, D), F32), pltpu.VMEM((DISPATCH_SLOTS, TM, D), F32),
                        pltpu.SemaphoreType.DMA((DISPATCH_SLOTS,)), pltpu.SemaphoreType.DMA((DISPATCH_SLOTS,)),
                        pltpu.SemaphoreType.DMA(())],
    )
    return pl.pallas_call(
        _dispatch_kernel,
        grid_spec=grid_spec,
        out_shape=jax.ShapeDtypeStruct((N_SLOTS, D), F32),
        compiler_params=_cparams(("arbitrary",)),
        name="moe_dispatch",
    )(e1, e2, r1, r2, pstart, pend, h2)


def _moe_kernel(l_ref, be_ref, nb_ref, xs_ref, wg_ref, wu_ref, wd_ref, ys_ref, wg_s, wu_s, wd_s):
    i = pl.program_id(0)
    prev = be_ref[jnp.maximum(i - 1, 0)]

    @pl.when(jnp.logical_or(i == 0, be_ref[i] != prev))
    def _():
        wg_s[...] = wg_ref[...].astype(BF16)
        wu_s[...] = wu_ref[...].astype(BF16)
        wd_s[...] = wd_ref[...].astype(BF16)

    @pl.when(i < nb_ref[0])
    def _():
        xb = xs_ref[...].astype(BF16)
        hid = _silu(_dot(xb, wg_s[...])) * _dot(xb, wu_s[...])
        ys_ref[...] = _dot(hid.astype(BF16), wd_s[...])

    @pl.when(i >= nb_ref[0])
    def _():
        ys_ref[...] = jnp.zeros_like(ys_ref)


def _moe_call(l_arr, block_expert, n_used, xs, w_gate, w_up, w_down):
    wspec = lambda a, b: pl.BlockSpec((None, None, a, b), lambda i, l, be, nb: (l[0], be[i], 0, 0))
    grid_spec = pltpu.PrefetchScalarGridSpec(
        num_scalar_prefetch=3,
        grid=(N_SLOT_BLOCKS,),
        in_specs=[pl.BlockSpec((MOE_BLOCK, D), lambda i, l, be, nb: (jnp.minimum(i, nb[0] - 1), 0)),
                  wspec(D, MOE_HIDDEN), wspec(D, MOE_HIDDEN), wspec(MOE_HIDDEN, D)],
        out_specs=pl.BlockSpec((MOE_BLOCK, D), lambda i, l, be, nb: (i, 0)),
        scratch_shapes=[pltpu.VMEM((D, MOE_HIDDEN), BF16), pltpu.VMEM((D, MOE_HIDDEN), BF16),
                        pltpu.VMEM((MOE_HIDDEN, D), BF16)],
    )
    return pl.pallas_call(
        _moe_kernel,
        grid_spec=grid_spec,
        out_shape=jax.ShapeDtypeStruct((N_SLOTS, D), F32),
        compiler_params=_cparams(("arbitrary",)),
        name="moe_experts",
    )(l_arr, block_expert, n_used, xs, w_gate, w_up, w_down)


def _moe_combine_kernel(l_ref, e1_ref, e2_ref, r1_ref, r2_ref, ps_ref, x1_ref, route_ref, mod_ref, fg_ref, ys_ref,
                        op_ref, ol_ref, buf, sem, *, final):
    i = pl.program_id(0)
    slot = i % 2

    def gather_tile(tile, dst_slot):
        base = tile * TM

        def issue(t):
            for which, (e_ref, r_ref) in enumerate(((e1_ref, r1_ref), (e2_ref, r2_ref))):
                src = ps_ref[e_ref[base + t]] + r_ref[base + t]
                pltpu.make_async_copy(ys_ref.at[pl.ds(src, 1), :], buf.at[dst_slot, which, pl.ds(t, 1), :],
                                      sem.at[dst_slot]).start(priority=which)

        _for_each_row(issue)

    @pl.when(i == 0)
    def _():
        gather_tile(0, 0)

    @pl.when(i + 1 < N_TILES)
    def _():
        gather_tile(i + 1, 1 - slot)

    for which in range(2):
        pltpu.make_async_copy(ys_ref.at[pl.ds(0, TM), :], buf.at[slot, which], sem.at[slot]).wait()
    w1 = route_ref[:, 2:3]
    w2 = route_ref[:, 3:4]
    y = buf[slot, 0] * w1 + buf[slot, 1] * w2
    x2 = x1_ref[...] + mod_ref[5:6, :] * y
    if final:
        x2 = x2 * lax.rsqrt(jnp.mean(x2 * x2, axis=-1, keepdims=True) + EPS) * fg_ref[...]

    @pl.when(i < N_PROMPT_TILES)
    def _():
        op_ref[...] = x2

    @pl.when(i >= N_PROMPT_TILES)
    def _():
        ol_ref[...] = x2


def _moe_combine_call(l_arr, e1, e2, r1, r2, pstart, x1, route, mod6, final_g, ys, final):
    grid_spec = pltpu.PrefetchScalarGridSpec(
        num_scalar_prefetch=6,
        grid=(N_TILES,),
        in_specs=[
            pl.BlockSpec((TM, D), lambda i, *_: (i, 0)),
            pl.BlockSpec((TM, LANES), lambda i, *_: (i, 0)),
            pl.BlockSpec((None, None, 6, D), lambda i, l, *_: (l[0], _mod_row(i), 0, 0)),
            pl.BlockSpec((1, D), lambda i, *_: (0, 0)),
            pl.BlockSpec(memory_space=pl.ANY),
        ],
        out_specs=_pair_specs(D),
        scratch_shapes=[pltpu.VMEM((2, 2, TM, D), F32), pltpu.SemaphoreType.DMA((2,))],
    )
    return pl.pallas_call(
        functools.partial(_moe_combine_kernel, final=final),
        grid_spec=grid_spec,
        out_shape=[jax.ShapeDtypeStruct((T_P, D), F32), jax.ShapeDtypeStruct((T_L, D), F32)],
        compiler_params=_cparams(("arbitrary",)),
        name="moe_combine_final" if final else "moe_combine",
    )(l_arr, e1, e2, r1, r2, pstart, x1, route, mod6, final_g, ys)


def _rope_tables():
    n = LATENT_LEN
    rows = n // GRID_W
    row = jnp.repeat(jnp.arange(rows), GRID_W).astype(F32)
    col = jnp.tile(jnp.arange(GRID_W), rows).astype(F32)
    axis_dim = DA_HEAD_DIM // 2
    inv_freq = 1.0 / (ROPE_BASE ** (jnp.arange(0, axis_dim, 2, dtype=F32) / axis_dim))
    ar, ac = row[:, None] * inv_freq, col[:, None] * inv_freq
    cos64 = jnp.concatenate([jnp.cos(ar), jnp.cos(ar), jnp.cos(ac), jnp.cos(ac)], axis=1)
    sin64 = jnp.concatenate([-jnp.sin(ar), jnp.sin(ar), -jnp.sin(ac), jnp.sin(ac)], axis=1)
    reps = DA_QK // DA_HEAD_DIM
    cos = jnp.concatenate([jnp.tile(cos64, (1, reps)), jnp.ones((TM, DA_QK), F32)], axis=0)
    sin = jnp.concatenate([jnp.tile(sin64, (1, reps)), jnp.zeros((TM, DA_QK), F32)], axis=0)
    return cos, sin


def _pad_lanes(a, n=LANES):
    return jnp.pad(a, [(0, 0)] * (a.ndim - 1) + [(0, n - a.shape[-1])])


def kernel(x_prompt, x_sample, cache_k, cache_v, state_ssd, c, c_ctx, w_ada, b_ada, norm1_g, norm2_g, w_in, b_in,
           ssd_conv_w, ssd_conv_b, ssd_dt_bias, ssd_a_log, ssd_d, ssd_norm_g, w_br_ssd, cv_dw_w, cv_dw_b, cv_ln_g,
           cv_ln_b, w_br_conv, da_lambda, da_subln_g, w_br_attn, w_out, moe_w_group, moe_b_group, moe_w_expert,
           moe_b_expert, moe_w_gate, moe_w_up, moe_w_down, final_g):
    L = DEPTH
    x_p, x_l = x_prompt.reshape(T_P, D), x_sample.reshape(T_L, D)

    cvec = jnp.concatenate([c, c_ctx[None, :], jnp.zeros((16 - N_LATENT_SEQ - 1, D), F32)], axis=0)
    mod6 = _ada_call(cvec, w_ada, b_ada).reshape(L, 16, 6, D)

    o_z, o_xbc, o_dt = 0, D, D + SSD_XBC
    o_glu = o_dt + 2 * SSD_HEADS
    o_q = o_glu + 2 * CONV_CH
    o_k, o_v, o_g = o_q + DA_QK, o_q + 2 * DA_QK, o_q + 2 * DA_QK + DA_V

    def regroup(w):
        return jnp.concatenate([w[..., o_z:o_dt], _pad_lanes(w[..., o_dt:o_glu]), w[..., o_glu:o_g]], axis=-1)

    w_proj = regroup(w_in).astype(BF16)
    b_proj = regroup(b_in).reshape(L, 1, N_PROJ)
    w_gates = w_in[..., o_g:].astype(BF16)
    b_gates = b_in[..., o_g:].reshape(L, 1, 3 * D)
    cos_tab, sin_tab = _rope_tables()

    tabs_f, tabs_b = _ssd_tables(True), _ssd_tables(False)
    dt_bias = _pad_lanes(ssd_dt_bias.reshape(L, 1, 2 * SSD_HEADS))
    a_log = _pad_lanes(ssd_a_log.reshape(L, 1, 2 * SSD_HEADS))
    dskip = jnp.repeat(ssd_d, SSD_HEAD_DIM, axis=-1).reshape(L, 1, D)
    hp = np.arange(D) // SSD_HEAD_DIM
    e_f, e_b = (jnp.asarray(np.tile(np.arange(LANES)[:, None] == ho + hp[None, :], (3, 1)).astype(np.float32))
                .astype(BF16) for ho in (0, SSD_HEADS))
    h0t = jnp.transpose(state_ssd, (1, 0, 2, 5, 3, 4)).reshape(L, N_LATENT_SEQ, 2, SSD_STATE, D)
    h0t = jnp.concatenate([h0t, jnp.zeros((L, 1, 2, SSD_STATE, D), F32)], axis=1)

    w_router = _pad_lanes(jnp.concatenate([moe_w_group, moe_w_expert], axis=-1))
    wr_hi = w_router.astype(BF16)
    wr_mid = (w_router - wr_hi.astype(F32)).astype(BF16)
    w_router = jnp.concatenate([wr_hi, wr_hi, wr_mid], axis=1)
    b_router = _pad_lanes(jnp.concatenate([moe_b_group, moe_b_expert], axis=-1)).reshape(L, 1, LANES)
    w_br_ssd_b, w_br_conv_b = w_br_ssd.astype(BF16), w_br_conv.astype(BF16)
    w_br_attn_b, w_out_b = w_br_attn.astype(BF16), w_out.astype(BF16)
    r3 = lambda a: a.reshape(L, 1, a.shape[-1])

    ks_new, vs_new, ss_new = [], [], []
    for layer in range(L):
        l_arr = jnp.full((1,), layer, jnp.int32)
        h, z, xbc, dt, u, q, k, v, k_cache, v_cache = _inproj_call(l_arr, x_p, x_l, mod6, r3(norm1_g), w_proj, b_proj,
                                                                   cos_tab, sin_tab)
        y_f, st_f, xc = _ssd_call(l_arr, tabs_f, xbc, dt, ssd_conv_w, r3(ssd_conv_b), dt_bias, a_log, dskip,
                                  e_f, h0t, None, True)
        y_b, st_b = _ssd_call(l_arr, tabs_b, None, dt, None, None, dt_bias, a_log, None, e_b, h0t, xc, False)
        oa_p = _attn_call(l_arr, q, k, v, None, None, da_lambda, r3(da_subln_g), latent=False)
        oa_l = _attn_call(l_arr, q, k, v, cache_k, cache_v, da_lambda, r3(da_subln_g), latent=True)
        x1, h2, route, counts, e1, e2, r1, r2 = _combine_call(
            l_arr, x_p, x_l, h, y_f, y_b, z, u, oa_p, oa_l, mod6, cv_dw_w, r3(cv_dw_b), r3(cv_ln_g), r3(cv_ln_b),
            w_gates, b_gates, r3(ssd_norm_g), w_br_ssd_b, w_br_conv_b, w_br_attn_b, w_out_b, r3(norm2_g),
            w_router, b_router)
        e1, e2, r1, r2 = (a.reshape(T) for a in (e1, e2, r1, r2))
        cnt = counts[0, ROUTER_LANE0:ROUTER_LANE0 + MOE_E].astype(jnp.int32)
        pcnt = (cnt + MOE_BLOCK - 1) // MOE_BLOCK * MOE_BLOCK
        pend = jnp.cumsum(pcnt)
        pstart = pend - pcnt
        blk0 = jnp.arange(N_SLOT_BLOCKS, dtype=jnp.int32) * MOE_BLOCK
        block_expert = jnp.minimum(jnp.sum((pend[None, :] <= blk0[:, None]).astype(jnp.int32), axis=1), MOE_E - 1)
        n_used = pend[-1:] // MOE_BLOCK

        xs = _dispatch_call(e1, e2, r1, r2, pstart, pend, h2)
        ys = _moe_call(l_arr, block_expert, n_used, xs, moe_w_gate, moe_w_up, moe_w_down)
        x_p, x_l = _moe_combine_call(l_arr, e1, e2, r1, r2, pstart, x1, route, mod6, final_g.reshape(1, D), ys,
                                     layer == L - 1)

        ks_new.append(k_cache)
        vs_new.append(v_cache)
        s = jnp.stack([st_f[:N_PROMPT_SEQ], st_b[:N_PROMPT_SEQ]], axis=1)
        s = s.reshape(N_PROMPT_SEQ, 2, SSD_STATE, SSD_HEADS, SSD_HEAD_DIM)
        ss_new.append(jnp.transpose(s, (0, 1, 3, 4, 2)))

    y_prompt = x_p.reshape(N_PROMPT_SEQ, PROMPT_LEN, D)
    y_sample = x_l.reshape(N_LATENT_SEQ, LATENT_LEN, D)
    return (y_prompt, y_sample, jnp.stack(ks_new, axis=1), jnp.stack(vs_new, axis=1), jnp.stack(ss_new, axis=1))
```
